```python
import math
import jax, jax.numpy as jnp
from jax import lax
import numpy as np

D_MODEL = 1024
BATCH = 4
SEQ = 4096
DEPTH = 4
DEC_BATCH = 32
DEC_SEQ = 1
PAST_LEN = 8192
PAGE_SIZE = 128

F32 = jnp.float32
HEAD_DIM = 64
POOL_WINDOWS = (2, 4, 8, 16)
POOL_GROUPS = len(POOL_WINDOWS)
POOL_DIM = D_MODEL // 4
POOL_GROUP = POOL_DIM // POOL_GROUPS
POOL_BUF = max(POOL_WINDOWS) - 1
NSA_DIM = D_MODEL - POOL_DIM
NSA_HEADS = NSA_DIM // HEAD_DIM
NSA_KV_HEADS = 4
NSA_REP = NSA_HEADS // NSA_KV_HEADS
KV_DIM = NSA_KV_HEADS * HEAD_DIM
CMP_LEN = 32
CMP_STRIDE = 16
SLC_LEN = 64
SLC_TOP = 16
WINDOW = 512
QBLOCK = 128
IN_DIM_EVEN = POOL_DIM + NSA_DIM + 6 * KV_DIM + 3 * NSA_HEADS
RWKV_HEAD = 64
RWKV_HEADS = D_MODEL // RWKV_HEAD
DECAY_LORA = 64
AAA_LORA = 64
MV_LORA = 32
GATE_LORA = 128
LN_X_EPS = 64e-5
D_FF = 4 * D_MODEL
PLE_DIM = 256
RMS_EPS = 1e-6
N_EVEN = (DEPTH + 1) // 2
N_ODD = DEPTH // 2
NEG_INF = -1e30
SEL_BIG = 1e9

kernel_name = 'pool_nsa_rwkv7_hybrid_step'


def rms_norm(x, g):
    xf = x.astype(F32)
    y = xf * lax.rsqrt(jnp.mean(xf * xf, axis=-1, keepdims=True) + RMS_EPS)
    return (y * g.astype(F32)).astype(x.dtype)


def alibi_slopes(n):
    p = 2 ** int(math.floor(math.log2(n)))
    s = [2.0 ** (-8.0 * (i + 1) / p) for i in range(p)]
    if p < n:
        s += [2.0 ** (-8.0 * (i + 1) / (2 * p)) for i in range(0, 2 * p, 2)][: n - p]
    return np.asarray(s, dtype=np.float32)


def masked_softmax(s, valid):
    s = jnp.where(valid, s, NEG_INF)
    e = jnp.exp(s - jnp.max(s, axis=-1, keepdims=True)) * valid
    return e / jnp.maximum(jnp.sum(e, axis=-1, keepdims=True), 1e-30)


def gqa_attend(q, k, v, dist, valid, slopes):
    s = jnp.einsum('...tgrd,...kgd->...grtk', q, k) * (HEAD_DIM ** -0.5)
    s = s - slopes[:, :, None, None] * dist[..., None, None, :, :]
    p = masked_softmax(s, valid[..., None, None, :, :])
    return jnp.einsum('...grtk,...kgd->...tgrd', p, v), p


def compress(x, pe, w):
    n, l, g, d = x.shape
    ratio = CMP_LEN // CMP_STRIDE
    nh = l // CMP_STRIDE
    n_cmp = nh - ratio + 1
    halves = x[:, : nh * CMP_STRIDE].reshape(n, nh, CMP_STRIDE, g, d)
    out = 0.0
    for o in range(ratio):
        sl = slice(o * CMP_STRIDE, (o + 1) * CMP_STRIDE)
        out = out + jnp.einsum('nilgd,lde->nige', halves[:, o:o + n_cmp] + pe[sl, None, :], w[sl])
    return out


def nsa_global(q, k_all, v_all, q_pos, kc_gain, cmp_pe, cmp_w, slopes):
    n, t = q.shape[:2]
    l = k_all.shape[1]
    k_all = k_all.astype(F32)
    v_all = v_all.astype(F32)
    kc = rms_norm(compress(k_all[:, :, 0], cmp_pe[0], cmp_w[0]), kc_gain)
    vc = compress(v_all[:, :, 0], cmp_pe[1], cmp_w[1])
    n_cmp = kc.shape[1]
    cmp_end = jnp.arange(n_cmp) * CMP_STRIDE + (CMP_LEN - 1)
    dist = q_pos[:, None] - cmp_end[None, :]
    o_cmp, p_cmp = gqa_attend(q, kc, vc, dist.astype(F32), dist >= 0, slopes)
    imp = jnp.sum(p_cmp, axis=2)
    n_slc = -(-l // SLC_LEN)
    per = SLC_LEN // CMP_STRIDE
    lead = CMP_LEN // CMP_STRIDE - 1
    cidx = jnp.arange(n_slc)[:, None] * per + jnp.arange(-lead, per)[None, :]
    cval = (cidx >= 0) & (cidx < n_cmp)
    imp = jnp.where(cval, jnp.take(imp, jnp.clip(cidx, 0, n_cmp - 1), axis=-1), 0.0).sum(-1)
    blk = jnp.arange(n_slc)[None, :]
    cur = (q_pos // SLC_LEN)[:, None]
    causal = blk * SLC_LEN <= q_pos[:, None]
    forced = (blk == 0) | (blk == cur) | (blk == cur - 1)
    score = jnp.where(forced, SEL_BIG, jnp.where(causal, imp, -SEL_BIG))
    n_top = min(SLC_TOP, n_slc)
    _, sel = lax.top_k(score, n_top)
    pad = n_slc * SLC_LEN - l

    def to_blocks(z):
        z = jnp.pad(z, ((0, 0), (0, pad), (0, 0), (0, 0)))
        return z.reshape(n, n_slc, SLC_LEN, NSA_KV_HEADS, HEAD_DIM).transpose(0, 3, 1, 2, 4)

    ksb = to_blocks(k_all[:, :, 1])
    vsb = to_blocks(v_all[:, :, 1])
    qb = QBLOCK if t % QBLOCK == 0 else t
    nq = t // qb
    g_idx = jnp.arange(NSA_KV_HEADS)[None, :, None]

    def one(args):
        b, q_blk, pos_blk, sel_blk = args
        kg = ksb[b][g_idx, sel_blk]
        vg = vsb[b][g_idx, sel_blk]
        kpos = sel_blk[..., None] * SLC_LEN + jnp.arange(SLC_LEN)
        d_ = pos_blk[:, None, None, None] - kpos
        s = jnp.einsum('tgrd,tgnkd->tgrnk', q_blk, kg) * (HEAD_DIM ** -0.5)
        s = s - slopes[None, :, :, None, None] * d_[:, :, None].astype(F32)
        s = s.reshape(qb, NSA_KV_HEADS, NSA_REP, n_top * SLC_LEN)
        valid = (d_ >= 0).reshape(qb, NSA_KV_HEADS, 1, n_top * SLC_LEN)
        p = masked_softmax(s, valid).reshape(qb, NSA_KV_HEADS, NSA_REP, n_top, SLC_LEN)
        return jnp.einsum('tgrnk,tgnkd->tgrd', p, vg)

    xs = (jnp.repeat(jnp.arange(n), nq),
          q.reshape(n * nq, qb, NSA_KV_HEADS, NSA_REP, HEAD_DIM),
          jnp.tile(q_pos.reshape(nq, qb), (n, 1)),
          sel.transpose(0, 2, 1, 3).reshape(n * nq, qb, NSA_KV_HEADS, n_top))
    o_slc = lax.map(one, xs).reshape(n, t, NSA_KV_HEADS, NSA_REP, HEAD_DIM)
    return o_cmp, o_slc


def window_prompt(q, kw, vw, slopes):
    n, t = q.shape[:2]
    nb = t // QBLOCK
    span = WINDOW + QBLOCK
    kp = jnp.pad(kw, ((0, 0), (WINDOW, 0), (0, 0), (0, 0)))
    vp = jnp.pad(vw, ((0, 0), (WINDOW, 0), (0, 0), (0, 0)))
    q_blocks = jnp.moveaxis(q.reshape(n, nb, QBLOCK, NSA_KV_HEADS, NSA_REP, HEAD_DIM), 1, 0)

    def one(args):
        b, q_blk = args
        start = b * QBLOCK
        k_blk = lax.dynamic_slice_in_dim(kp, start, span, axis=1)
        v_blk = lax.dynamic_slice_in_dim(vp, start, span, axis=1)
        qpos = start + jnp.arange(QBLOCK)
        kpos = start - WINDOW + jnp.arange(span)
        dist = qpos[:, None] - kpos[None, :]
        valid = (dist >= 0) & (dist < WINDOW) & (kpos[None, :] >= 0)
        o, _ = gqa_attend(q_blk, k_blk, v_blk, dist.astype(F32), valid, slopes)
        return o

    o = lax.map(one, (jnp.arange(nb), q_blocks))
    return jnp.moveaxis(o, 0, 1).reshape(n, t, NSA_KV_HEADS, NSA_REP, HEAD_DIM)


def window_sample(q, k, v, q_pos, k_start, slopes):
    kpos = k_start + jnp.arange(k.shape[1])
    dist = q_pos[:, None] - kpos[None, :]
    o, _ = gqa_attend(q, k, v, dist.astype(F32), (dist >= 0) & (dist < WINDOW), slopes)
    return o


def pool_mix(u, buf, pos0, w, scale):
    n, t, c = u.shape
    uf = u.astype(F32)
    prev = jnp.zeros((n, POOL_BUF, c), F32) if buf is None else buf.astype(F32)
    ext = jnp.concatenate([prev, uf], axis=1)
    cs = jnp.concatenate([jnp.zeros((n, 1, c), F32), jnp.cumsum(ext, axis=1)], axis=1)
    end = cs[:, POOL_BUF + 1:]
    pos = pos0 + jnp.arange(t)
    outs = []
    for gi, win in enumerate(POOL_WINDOWS):
        ch = slice(gi * POOL_GROUP, (gi + 1) * POOL_GROUP)
        start = cs[:, POOL_BUF + 1 - win: POOL_BUF + 1 - win + t, ch]
        cnt = jnp.minimum(win, pos + 1).astype(F32)[None, :, None]
        outs.append((end[:, :, ch] - start) / cnt - uf[:, :, ch])
    mixed = jnp.stack(outs, axis=2)
    y = jnp.einsum('ntgc,gcd->ntgd', mixed, w.astype(F32)).reshape(n, t, c) * scale.astype(F32)
    return y.astype(u.dtype), ext[:, -POOL_BUF:].astype(u.dtype)


def even_mixer(xn, pos0, pool_buf, kv_past, win_buf, wbuf, w_in, pool_w, pool_scale, q_gain, k_gain,
               cmp_pe, cmp_w, w_out, slopes):
    n, t, _ = xn.shape
    z = xn @ w_in
    u, q, kv, gl = jnp.split(z, [POOL_DIM, POOL_DIM + NSA_DIM, POOL_DIM + NSA_DIM + 6 * KV_DIM], axis=-1)
    y_pool, pool_state = pool_mix(u, pool_buf, pos0, pool_w, pool_scale)
    q = rms_norm(q.reshape(n, t, NSA_KV_HEADS, NSA_REP, HEAD_DIM), q_gain).astype(F32)
    kv = kv.reshape(n, t, 6, NSA_KV_HEADS, HEAD_DIM)
    k_s = rms_norm(kv[:, :, 2], k_gain[1])
    k_w = rms_norm(kv[:, :, 4], k_gain[2])
    v_w = kv[:, :, 5]
    new_k = jnp.stack([kv[:, :, 0], k_s], axis=2)
    new_v = jnp.stack([kv[:, :, 1], kv[:, :, 3]], axis=2)
    q_pos = pos0 + jnp.arange(t)
    if kv_past is None:
        k_all, v_all = new_k, new_v
        o_win = window_prompt(q, k_w.astype(F32), v_w.astype(F32), slopes)
        zpad = jnp.zeros((n, wbuf, NSA_KV_HEADS, HEAD_DIM), k_w.dtype)
        win_k = jnp.concatenate([zpad, k_w], axis=1)[:, -wbuf:]
        win_v = jnp.concatenate([zpad, v_w], axis=1)[:, -wbuf:]
    else:
        k_all = jnp.concatenate([kv_past[0], new_k.astype(kv_past[0].dtype)], axis=1)
        v_all = jnp.concatenate([kv_past[1], new_v.astype(kv_past[1].dtype)], axis=1)
        wk_all = jnp.concatenate([win_buf[0], k_w.astype(win_buf[0].dtype)], axis=1)
        wv_all = jnp.concatenate([win_buf[1], v_w.astype(win_buf[1].dtype)], axis=1)
        o_win = window_sample(q, wk_all.astype(F32), wv_all.astype(F32), q_pos, pos0 - wbuf, slopes)
        win_k = wk_all[:, -wbuf:]
        win_v = wv_all[:, -wbuf:]
    o_cmp, o_slc = nsa_global(q, k_all, v_all, q_pos, k_gain[0], cmp_pe, cmp_w, slopes)
    gates = jax.nn.sigmoid(gl.astype(F32)).reshape(n, t, NSA_KV_HEADS, NSA_REP, 3)
    o = gates[..., 0:1] * o_cmp + gates[..., 1:2] * o_slc + gates[..., 2:3] * o_win
    mix = jnp.concatenate([y_pool, o.reshape(n, t, NSA_DIM).astype(xn.dtype)], axis=-1)
    return mix @ w_out, new_k, new_v, win_k, win_v, pool_state


def wkv7_scan(s0, r, w, k, v, kk, a):
    def step(S, inp):
        r_t, w_t, k_t, v_t, kk_t, a_t = inp
        sa = jnp.einsum('nhij,nhj->nhi', S, -kk_t)
        S = S * w_t[:, :, None, :] + sa[..., None] * (kk_t * a_t)[:, :, None, :] + v_t[..., None] * k_t[:, :, None, :]
        return S, jnp.einsum('nhij,nhj->nhi', S, r_t)
    xs = tuple(jnp.swapaxes(z, 0, 1) for z in (r, w, k, v, kk, a))
    s_fin, out = lax.scan(step, s0, xs)
    return jnp.swapaxes(out, 0, 1), s_fin


def rwkv7_mixer(xn, shift_prev, s0, v_first, vres, mu, wr, wk, wv, wo, w0, w1, w2,
                a0, a1, a2, g1, g2, k_k, k_a, r_k, ln_w, ln_b):
    n, t, d = xn.shape
    xf = xn.astype(F32)
    prev = jnp.zeros((n, 1, d), F32) if shift_prev is None else shift_prev[:, None].astype(F32)
    xx = jnp.concatenate([prev, xf[:, :-1]], axis=1) - xf
    xr, xw, xk, xv, xa, xg = [xf + xx * mu[j] for j in range(6)]
    r = xr @ wr
    k = xk @ wk
    v = xv @ wv
    w_log = -jax.nn.softplus(-(w0 + jnp.tanh(xw @ w1) @ w2)) - 0.5
    decay = jnp.exp(-jnp.exp(w_log))
    if vres is None:
        v_first = v
    else:
        v0, v1, v2 = vres
        v = v + (v_first - v) * jax.nn.sigmoid(v0 + (xv @ v1) @ v2)
    a = jax.nn.sigmoid(a0 + (xa @ a1) @ a2)
    g = jax.nn.sigmoid(xg @ g1) @ g2
    hs = lambda z: z.reshape(n, t, RWKV_HEADS, RWKV_HEAD)
    kk = hs(k * k_k)
    kk = kk / jnp.maximum(jnp.sqrt(jnp.sum(kk * kk, axis=-1, keepdims=True)), 1e-12)
    k = k * (1.0 + (a - 1.0) * k_a)
    s_init = jnp.zeros((n, RWKV_HEADS, RWKV_HEAD, RWKV_HEAD), F32) if s0 is None else s0.astype(F32)
    o, s_fin = wkv7_scan(s_init, hs(r), hs(decay), hs(k), hs(v), kk, hs(a))
    mean = jnp.mean(o, axis=-1, keepdims=True)
    var = jnp.mean(jnp.square(o - mean), axis=-1, keepdims=True)
    o = ((o - mean) * lax.rsqrt(var + LN_X_EPS)).reshape(n, t, d) * ln_w + ln_b
    o = o + (jnp.sum(hs(r) * hs(k) * r_k, axis=-1, keepdims=True) * hs(v)).reshape(n, t, d)
    y = (o * g) @ wo
    return y.astype(xn.dtype), xf[:, -1], s_fin, v_first


def layer_tail(h, p, g_mlp, w1, w2, g_ple, w_pe, w_pg):
    u = rms_norm(h, g_mlp)
    h = h + jnp.square(jax.nn.relu(u @ w1)) @ w2
    gate = jax.nn.sigmoid((rms_norm(h, g_ple) @ w_pg).astype(F32))
    return h + (gate * (p @ w_pe).astype(F32)).astype(h.dtype)


def setup_inputs(seed: int = 0) -> dict:
    key = jax.random.key(seed)
    ks = iter(jax.random.split(key, 80))
    nrm = lambda shape, scale=1.0: jax.random.normal(next(ks), shape, F32) * scale
    gain = lambda shape: 1.0 + 0.05 * jax.random.normal(next(ks), shape, F32)
    uni = lambda shape, lo, hi: jax.random.uniform(next(ks), shape, F32, lo, hi)
    d = D_MODEL
    n_pages = PAST_LEN // PAGE_SIZE
    n_phys = (5 * DEC_BATCH * n_pages) // 4
    wbuf = min(WINDOW, PAST_LEN)
    perm = jax.random.permutation(next(ks), n_phys)[: DEC_BATCH * n_pages]
    page_table = perm.reshape(DEC_BATCH, n_pages).astype(jnp.int32)
    return {
        'x_prompt': nrm((BATCH, SEQ, d)),
        'x_sample': nrm((DEC_BATCH, DEC_SEQ, d)),
        'cache_k': nrm((N_EVEN, n_phys, PAGE_SIZE, 2, NSA_KV_HEADS, HEAD_DIM)),
        'cache_v': nrm((N_EVEN, n_phys, PAGE_SIZE, 2, NSA_KV_HEADS, HEAD_DIM)),
        'page_table': page_table,
        'state_win_k': nrm((N_EVEN, DEC_BATCH, wbuf, NSA_KV_HEADS, HEAD_DIM)),
        'state_win_v': nrm((N_EVEN, DEC_BATCH, wbuf, NSA_KV_HEADS, HEAD_DIM)),
        'state_pool': nrm((N_EVEN, DEC_BATCH, POOL_BUF, POOL_DIM)),
        'state_shift': nrm((N_ODD, DEC_BATCH, d)),
        'state_wkv': nrm((N_ODD, DEC_BATCH, RWKV_HEADS, RWKV_HEAD, RWKV_HEAD), 0.3),
        'p_prompt': nrm((DEPTH, BATCH, SEQ, PLE_DIM)),
        'p_sample': nrm((DEPTH, DEC_BATCH, DEC_SEQ, PLE_DIM)),
        'norm_mix': gain((DEPTH, d)),
        'norm_mlp': gain((DEPTH, d)),
        'norm_ple': gain((DEPTH, d)),
        'mlp_w1': nrm((DEPTH, d, D_FF), d ** -0.5),
        'mlp_w2': nrm((DEPTH, D_FF, d), D_FF ** -0.5),
        'ple_proj': nrm((DEPTH, PLE_DIM, d), PLE_DIM ** -0.5),
        'ple_gate': nrm((DEPTH, d, d), d ** -0.5),
        'even_w_in': nrm((N_EVEN, d, IN_DIM_EVEN), d ** -0.5),
        'even_w_out': nrm((N_EVEN, POOL_DIM + NSA_DIM, d), (POOL_DIM + NSA_DIM) ** -0.5),
        'pool_w': nrm((N_EVEN, POOL_GROUPS, POOL_GROUP, POOL_GROUP), POOL_GROUP ** -0.5),
        'pool_scale': gain((N_EVEN, POOL_DIM)),
        'q_gain': gain((N_EVEN, HEAD_DIM)),
        'k_gain': gain((N_EVEN, 3, HEAD_DIM)),
        'cmp_pe': nrm((N_EVEN, 2, CMP_LEN, HEAD_DIM), 0.1),
        'cmp_w': nrm((N_EVEN, 2, CMP_LEN, HEAD_DIM, HEAD_DIM), (CMP_LEN * HEAD_DIM) ** -0.5),
        'rwkv_mu': uni((N_ODD, 6, d), 0.0, 1.0),
        'rwkv_wr': nrm((N_ODD, d, d), d ** -0.5),
        'rwkv_wk': nrm((N_ODD, d, d), d ** -0.5),
        'rwkv_wv': nrm((N_ODD, d, d), d ** -0.5),
        'rwkv_wo': nrm((N_ODD, d, d), d ** -0.5),
        'rwkv_w0': uni((N_ODD, d), -5.0, 0.0),
        'rwkv_w1': nrm((N_ODD, d, DECAY_LORA), 0.1 * d ** -0.5),
        'rwkv_w2': nrm((N_ODD, DECAY_LORA, d), 0.1 * DECAY_LORA ** -0.5),
        'rwkv_a0': nrm((N_ODD, d), 0.1),
        'rwkv_a1': nrm((N_ODD, d, AAA_LORA), 0.1 * d ** -0.5),
        'rwkv_a2': nrm((N_ODD, AAA_LORA, d), 0.1 * AAA_LORA ** -0.5),
        'rwkv_v0': nrm((N_ODD - 1, d), 0.1),
        'rwkv_v1': nrm((N_ODD - 1, d, MV_LORA), 0.1 * d ** -0.5),
        'rwkv_v2': nrm((N_ODD - 1, MV_LORA, d), 0.1 * MV_LORA ** -0.5),
        'rwkv_g1': nrm((N_ODD, d, GATE_LORA), d ** -0.5),
        'rwkv_g2': nrm((N_ODD, GATE_LORA, d), GATE_LORA ** -0.5),
        'rwkv_kk': gain((N_ODD, d)),
        'rwkv_ka': gain((N_ODD, d)),
        'rwkv_rk': nrm((N_ODD, RWKV_HEADS, RWKV_HEAD), 0.1),
        'rwkv_lnw': gain((N_ODD, d)),
        'rwkv_lnb': nrm((N_ODD, d), 0.02),
    }


def reference(x_prompt, x_sample, cache_k, cache_v, page_table, state_win_k, state_win_v, state_pool,
              state_shift, state_wkv, p_prompt, p_sample,
              norm_mix, norm_mlp, norm_ple, mlp_w1, mlp_w2, ple_proj, ple_gate,
              even_w_in, even_w_out, pool_w, pool_scale, q_gain, k_gain, cmp_pe, cmp_w,
              rwkv_mu, rwkv_wr, rwkv_wk, rwkv_wv, rwkv_wo, rwkv_w0, rwkv_w1, rwkv_w2,
              rwkv_a0, rwkv_a1, rwkv_a2, rwkv_v0, rwkv_v1, rwkv_v2, rwkv_g1, rwkv_g2,
              rwkv_kk, rwkv_ka, rwkv_rk, rwkv_lnw, rwkv_lnb):
    slopes = jnp.asarray(alibi_slopes(NSA_HEADS)).reshape(NSA_KV_HEADS, NSA_REP)
    dec_b, n_pages = page_table.shape
    past_len = n_pages * PAGE_SIZE
    wbuf = state_win_k.shape[2]
    h_p, h_s = x_prompt, x_sample
    vf_p = None
    vf_s = None
    nk_p, nv_p, nk_s, nv_s = [], [], [], []
    wk_p, wv_p, wk_s, wv_s = [], [], [], []
    pl_p, pl_s, sh_p, sh_s, st_p, st_s = [], [], [], [], [], []
    for i in range(DEPTH):
        xp = rms_norm(h_p, norm_mix[i])
        xs = rms_norm(h_s, norm_mix[i])
        if i % 2 == 0:
            e = i // 2
            ew = (even_w_in[e], pool_w[e], pool_scale[e], q_gain[e], k_gain[e], cmp_pe[e], cmp_w[e],
                  even_w_out[e], slopes)
            yp, a1_, a2_, a3_, a4_, a5_ = even_mixer(xp, 0, None, None, None, wbuf, *ew)
            past_k = cache_k[e][page_table].reshape(dec_b, past_len, 2, NSA_KV_HEADS, HEAD_DIM)
            past_v = cache_v[e][page_table].reshape(dec_b, past_len, 2, NSA_KV_HEADS, HEAD_DIM)
            ys, b1_, b2_, b3_, b4_, b5_ = even_mixer(xs, past_len, state_pool[e], (past_k, past_v),
                                                     (state_win_k[e], state_win_v[e]), wbuf, *ew)
            nk_p.append(a1_); nv_p.append(a2_); wk_p.append(a3_); wv_p.append(a4_); pl_p.append(a5_)
            nk_s.append(b1_); nv_s.append(b2_); wk_s.append(b3_); wv_s.append(b4_); pl_s.append(b5_)
        else:
            o = i // 2
            vres = None if o == 0 else (rwkv_v0[o - 1], rwkv_v1[o - 1], rwkv_v2[o - 1])
            rw = (rwkv_mu[o], rwkv_wr[o], rwkv_wk[o], rwkv_wv[o], rwkv_wo[o], rwkv_w0[o], rwkv_w1[o], rwkv_w2[o],
                  rwkv_a0[o], rwkv_a1[o], rwkv_a2[o], rwkv_g1[o], rwkv_g2[o], rwkv_kk[o], rwkv_ka[o], rwkv_rk[o],
                  rwkv_lnw[o], rwkv_lnb[o])
            yp, c1_, c2_, vf_p = rwkv7_mixer(xp, None, None, vf_p, vres, *rw)
            ys, d1_, d2_, vf_s = rwkv7_mixer(xs, state_shift[o], state_wkv[o], vf_s, vres, *rw)
            sh_p.append(c1_); st_p.append(c2_); sh_s.append(d1_); st_s.append(d2_)
        h_p = h_p + yp
        h_s = h_s + ys
        h_p = layer_tail(h_p, p_prompt[i], norm_mlp[i], mlp_w1[i], mlp_w2[i], norm_ple[i], ple_proj[i], ple_gate[i])
        h_s = layer_tail(h_s, p_sample[i], norm_mlp[i], mlp_w1[i], mlp_w2[i], norm_ple[i], ple_proj[i], ple_gate[i])
    y_prompt = h_p
    y_sample = h_s
    new_k_prompt = jnp.stack(nk_p)
    new_v_prompt = jnp.stack(nv_p)
    new_k_sample = jnp.stack(nk_s)
    new_v_sample = jnp.stack(nv_s)
    win_k_prompt = jnp.stack(wk_p)
    win_v_prompt = jnp.stack(wv_p)
    win_k_sample = jnp.stack(wk_s)
    win_v_sample = jnp.stack(wv_s)
    pool_prompt = jnp.stack(pl_p)
    pool_sample = jnp.stack(pl_s)
    shift_prompt = jnp.stack(sh_p)
    shift_sample = jnp.stack(sh_s)
    wkv_prompt = jnp.stack(st_p)
    wkv_sample = jnp.stack(st_s)
    return (y_prompt, y_sample, new_k_prompt, new_v_prompt, new_k_sample, new_v_sample,
            win_k_prompt, win_v_prompt, win_k_sample, win_v_sample, pool_prompt, pool_sample,
            shift_prompt, shift_sample, wkv_prompt, wkv_sample)
```

```python
import functools
import math

import numpy as np
import jax
import jax.numpy as jnp
from jax import lax
from jax.experimental import pallas as pl
from jax.experimental.pallas import tpu as pltpu

F32 = jnp.float32
_MX = jnp.bfloat16

HD = 64
POOL_DIM = 256
POOL_WINDOWS = (2, 4, 8, 16)
POOL_BUF = 15
KVH = 4
REP = 3
NSA_DIM = KVH * REP * HD
KV_DIM = KVH * HD
CMP_LEN, CMP_STRIDE = 32, 16
SLC_LEN, SLC_TOP = 64, 16
WINDOW = 512
QBLK = 128
PAGE = 128
RMS_EPS = 1e-6
LN_X_EPS = 64e-5
NEG = -1e30
BIG = 1e9
LORA_PAD = 128
IN_PAD = 2688
VMEM_LIMIT = 56 * 1024 * 1024


def _cp(*sem):
    return pltpu.CompilerParams(dimension_semantics=sem, vmem_limit_bytes=VMEM_LIMIT)


def _alibi_slopes(n):
    p = 2 ** int(math.floor(math.log2(n)))
    s = [2.0 ** (-8.0 * (i + 1) / p) for i in range(p)]
    if p < n:
        s += [2.0 ** (-8.0 * (i + 1) / (2 * p)) for i in range(0, 2 * p, 2)][: n - p]
    return np.asarray(s, dtype=np.float32)


def _rms(x, g):
    return x * lax.rsqrt(jnp.mean(x * x, axis=-1, keepdims=True) + RMS_EPS) * g


def _sigmoid(x):
    return 1.0 / (1.0 + jnp.exp(-x))


def _dot(a, b):
    return jnp.dot(a.astype(_MX), b.astype(_MX), preferred_element_type=F32)


def _dot_nt(a, b):
    return lax.dot_general(a.astype(_MX), b.astype(_MX), (((1,), (1,)), ((), ())),
                           preferred_element_type=F32)


def _dot2(a, b):
    hi = a.astype(_MX)
    lo = (a - hi.astype(F32)).astype(_MX)
    b = b.astype(_MX)
    return (jnp.dot(hi, b, preferred_element_type=F32) + jnp.dot(lo, b, preferred_element_type=F32))


def _dot2r(a, b):
    hi = b.astype(_MX)
    lo = (b - hi.astype(F32)).astype(_MX)
    a = a.astype(_MX)
    return (jnp.dot(a, hi, preferred_element_type=F32) + jnp.dot(a, lo, preferred_element_type=F32))


def _seg64_sum(y):
    left = lax.broadcasted_iota(jnp.int32, y.shape, 1) < HD
    sa = jnp.sum(jnp.where(left, y, 0.0), axis=-1, keepdims=True)
    sb = jnp.sum(jnp.where(left, 0.0, y), axis=-1, keepdims=True)
    return jnp.where(left, sa, sb)


def _head_rms_tile(zt, gain2):
    ms = _seg64_sum(zt * zt) * (1.0 / HD)
    return zt * lax.rsqrt(ms + RMS_EPS) * gain2


def _masked_softmax(s, valid):
    s = jnp.where(valid, s, NEG)
    e = jnp.where(valid, jnp.exp(s - jnp.max(s, axis=-1, keepdims=True)), 0.0)
    return e / jnp.maximum(jnp.sum(e, axis=-1, keepdims=True), 1e-30)


def _online_update(s, valid, v, m_ref, l_ref, acc_ref):
    s = jnp.where(valid, s, NEG)
    m_old = m_ref[...]
    m_new = jnp.maximum(m_old, jnp.max(s, axis=-1, keepdims=True))
    p = jnp.where(valid, jnp.exp(s - m_new), 0.0)
    alpha = jnp.exp(m_old - m_new)
    l_ref[...] = alpha * l_ref[...] + jnp.sum(p, axis=-1, keepdims=True)
    acc_ref[...] = alpha * acc_ref[...] + _dot(p, v)
    m_ref[...] = m_new


def _topk_mask(score, n_cand, n_top):
    col_id = lax.broadcasted_iota(jnp.int32, score.shape, 1)
    rank = jnp.zeros(score.shape, F32)
    for j0 in range(n_cand):
        cj = score[:, j0:j0 + 1]
        beats = jnp.where(cj > score, 1.0, jnp.where((cj == score) & (col_id > j0), 1.0, 0.0))
        rank = rank + beats
    return jnp.where((rank < n_top) & (col_id < n_cand), 1.0, 0.0)


def _norm_kernel(x_ref, g_ref, o_ref):
    o_ref[...] = _rms(x_ref[...], g_ref[...])


def _norm(x, g, tm):
    m, d = x.shape
    return pl.pallas_call(
        _norm_kernel, grid=(m // tm,),
        in_specs=[pl.BlockSpec((tm, d), lambda i: (i, 0)), pl.BlockSpec((1, d), lambda i: (0, 0))],
        out_specs=pl.BlockSpec((tm, d), lambda i: (i, 0)),
        out_shape=jax.ShapeDtypeStruct((m, d), F32), compiler_params=_cp("parallel"))(x, g)


def _mlp_kernel(h_ref, g_ref, w1_ref, w2_ref, o_ref, xn_ref):
    @pl.when(pl.program_id(1) == 0)
    def _():
        x = h_ref[...]
        xn_ref[...] = _rms(x, g_ref[...]).astype(_MX)
        o_ref[...] = x

    a = jnp.dot(xn_ref[...], w1_ref[...], preferred_element_type=F32)
    a = jnp.square(jnp.maximum(a, 0.0))
    o_ref[...] += _dot(a, w2_ref[...])


def _mlp(h, g, w1, w2, tm, tf):
    m, d = h.shape
    dff = w1.shape[1]
    return pl.pallas_call(
        _mlp_kernel, grid=(m // tm, dff // tf),
        in_specs=[pl.BlockSpec((tm, d), lambda i, j: (i, 0)), pl.BlockSpec((1, d), lambda i, j: (0, 0)),
                  pl.BlockSpec((d, tf), lambda i, j: (0, j)), pl.BlockSpec((tf, d), lambda i, j: (j, 0))],
        out_specs=pl.BlockSpec((tm, d), lambda i, j: (i, 0)),
        out_shape=jax.ShapeDtypeStruct((m, d), F32),
        scratch_shapes=[pltpu.VMEM((tm, d), _MX)],
        compiler_params=_cp("parallel", "arbitrary"))(h, g, w1, w2)


def _ple_kernel(h_ref, p_ref, g_ref, wg_ref, wp_ref, o_ref):
    h = h_ref[...]
    gate = _sigmoid(_dot(_rms(h, g_ref[...]), wg_ref[...]))
    o_ref[...] = h + gate * _dot(p_ref[...], wp_ref[...])


def _ple(h, p, g, wg, wp, tm):
    m, d = h.shape
    pd = p.shape[1]
    return pl.pallas_call(
        _ple_kernel, grid=(m // tm,),
        in_specs=[pl.BlockSpec((tm, d), lambda i: (i, 0)), pl.BlockSpec((tm, pd), lambda i: (i, 0)),
                  pl.BlockSpec((1, d), lambda i: (0, 0)), pl.BlockSpec((d, d), lambda i: (0, 0)),
                  pl.BlockSpec((pd, d), lambda i: (0, 0))],
        out_specs=pl.BlockSpec((tm, d), lambda i: (i, 0)),
        out_shape=jax.ShapeDtypeStruct((m, d), F32), compiler_params=_cp("parallel"))(h, p, g, wg, wp)


_SEG_U = 0
_SEG_Q = POOL_DIM
_SEG_KV = POOL_DIM + NSA_DIM
_SEG_GL = _SEG_KV + 6 * KV_DIM


def _inproj_kernel(x_ref, gn_ref, w_ref, qg_ref, ksg_ref, kwg_ref,
                   u_ref, q_ref, nk_ref, nv_ref, kw_ref, vw_ref, gate_ref, z_ref):
    xn = _rms(x_ref[...], gn_ref[...])
    z_ref[...] = _dot(xn, w_ref[...])
    u_ref[...] = z_ref[:, _SEG_U:_SEG_U + POOL_DIM]
    for c in range(NSA_DIM // 128):
        zt = z_ref[:, _SEG_Q + c * 128:_SEG_Q + (c + 1) * 128]
        q_ref[:, c * 128:(c + 1) * 128] = (_head_rms_tile(zt, qg_ref[...]) * (HD ** -0.5)).astype(q_ref.dtype)
    kv = _SEG_KV
    nk_ref[:, 0:KV_DIM] = z_ref[:, kv:kv + KV_DIM]
    nv_ref[:, 0:KV_DIM] = z_ref[:, kv + KV_DIM:kv + 2 * KV_DIM]
    nv_ref[:, KV_DIM:2 * KV_DIM] = z_ref[:, kv + 3 * KV_DIM:kv + 4 * KV_DIM]
    vw_ref[...] = z_ref[:, kv + 5 * KV_DIM:kv + 6 * KV_DIM]
    for c in range(KV_DIM // 128):
        zs = z_ref[:, kv + 2 * KV_DIM + c * 128:kv + 2 * KV_DIM + (c + 1) * 128]
        nk_ref[:, KV_DIM + c * 128:KV_DIM + (c + 1) * 128] = _head_rms_tile(zs, ksg_ref[...])
        zw = z_ref[:, kv + 4 * KV_DIM + c * 128:kv + 4 * KV_DIM + (c + 1) * 128]
        kw_ref[:, c * 128:(c + 1) * 128] = _head_rms_tile(zw, kwg_ref[...])
    gate_ref[...] = _sigmoid(z_ref[:, _SEG_GL:_SEG_GL + 128])


def _inproj(x, gn, w, qg, ksg, kwg, tm):
    m, d = x.shape
    row = lambda i: (i, 0)
    fix = lambda i: (0, 0)
    outs = [(POOL_DIM, F32), (NSA_DIM, _MX), (2 * KV_DIM, F32), (2 * KV_DIM, F32), (KV_DIM, F32), (KV_DIM, F32),
            (128, F32)]
    return pl.pallas_call(
        _inproj_kernel, grid=(m // tm,),
        in_specs=[pl.BlockSpec((tm, d), row), pl.BlockSpec((1, d), fix), pl.BlockSpec((d, IN_PAD), fix),
                  pl.BlockSpec((1, 128), fix), pl.BlockSpec((1, 128), fix), pl.BlockSpec((1, 128), fix)],
        out_specs=[pl.BlockSpec((tm, c), row) for c, _ in outs],
        out_shape=[jax.ShapeDtypeStruct((m, c), dt) for c, dt in outs],
        scratch_shapes=[pltpu.VMEM((tm, IN_PAD), F32)],
        compiler_params=_cp("parallel"))(x, gn, w, qg, ksg, kwg)


def _pool_select(sums, u, cnt):
    grp = lax.broadcasted_iota(jnp.int32, u.shape, 1) >> 6
    ssel = jnp.where(grp == 0, sums[2], jnp.where(grp == 1, sums[4], jnp.where(grp == 2, sums[8], sums[16])))
    return ssel / cnt - u, grp


def _pool_kernel(u_ref, wbd_ref, sc_ref, y_ref, ext_ref, *, tm):
    i = pl.program_id(1)

    @pl.when(i == 0)
    def _():
        ext_ref[0:16, :] = jnp.zeros((16, POOL_DIM), F32)

    u = u_ref[...]
    ext_ref[16:16 + tm, :] = u
    acc = u
    sums = {}
    for s in range(1, 16):
        acc = acc + ext_ref[16 - s:16 - s + tm, :]
        if s + 1 in POOL_WINDOWS:
            sums[s + 1] = acc
    grp = lax.broadcasted_iota(jnp.int32, u.shape, 1) >> 6
    win = jnp.where(grp == 0, 2, jnp.where(grp == 1, 4, jnp.where(grp == 2, 8, 16)))
    pos = i * tm + lax.broadcasted_iota(jnp.int32, u.shape, 0)
    cnt = jnp.minimum(win, pos + 1).astype(F32)
    mixed, _ = _pool_select(sums, u, cnt)
    y_ref[...] = _dot(mixed, wbd_ref[...]) * sc_ref[...]
    ext_ref[0:16, :] = ext_ref[tm:tm + 16, :]


def _pool_prompt(u, wbd, scale, tm):
    n, t, c = u.shape
    return pl.pallas_call(
        functools.partial(_pool_kernel, tm=tm), grid=(n, t // tm),
        in_specs=[pl.BlockSpec((None, tm, c), lambda b, i: (b, i, 0)), pl.BlockSpec((c, c), lambda b, i: (0, 0)),
                  pl.BlockSpec((1, c), lambda b, i: (0, 0))],
        out_specs=pl.BlockSpec((None, tm, c), lambda b, i: (b, i, 0)),
        out_shape=jax.ShapeDtypeStruct((n, t, c), F32),
        scratch_shapes=[pltpu.VMEM((tm + 16, c), F32)],
        compiler_params=_cp("parallel", "arbitrary"))(u, wbd, scale)


def _pool_step_kernel(buf_ref, u_ref, wbd_ref, sc_ref, y_ref, *, pos0):
    u = u_ref[...]
    acc = u
    sums = {}
    for s in range(1, 16):
        acc = acc + buf_ref[:, POOL_BUF - s, :]
        if s + 1 in POOL_WINDOWS:
            sums[s + 1] = acc
    grp = lax.broadcasted_iota(jnp.int32, u.shape, 1) >> 6
    win = jnp.where(grp == 0, 2, jnp.where(grp == 1, 4, jnp.where(grp == 2, 8, 16)))
    cnt = jnp.minimum(win, pos0 + 1).astype(F32)
    mixed, _ = _pool_select(sums, u, cnt)
    y_ref[...] = _dot(mixed, wbd_ref[...]) * sc_ref[...]


def _pool_step(buf, u, wbd, scale, pos0):
    n, c = u.shape
    full = lambda *shape: pl.BlockSpec(shape, lambda i: (0,) * len(shape))
    return pl.pallas_call(
        functools.partial(_pool_step_kernel, pos0=pos0), grid=(1,),
        in_specs=[full(n, POOL_BUF, c), full(n, c), full(c, c), full(1, c)],
        out_specs=full(n, c), out_shape=jax.ShapeDtypeStruct((n, c), F32),
        compiler_params=_cp("arbitrary"))(buf, u, wbd, scale)


def _compress_kernel(pt_ref, xk_ref, xv_ref, pek_ref, pev_ref, wk_ref, wv_ref, kcg_ref,
                     kc_ref, vc_ref, xs_k, xs_v, bsh, *, n_pages):
    j = pl.program_id(1)
    row0 = pl.multiple_of(j * PAGE, PAGE)
    n_tiles = KV_DIM // 128
    for c in range(n_tiles):
        xs_k[c, pl.ds(row0, PAGE), :] = xk_ref[:, c * 128:(c + 1) * 128]
        xs_v[c, pl.ds(row0, PAGE), :] = xv_ref[:, c * 128:(c + 1) * 128]

    @pl.when(j == n_pages - 1)
    def _():
        nh = n_pages * (PAGE // CMP_STRIDE)

        def run(xs, c, pe_ref, w_ref):
            a = jnp.zeros((nh, 128), F32)
            b = jnp.zeros((nh, 128), F32)
            for l in range(CMP_STRIDE):
                xl = xs[c, pl.ds(l, nh, stride=CMP_STRIDE), :]
                a = a + _dot(xl + pe_ref[l:l + 1, :], w_ref[l])
                b = b + _dot(xl + pe_ref[CMP_STRIDE + l:CMP_STRIDE + l + 1, :], w_ref[CMP_STRIDE + l])
            bsh[0:nh, :] = b
            bsh[nh:nh + 8, :] = jnp.zeros((8, 128), F32)
            return a + bsh[1:nh + 1, :]

        for c in range(n_tiles):
            kc_ref[:, c * 128:(c + 1) * 128] = _head_rms_tile(run(xs_k, c, pek_ref, wk_ref), kcg_ref[...])
            vc_ref[:, c * 128:(c + 1) * 128] = run(xs_v, c, pev_ref, wv_ref)


def _compress(cache_k, cache_v, page_table, e, pek, pev, wk, wv, kcg):
    nb, n_pages = page_table.shape
    nh = n_pages * (PAGE // CMP_STRIDE)
    page = pl.BlockSpec((None, None, PAGE, KV_DIM), lambda b, j, pt: (e, pt[b, j], 0, 0))
    fix2 = lambda b, j, pt: (0, 0)
    fix3 = lambda b, j, pt: (0, 0, 0)
    out = pl.BlockSpec((None, nh, KV_DIM), lambda b, j, pt: (b, 0, 0))
    gs = pltpu.PrefetchScalarGridSpec(
        num_scalar_prefetch=1, grid=(nb, n_pages),
        in_specs=[page, page, pl.BlockSpec((CMP_LEN, 128), fix2), pl.BlockSpec((CMP_LEN, 128), fix2),
                  pl.BlockSpec((CMP_LEN, 128, 128), fix3), pl.BlockSpec((CMP_LEN, 128, 128), fix3),
                  pl.BlockSpec((1, 128), fix2)],
        out_specs=[out, out],
        scratch_shapes=[pltpu.VMEM((KV_DIM // 128, n_pages * PAGE, 128), F32),
                        pltpu.VMEM((KV_DIM // 128, n_pages * PAGE, 128), F32),
                        pltpu.VMEM((nh + 8, 128), F32)])
    return pl.pallas_call(
        functools.partial(_compress_kernel, n_pages=n_pages), grid_spec=gs,
        out_shape=[jax.ShapeDtypeStruct((nb, nh, KV_DIM), F32)] * 2,
        compiler_params=_cp("parallel", "arbitrary"))(page_table, cache_k, cache_v, pek, pev, wk, wv, kcg)


def _nsa_prompt_kernel(sl_ref, q_ref, kc_ref, vc_ref, ks_ref, vs_ref, kw_ref, vw_ref, g_ref, mimp_ref,
                       o_ref, m_ref, l_ref, acc_ref, *, n_cmp, n_slc, n_top):
    g = pl.program_id(1)
    i = pl.program_id(2)
    nh = kc_ref.shape[0]
    rows = REP * QBLK
    q = q_ref[...].reshape(rows, HD)
    qpos0 = i * QBLK
    t_col = lax.broadcasted_iota(jnp.int32, (QBLK, 1), 0)
    pos = qpos0 + t_col

    cid = lax.broadcasted_iota(jnp.int32, (QBLK, nh), 1)
    dist_c = pos - (cid * CMP_STRIDE + (CMP_LEN - 1))
    valid_c = (dist_c >= 0) & (cid < n_cmp)
    dist_cf = dist_c.astype(F32)
    s_c = _dot_nt(q, kc_ref[...])
    vc = vc_ref[...]
    imp = jnp.zeros((QBLK, nh), F32)
    o_cmp = []
    for r in range(REP):
        p = _masked_softmax(s_c[r * QBLK:(r + 1) * QBLK] - sl_ref[g * REP + r] * dist_cf, valid_c)
        imp = imp + p
        o_cmp.append(_dot(p, vc))
    imp_slc = _dot2(imp, mimp_ref[...])
    blk = lax.broadcasted_iota(jnp.int32, imp_slc.shape, 1)
    cur = pos >> 6
    forced = (blk == 0) | (blk == cur) | (blk == cur - 1)
    causal = blk * SLC_LEN <= pos
    score = jnp.where(forced, BIG, jnp.where(causal, imp_slc, -BIG))
    sel = _topk_mask(score, n_slc, n_top)

    d0 = (lax.broadcasted_iota(jnp.int32, (QBLK, QBLK), 0)
          - lax.broadcasted_iota(jnp.int32, (QBLK, QBLK), 1))
    e_row = lax.broadcasted_iota(jnp.int32, (sel.shape[1], QBLK), 0)
    e_col = lax.broadcasted_iota(jnp.int32, (sel.shape[1], QBLK), 1) >> 6

    def reset():
        m_ref[...] = jnp.full(m_ref.shape, NEG, F32)
        l_ref[...] = jnp.zeros(l_ref.shape, F32)
        acc_ref[...] = jnp.zeros(acc_ref.shape, F32)

    def chunk(c, k_ref, v_ref, valid):
        k0 = pl.multiple_of(c * QBLK, QBLK)
        s = _dot_nt(q, k_ref[pl.ds(k0, QBLK), :])
        dist = (d0 + (qpos0 - c * QBLK)).astype(F32)
        s = jnp.concatenate([s[r * QBLK:(r + 1) * QBLK] - sl_ref[g * REP + r] * dist for r in range(REP)], axis=0)
        _online_update(s, jnp.concatenate([valid] * REP, axis=0), v_ref[pl.ds(k0, QBLK), :], m_ref, l_ref, acc_ref)

    def result():
        return acc_ref[...] / jnp.maximum(l_ref[...], 1e-30)

    reset()

    def slc_body(c, carry):
        expand = jnp.where(e_row == 2 * c + e_col, 1.0, 0.0)
        picked = _dot(sel, expand) > 0.5
        dist = d0 + (qpos0 - c * QBLK)
        chunk(c, ks_ref, vs_ref, picked & (dist >= 0))
        return carry

    lax.fori_loop(0, i + 1, slc_body, 0)
    o_slc = result()

    reset()

    def win_body(c, carry):
        dist = d0 + (qpos0 - c * QBLK)
        chunk(c, kw_ref, vw_ref, (dist >= 0) & (dist < WINDOW))
        return carry

    lax.fori_loop(jnp.maximum(i - WINDOW // QBLK, 0), i + 1, win_body, 0)
    o_win = result()

    gates = g_ref[...]
    for r in range(REP):
        sl = slice(r * QBLK, (r + 1) * QBLK)
        o = (gates[:, 3 * r:3 * r + 1] * o_cmp[r] + gates[:, 3 * r + 1:3 * r + 2] * o_slc[sl]
             + gates[:, 3 * r + 2:3 * r + 3] * o_win[sl])
        o_ref[:, r * HD:(r + 1) * HD] = o
    o_ref[:, REP * HD:] = jnp.zeros((QBLK, o_ref.shape[1] - REP * HD), F32)


def _nsa_prompt(slopes, q_h, kc_h, vc_h, ks_h, vs_h, kw_h, vw_h, gates_h, mimp, n_cmp):
    n, _, t, _ = q_h.shape
    nh = kc_h.shape[2]
    n_slc = t // SLC_LEN
    nq = t // QBLK
    seq = lambda rows: pl.BlockSpec((None, None, rows, HD), lambda b, g, i: (b, g, 0, 0))
    rows = REP * QBLK
    return pl.pallas_call(
        functools.partial(_nsa_prompt_kernel, n_cmp=n_cmp, n_slc=n_slc, n_top=min(SLC_TOP, n_slc)),
        grid=(n, KVH, nq),
        in_specs=[pl.BlockSpec(memory_space=pltpu.SMEM),
                  pl.BlockSpec((None, REP, QBLK, HD), lambda b, g, i: (b, g, i, 0)),
                  seq(nh), seq(nh), seq(t), seq(t), seq(t), seq(t),
                  pl.BlockSpec((None, None, QBLK, 9), lambda b, g, i: (b, g, i, 0)),
                  pl.BlockSpec(mimp.shape, lambda b, g, i: (0, 0))],
        out_specs=pl.BlockSpec((None, None, QBLK, 256), lambda b, g, i: (g, b, i, 0)),
        out_shape=jax.ShapeDtypeStruct((KVH, n, t, 256), F32),
        scratch_shapes=[pltpu.VMEM((rows, 1), F32), pltpu.VMEM((rows, 1), F32), pltpu.VMEM((rows, HD), F32)],
        compiler_params=_cp("parallel", "parallel", "arbitrary"),
    )(slopes, q_h, kc_h, vc_h, ks_h, vs_h, kw_h, vw_h, gates_h, mimp)


def _nsa_sample_kernel(pt_ref, q_ref, kc_ref, vc_ref, ck_ref, cv_ref, new_ref, wk_ref, wv_ref, g_ref,
                       sl_ref, grp_ref, mimp_ref, o_ref, m_ref, l_ref, acc_ref, sel_ref, ocmp_ref, ofull_ref,
                       *, n_pages, n_cmp, n_slc, n_top, wbuf):
    j = pl.program_id(1)
    pos0 = n_pages * PAGE
    q = q_ref[...]
    slope = sl_ref[:, 0:1]
    nrow = q.shape[0]

    @pl.when(j == 0)
    def _():
        nh = kc_ref.shape[0]
        cid = lax.broadcasted_iota(jnp.int32, (nrow, nh), 1)
        dist = pos0 - (cid * CMP_STRIDE + (CMP_LEN - 1))
        valid = (dist >= 0) & (cid < n_cmp)
        p = _masked_softmax(_dot_nt(q, kc_ref[...]) - slope * dist.astype(F32), valid)
        ocmp_ref[...] = _dot(p, vc_ref[...])
        imp = _dot2(_dot2r(grp_ref[...], p), mimp_ref[...])
        blk = lax.broadcasted_iota(jnp.int32, imp.shape, 1)
        cur = pos0 // SLC_LEN
        forced = (blk == 0) | (blk == cur) | (blk == cur - 1)
        causal = blk * SLC_LEN <= pos0
        score = jnp.where(forced, BIG, jnp.where(causal, imp, -BIG))
        sel_ref[...] = _topk_mask(score, n_slc, n_top)
        m_ref[...] = jnp.full(m_ref.shape, NEG, F32)
        l_ref[...] = jnp.zeros(l_ref.shape, F32)
        acc_ref[...] = jnp.zeros(acc_ref.shape, F32)

    sel = sel_ref[...]
    e_row = lax.broadcasted_iota(jnp.int32, (sel.shape[1], PAGE), 0)
    e_col = lax.broadcasted_iota(jnp.int32, (sel.shape[1], PAGE), 1) >> 6
    picked = _dot(sel, jnp.where(e_row == 2 * j + e_col, 1.0, 0.0)) > 0.5
    kpos = j * PAGE + lax.broadcasted_iota(jnp.int32, (nrow, PAGE), 1)
    dist = pos0 - kpos
    s = _dot_nt(q, ck_ref[...]) - slope * dist.astype(F32)
    _online_update(s, picked & (dist >= 0), cv_ref[...], m_ref, l_ref, acc_ref)

    @pl.when(j == n_pages - 1)
    def _():
        qf = q.astype(F32)
        new = new_ref[...]
        s_new = jnp.sum(qf * new[0:1, :], axis=-1, keepdims=True)
        ok = sel[:, n_slc - 1:n_slc] > 0.5
        s_new = jnp.where(ok, s_new, NEG)
        m_old = m_ref[...]
        m_new = jnp.maximum(m_old, s_new)
        p_new = jnp.where(ok, jnp.exp(s_new - m_new), 0.0)
        alpha = jnp.exp(m_old - m_new)
        l_tot = alpha * l_ref[...] + p_new
        acc = alpha * acc_ref[...] + p_new * new[1:2, :]
        o_slc = acc / jnp.maximum(l_tot, 1e-30)

        idx = lax.broadcasted_iota(jnp.int32, (nrow, wbuf), 1)
        dw = wbuf - idx
        valid_w = (dw >= 0) & (dw < WINDOW)
        s_w = jnp.where(valid_w, _dot_nt(q, wk_ref[...]) - slope * dw.astype(F32), NEG)
        s_wn = jnp.sum(qf * new[2:3, :], axis=-1, keepdims=True)
        m_w = jnp.maximum(jnp.max(s_w, axis=-1, keepdims=True), s_wn)
        p_w = jnp.where(valid_w, jnp.exp(s_w - m_w), 0.0)
        p_wn = jnp.exp(s_wn - m_w)
        den = jnp.maximum(jnp.sum(p_w, axis=-1, keepdims=True) + p_wn, 1e-30)
        o_win = (_dot(p_w, wv_ref[...]) + p_wn * new[3:4, :]) / den

        gates = g_ref[...]
        ofull_ref[...] = gates[:, 0:1] * ocmp_ref[...] + gates[:, 1:2] * o_slc + gates[:, 2:3] * o_win
        row_g = lax.broadcasted_iota(jnp.int32, (nrow, HD), 0) >> 2
        o = jnp.zeros((nrow, HD), F32)
        for gg in range(KVH):
            o = o + jnp.where(row_g == gg, ofull_ref[:, gg * HD:(gg + 1) * HD], 0.0)
        o_ref[...] = o


def _nsa_sample(page_table, e, qbd, kc, vc, cache_k, cache_v, new_rows, win_k, win_v, gates16, slopes16, grp16,
                mimp, n_cmp, n_slc):
    nb, n_pages = page_table.shape
    nh = kc.shape[1]
    wbuf = win_k.shape[2]
    per_b = lambda *shape: pl.BlockSpec((None,) + shape, lambda b, j, pt: (b,) + (0,) * len(shape))
    fix = lambda *shape: pl.BlockSpec(shape, lambda b, j, pt: (0,) * len(shape))
    page = pl.BlockSpec((None, None, PAGE, KV_DIM), lambda b, j, pt: (e, pt[b, j], 0, 1))
    win = pl.BlockSpec((None, None, wbuf, KV_DIM), lambda b, j, pt: (e, b, 0, 0))
    nrow = qbd.shape[1]
    gs = pltpu.PrefetchScalarGridSpec(
        num_scalar_prefetch=1, grid=(nb, n_pages),
        in_specs=[per_b(nrow, KV_DIM), per_b(nh, KV_DIM), per_b(nh, KV_DIM), page, page, per_b(8, KV_DIM),
                  win, win, per_b(nrow, 128), fix(nrow, 128), fix(nrow, nrow), fix(*mimp.shape)],
        out_specs=per_b(nrow, HD),
        scratch_shapes=[pltpu.VMEM((nrow, 1), F32), pltpu.VMEM((nrow, 1), F32), pltpu.VMEM((nrow, KV_DIM), F32),
                        pltpu.VMEM((nrow, mimp.shape[1]), F32), pltpu.VMEM((nrow, KV_DIM), F32),
                        pltpu.VMEM((nrow, KV_DIM), F32)])
    return pl.pallas_call(
        functools.partial(_nsa_sample_kernel, n_pages=n_pages, n_cmp=n_cmp, n_slc=n_slc,
                          n_top=min(SLC_TOP, n_slc), wbuf=wbuf),
        grid_spec=gs, out_shape=jax.ShapeDtypeStruct((nb, nrow, HD), F32),
        compiler_params=_cp("parallel", "arbitrary"),
    )(page_table, qbd, kc, vc, cache_k, cache_v, new_rows, win_k, win_v, gates16, slopes16, grp16, mimp)


def _outproj_kernel(h_ref, y_ref, o_ref, wp_ref, wn_ref, out_ref):
    acc = h_ref[...] + _dot(y_ref[...], wp_ref[...])
    for g in range(KVH):
        acc = acc + _dot(o_ref[g], wn_ref[g])
    out_ref[...] = acc


def _outproj(h, y_pool, o4, wp, wn, tm):
    m, d = h.shape
    return pl.pallas_call(
        _outproj_kernel, grid=(m // tm,),
        in_specs=[pl.BlockSpec((tm, d), lambda i: (i, 0)), pl.BlockSpec((tm, POOL_DIM), lambda i: (i, 0)),
                  pl.BlockSpec((KVH, tm, 256), lambda i: (0, i, 0)), pl.BlockSpec((POOL_DIM, d), lambda i: (0, 0)),
                  pl.BlockSpec((KVH, 256, d), lambda i: (0, 0, 0))],
        out_specs=pl.BlockSpec((tm, d), lambda i: (i, 0)),
        out_shape=jax.ShapeDtypeStruct((m, d), F32), compiler_params=_cp("parallel"))(h, y_pool, o4, wp, wn)


def _rwkv_proj_kernel(*refs, has_vres):
    if has_vres:
        (xn_ref, xp_ref, mu_ref, wr_ref, wk_ref, wv_ref, w1_ref, w2_ref, a1_ref, a2_ref, g1_ref, g2_ref,
         w0_ref, a0_ref, kk_ref, ka_ref, vf_ref, v0_ref, v1_ref, v2_ref,
         r_out, w_out, k_out, v_out, kk_out, b_out, g_out) = refs
    else:
        (xn_ref, xp_ref, mu_ref, wr_ref, wk_ref, wv_ref, w1_ref, w2_ref, a1_ref, a2_ref, g1_ref, g2_ref,
         w0_ref, a0_ref, kk_ref, ka_ref,
         r_out, w_out, k_out, v_out, kk_out, b_out, g_out) = refs
    xn = xn_ref[...]
    xx = xp_ref[...] - xn
    mix = lambda j: xn + xx * mu_ref[j:j + 1, :]
    xr, xw, xk, xv, xa, xg = [mix(j) for j in range(6)]
    r_out[...] = _dot(xr, wr_ref[...])
    k = _dot(xk, wk_ref[...])
    v = _dot(xv, wv_ref[...])
    z = w0_ref[...] + _dot(jnp.tanh(_dot(xw, w1_ref[...])), w2_ref[...])
    w_log = -(jnp.maximum(-z, 0.0) + jnp.log(1.0 + jnp.exp(-jnp.abs(z)))) - 0.5
    w_out[...] = jnp.exp(-jnp.exp(w_log))
    if has_vres:
        v = v + (vf_ref[...] - v) * _sigmoid(v0_ref[...] + _dot(_dot(xv, v1_ref[...]), v2_ref[...]))
    v_out[...] = v
    a = _sigmoid(a0_ref[...] + _dot(_dot(xa, a1_ref[...]), a2_ref[...]))
    g_out[...] = _dot(_sigmoid(_dot(xg, g1_ref[...])), g2_ref[...])
    kk = k * kk_ref[...]
    for c in range(kk.shape[1] // 128):
        sl = slice(c * 128, (c + 1) * 128)
        kt = kk[:, sl]
        kn = kt / jnp.maximum(jnp.sqrt(_seg64_sum(kt * kt)), 1e-12)
        kk_out[:, sl] = kn
        b_out[:, sl] = kn * a[:, sl]
    k_out[...] = k * (1.0 + (a - 1.0) * ka_ref[...])


def _rwkv_proj(xn, xprev, p, vres, tm):
    m, d = xn.shape
    row = pl.BlockSpec((tm, d), lambda i: (i, 0))
    fix = lambda a: pl.BlockSpec(a.shape, lambda i: (0,) * a.ndim)
    args = [xn, xprev, p['mu'], p['wr'], p['wk'], p['wv'], p['w1'], p['w2'], p['a1'], p['a2'], p['g1'], p['g2'],
            p['w0'], p['a0'], p['k_k'], p['k_a']]
    specs = [row, row] + [fix(a) for a in args[2:]]
    if vres is not None:
        vf, v0, v1, v2 = vres
        args += [vf, v0, v1, v2]
        specs += [row, fix(v0), fix(v1), fix(v2)]
    return pl.pallas_call(
        functools.partial(_rwkv_proj_kernel, has_vres=vres is not None), grid=(m // tm,),
        in_specs=specs, out_specs=[row] * 7, out_shape=[jax.ShapeDtypeStruct((m, d), F32)] * 7,
        compiler_params=_cp("parallel"))(*args)


def _wkv_kernel(r_ref, w_ref, k_ref, kk_ref, b_ref, vt_ref, s0_ref, ot_ref, s_ref, *, tc, hb):
    @pl.when(pl.program_id(2) == 0)
    def _():
        s_ref[...] = s0_ref[...]

    ot_ref[...] = jnp.zeros(ot_ref.shape, F32)
    lane_t = lax.broadcasted_iota(jnp.int32, (HD, tc), 1)

    def body(t, carry):
        onehot = lane_t == t
        row = pl.ds(t, 1)
        for h in range(hb):
            s = s_ref[h]
            v_col = jnp.sum(jnp.where(onehot, vt_ref[h], 0.0), axis=-1, keepdims=True)
            sa = -jnp.sum(s * kk_ref[h, row, :], axis=-1, keepdims=True)
            s = s * w_ref[h, row, :] + sa * b_ref[h, row, :] + v_col * k_ref[h, row, :]
            s_ref[h] = s
            o_col = jnp.sum(s * r_ref[h, row, :], axis=-1, keepdims=True)
            ot_ref[h] = jnp.where(onehot, o_col, ot_ref[h])
        return carry

    lax.fori_loop(0, tc, body, 0)


def _wkv_scan(r, w, k, kk, b, vt, s0, tc, hb):
    n, nh_, t, _ = r.shape
    tok = pl.BlockSpec((None, hb, tc, HD), lambda bb, hh, tt: (bb, hh, tt, 0))
    tr = pl.BlockSpec((None, hb, HD, tc), lambda bb, hh, tt: (bb, hh, 0, tt))
    st = pl.BlockSpec((None, hb, HD, HD), lambda bb, hh, tt: (bb, hh, 0, 0))
    return pl.pallas_call(
        functools.partial(_wkv_kernel, tc=tc, hb=hb), grid=(n, nh_ // hb, t // tc),
        in_specs=[tok] * 5 + [tr, st], out_specs=[tr, st],
        out_shape=[jax.ShapeDtypeStruct((n, nh_, HD, t), F32), jax.ShapeDtypeStruct((n, nh_, HD, HD), F32)],
        compiler_params=_cp("parallel", "parallel", "arbitrary"))(r, w, k, kk, b, vt, s0)


def _wkv_step_kernel(r_ref, w_ref, k_ref, kk_ref, b_ref, vc_ref, s0_ref, o_ref, s_ref):
    s = s0_ref[...]
    sa = -jnp.sum(s * kk_ref[...], axis=-1, keepdims=True)
    s = s * w_ref[...] + sa * b_ref[...] + vc_ref[...] * k_ref[...]
    s_ref[...] = s
    o_ref[...] = jnp.sum(s * r_ref[...], axis=-1, keepdims=True)


def _wkv_step(r, w, k, kk, b, v_col, s0):
    n, nh_ = r.shape[:2]
    rowv = pl.BlockSpec((None, nh_, 1, HD), lambda i: (i, 0, 0, 0))
    colv = pl.BlockSpec((None, nh_, HD, 1), lambda i: (i, 0, 0, 0))
    st = pl.BlockSpec((None, nh_, HD, HD), lambda i: (i, 0, 0, 0))
    return pl.pallas_call(
        _wkv_step_kernel, grid=(n,), in_specs=[rowv] * 5 + [colv, st], out_specs=[colv, st],
        out_shape=[jax.ShapeDtypeStruct((n, nh_, HD, 1), F32), jax.ShapeDtypeStruct((n, nh_, HD, HD), F32)],
        compiler_params=_cp("parallel"))(r, w, k, kk, b, v_col, s0)


def _rwkv_out_kernel(h_ref, o_ref, r_ref, k_ref, v_ref, g_ref, lnw_ref, lnb_ref, rk_ref, wo_ref, out_ref, y_ref):
    for c in range(h_ref.shape[1] // 128):
        sl = slice(c * 128, (c + 1) * 128)
        o = o_ref[:, sl]
        dlt = o - _seg64_sum(o) * (1.0 / HD)
        var = _seg64_sum(dlt * dlt) * (1.0 / HD)
        on = dlt * lax.rsqrt(var + LN_X_EPS) * lnw_ref[:, sl] + lnb_ref[:, sl]
        bonus = _seg64_sum(r_ref[:, sl] * k_ref[:, sl] * rk_ref[:, sl])
        y_ref[:, sl] = ((on + bonus * v_ref[:, sl]) * g_ref[:, sl]).astype(y_ref.dtype)
    out_ref[...] = h_ref[...] + jnp.dot(y_ref[...], wo_ref[...], preferred_element_type=F32)


def _rwkv_out(h, o, r, k, v, g, lnw, lnb, rk, wo, tm):
    m, d = h.shape
    row = pl.BlockSpec((tm, d), lambda i: (i, 0))
    vec = pl.BlockSpec((1, d), lambda i: (0, 0))
    return pl.pallas_call(
        _rwkv_out_kernel, grid=(m // tm,),
        in_specs=[row] * 6 + [vec] * 3 + [pl.BlockSpec((d, d), lambda i: (0, 0))],
        out_specs=row, out_shape=jax.ShapeDtypeStruct((m, d), F32),
        scratch_shapes=[pltpu.VMEM((tm, d), _MX)],
        compiler_params=_cp("parallel"))(h, o, r, k, v, g, lnw, lnb, rk, wo)


def _row_tile(m, pref):
    return pref if m % pref == 0 else m


def _block_diag(w):
    g, a, b = w.shape
    eye = jnp.eye(g, dtype=w.dtype)
    return (eye[:, None, :, None] * w[:, :, None, :]).reshape(g * a, g * b)


def _imp_matrix(nh, n_cmp, n_slc, width):
    per = SLC_LEN // CMP_STRIDE
    lead = CMP_LEN // CMP_STRIDE - 1
    c = np.arange(nh)[:, None]
    j = np.arange(width)[None, :]
    m = (c - per * j >= -lead) & (c - per * j < per) & (c < n_cmp) & (j < n_slc)
    return jnp.asarray(m.astype(np.float32), dtype=_MX)


def _pad_lanes(n):
    return -(-n // 128) * 128


def _heads(x, n, t):
    return x.reshape(n, t, -1, HD).transpose(0, 2, 1, 3).astype(_MX)


def kernel(x_prompt, x_sample, cache_k, cache_v, page_table, state_win_k, state_win_v, state_pool, state_shift, state_wkv, p_prompt, p_sample, norm_mix, norm_mlp, norm_ple, mlp_w1, mlp_w2, ple_proj, ple_gate, even_w_in, even_w_out, pool_w, pool_scale, q_gain, k_gain, cmp_pe, cmp_w, rwkv_mu, rwkv_wr, rwkv_wk, rwkv_wv, rwkv_wo, rwkv_w0, rwkv_w1, rwkv_w2, rwkv_a0, rwkv_a1, rwkv_a2, rwkv_v0, rwkv_v1, rwkv_v2, rwkv_g1, rwkv_g2, rwkv_kk, rwkv_ka, rwkv_rk, rwkv_lnw, rwkv_lnb):
    nb, t, d = x_prompt.shape
    ns = x_sample.shape[0]
    depth = norm_mix.shape[0]
    n_even = even_w_in.shape[0]
    n_pages = page_table.shape[1]
    past_len = n_pages * PAGE
    wbuf = state_win_k.shape[2]
    n_phys = cache_k.shape[1]
    mp = nb * t
    tm_p = _row_tile(mp, 512)
    tm_a = _row_tile(mp, 256)
    slopes = jnp.asarray(_alibi_slopes(KVH * REP))
    mx = lambda a: a.astype(_MX)
    row1 = lambda a: a.reshape(1, -1)
    two = lambda a: jnp.tile(a.reshape(1, HD), (1, 2))

    rid = np.arange(16)
    real = (rid % 4) < REP
    slopes16 = jnp.asarray(np.where(real, _alibi_slopes(KVH * REP)[np.minimum((rid // 4) * REP + rid % 4, 11)],
                                    0.0).astype(np.float32))[:, None] * jnp.ones((1, 128), F32)
    grp16 = jnp.asarray(((rid[:, None] // 4 == rid[None, :] // 4) & real[None, :]).astype(np.float32))

    cache_k5 = cache_k.reshape(n_even, n_phys, PAGE, 2 * KV_DIM)
    cache_v5 = cache_v.reshape(n_even, n_phys, PAGE, 2 * KV_DIM)
    win_k4 = state_win_k.reshape(n_even, ns, wbuf, KV_DIM)
    win_v4 = state_win_v.reshape(n_even, ns, wbuf, KV_DIM)

    h_p = x_prompt.reshape(mp, d)
    h_s = x_sample.reshape(ns, d)
    outs = {k_: [] for k_ in ('nk_p', 'nv_p', 'nk_s', 'nv_s', 'wk_p', 'wv_p', 'wk_s', 'wv_s', 'pl_p', 'pl_s',
                              'sh_p', 'sh_s', 'st_p', 'st_s')}
    vf_p = vf_s = None

    for i in range(depth):
        gn = row1(norm_mix[i])
        if i % 2 == 0:
            e = i // 2
            w_in = mx(jnp.pad(even_w_in[e], ((0, 0), (0, IN_PAD - even_w_in.shape[2]))))
            wbd = mx(_block_diag(pool_w[e]))
            psc = row1(pool_scale[e])
            qg, ksg, kwg, kcg = two(q_gain[e]), two(k_gain[e, 1]), two(k_gain[e, 2]), two(k_gain[e, 0])
            w_out = even_w_out[e]
            wo_pool = mx(w_out[:POOL_DIM])
            wo_nsa = mx(jnp.pad(w_out[POOL_DIM:].reshape(KVH, REP * HD, d), ((0, 0), (0, 256 - REP * HD), (0, 0))))
            pe_k = jnp.tile(cmp_pe[e, 0], (1, 2))
            pe_v = jnp.tile(cmp_pe[e, 1], (1, 2))
            eye = jnp.eye(2, dtype=F32)
            bd = lambda w: mx((eye[None, :, None, :, None] * w[:, None, :, None, :]).reshape(CMP_LEN, 128, 128))
            cw_k, cw_v = bd(cmp_w[e, 0]), bd(cmp_w[e, 1])

            u, q, nk, nv, kw, vw, gate = _inproj(h_p, gn, w_in, qg, ksg, kwg, tm_a)
            y_pool = _pool_prompt(u.reshape(nb, t, POOL_DIM), wbd, psc, _row_tile(t, 512))
            nh_p = t // CMP_STRIDE
            pt_p = jnp.arange(nb * (t // PAGE), dtype=jnp.int32).reshape(nb, t // PAGE)
            kc, vc = _compress(nk.reshape(1, mp // PAGE, PAGE, 2 * KV_DIM), nv.reshape(1, mp // PAGE, PAGE, 2 * KV_DIM),
                               pt_p, 0, pe_k, pe_v, cw_k, cw_v, kcg)
            n_cmp_p = nh_p - (CMP_LEN // CMP_STRIDE - 1)
            n_slc_p = t // SLC_LEN
            gates_h = gate[:, :KVH * REP * 3].reshape(nb, t, KVH, REP * 3).transpose(0, 2, 1, 3)
            o4 = _nsa_prompt(slopes, _heads(q, nb, t), _heads(kc, nb, nh_p), _heads(vc, nb, nh_p),
                             _heads(nk[:, KV_DIM:], nb, t), _heads(nv[:, KV_DIM:], nb, t),
                             _heads(kw, nb, t), _heads(vw, nb, t), gates_h,
                             _imp_matrix(nh_p, n_cmp_p, n_slc_p, _pad_lanes(n_slc_p)), n_cmp_p)
            h_p = _outproj(h_p, y_pool.reshape(mp, POOL_DIM), o4.reshape(KVH, mp, 256), wo_pool, wo_nsa, tm_p)
            outs['nk_p'].append(nk.reshape(nb, t, 2, KVH, HD))
            outs['nv_p'].append(nv.reshape(nb, t, 2, KVH, HD))
            kw3 = kw.reshape(nb, t, KVH, HD)
            vw3 = vw.reshape(nb, t, KVH, HD)
            if t < wbuf:
                zpad = jnp.zeros((nb, wbuf - t, KVH, HD), F32)
                kw3, vw3 = jnp.concatenate([zpad, kw3], 1), jnp.concatenate([zpad, vw3], 1)
            outs['wk_p'].append(kw3[:, -wbuf:])
            outs['wv_p'].append(vw3[:, -wbuf:])
            outs['pl_p'].append(u.reshape(nb, t, POOL_DIM)[:, -POOL_BUF:])

            u, q, nk, nv, kw, vw, gate = _inproj(h_s, gn, w_in, qg, ksg, kwg, ns)
            y_pool = _pool_step(state_pool[e], u, wbd, psc, past_len)
            kc, vc = _compress(cache_k5, cache_v5, page_table, e, pe_k, pe_v, cw_k, cw_v, kcg)
            nh_s = kc.shape[1]
            n_cmp_s = nh_s - (CMP_LEN // CMP_STRIDE - 1)
            n_slc_s = -(-(past_len + 1) // SLC_LEN)
            q16 = jnp.pad(q.reshape(ns, KVH, REP, HD), ((0, 0), (0, 0), (0, 1), (0, 0))).reshape(ns, 16, 1, HD)
            gsel = jnp.asarray((np.arange(16)[:, None] // 4 == np.arange(KVH)[None, :]).astype(np.float32))
            qbd = (q16.astype(F32) * gsel[None, :, :, None]).reshape(ns, 16, KV_DIM).astype(_MX)
            new_rows = jnp.pad(jnp.stack([nk[:, KV_DIM:], nv[:, KV_DIM:], kw, vw], axis=1), ((0, 0), (0, 4), (0, 0)))
            g16 = jnp.pad(gate[:, :KVH * REP * 3].reshape(ns, KVH, REP, 3), ((0, 0), (0, 0), (0, 1), (0, 125)))
            o16 = _nsa_sample(page_table, e, qbd, mx(kc), mx(vc), cache_k5, cache_v5, new_rows, win_k4, win_v4,
                              g16.reshape(ns, 16, 128), slopes16, grp16,
                              _imp_matrix(nh_s, n_cmp_s, n_slc_s, _pad_lanes(n_slc_s)), n_cmp_s, n_slc_s)
            o4 = o16.reshape(ns, KVH, 4, HD)[:, :, :REP].reshape(ns, KVH, REP * HD).transpose(1, 0, 2)
            o4 = jnp.pad(o4, ((0, 0), (0, 0), (0, 256 - REP * HD)))
            h_s = _outproj(h_s, y_pool, o4, wo_pool, wo_nsa, ns)
            outs['nk_s'].append(nk.reshape(ns, 1, 2, KVH, HD))
            outs['nv_s'].append(nv.reshape(ns, 1, 2, KVH, HD))
            outs['wk_s'].append(jnp.concatenate([state_win_k[e], kw.reshape(ns, 1, KVH, HD)], axis=1)[:, -wbuf:])
            outs['wv_s'].append(jnp.concatenate([state_win_v[e], vw.reshape(ns, 1, KVH, HD)], axis=1)[:, -wbuf:])
            outs['pl_s'].append(jnp.concatenate([state_pool[e], u[:, None]], axis=1)[:, -POOL_BUF:])
        else:
            o = i // 2
            lora_in = lambda a: mx(jnp.pad(a, ((0, 0), (0, LORA_PAD - a.shape[1]))))
            lora_out = lambda a: mx(jnp.pad(a, ((0, LORA_PAD - a.shape[0]), (0, 0))))
            p = dict(mu=rwkv_mu[o], wr=mx(rwkv_wr[o]), wk=mx(rwkv_wk[o]), wv=mx(rwkv_wv[o]),
                     w1=lora_in(rwkv_w1[o]), w2=lora_out(rwkv_w2[o]), a1=lora_in(rwkv_a1[o]), a2=lora_out(rwkv_a2[o]),
                     g1=lora_in(rwkv_g1[o]), g2=lora_out(rwkv_g2[o]), w0=row1(rwkv_w0[o]), a0=row1(rwkv_a0[o]),
                     k_k=row1(rwkv_kk[o]), k_a=row1(rwkv_ka[o]))
            vparams = None if o == 0 else (row1(rwkv_v0[o - 1]), lora_in(rwkv_v1[o - 1]), lora_out(rwkv_v2[o - 1]))
            lnw, lnb, rk, wo = row1(rwkv_lnw[o]), row1(rwkv_lnb[o]), row1(rwkv_rk[o]), mx(rwkv_wo[o])
            nhd = d // HD

            xn = _norm(h_p, gn, tm_p)
            xn3 = xn.reshape(nb, t, d)
            xprev = jnp.concatenate([jnp.zeros((nb, 1, d), F32), xn3[:, :-1]], axis=1).reshape(mp, d)
            vres = None if o == 0 else (vf_p,) + vparams
            r, w, k, v, kk, b, g = _rwkv_proj(xn, xprev, p, vres, tm_a)
            if o == 0:
                vf_p = v
            hm = lambda a: a.reshape(nb, t, nhd, HD).transpose(0, 2, 1, 3)
            vt = v.reshape(nb, t, nhd, HD).transpose(0, 2, 3, 1)
            tc = _row_tile(t, 128)
            ot, s_fin = _wkv_scan(hm(r), hm(w), hm(k), hm(kk), hm(b), vt, jnp.zeros((nb, nhd, HD, HD), F32), tc, 8)
            o_tok = ot.transpose(0, 3, 1, 2).reshape(mp, d)
            h_p = _rwkv_out(h_p, o_tok, r, k, v, g, lnw, lnb, rk, wo, tm_a)
            outs['sh_p'].append(xn3[:, -1])
            outs['st_p'].append(s_fin)

            xn = _norm(h_s, gn, ns)
            vres = None if o == 0 else (vf_s,) + vparams
            r, w, k, v, kk, b, g = _rwkv_proj(xn, state_shift[o], p, vres, ns)
            if o == 0:
                vf_s = v
            rows = lambda a: a.reshape(ns, nhd, 1, HD)
            o_col, s_fin = _wkv_step(rows(r), rows(w), rows(k), rows(kk), rows(b), v.reshape(ns, nhd, HD, 1),
                                     state_wkv[o])
            h_s = _rwkv_out(h_s, o_col.reshape(ns, d), r, k, v, g, lnw, lnb, rk, wo, ns)
            outs['sh_s'].append(xn)
            outs['st_s'].append(s_fin)

        w1, w2 = mx(mlp_w1[i]), mx(mlp_w2[i])
        gm, gp, wg, wp = row1(norm_mlp[i]), row1(norm_ple[i]), mx(ple_gate[i]), mx(ple_proj[i])
        h_p = _mlp(h_p, gm, w1, w2, tm_p, 1024)
        h_p = _ple(h_p, p_prompt[i].reshape(mp, -1), gp, wg, wp, tm_p)
        h_s = _mlp(h_s, gm, w1, w2, ns, 1024)
        h_s = _ple(h_s, p_sample[i].reshape(ns, -1), gp, wg, wp, ns)

    st = lambda name: jnp.stack(outs[name])
    return (h_p.reshape(nb, t, d), h_s.reshape(ns, 1, d), st('nk_p'), st('nv_p'), st('nk_s'), st('nv_s'),
            st('wk_p'), st('wv_p'), st('wk_s'), st('wv_s'), st('pl_p'), st('pl_s'),
            st('sh_p'), st('sh_s'), st('st_p'), st('st_s'))
```

```python
import functools
import math

import numpy as np
import jax
import jax.numpy as jnp
from jax import lax
from jax.experimental import pallas as pl
from jax.experimental.pallas import tpu as pltpu

F32 = jnp.float32
_MX = jnp.bfloat16

HD = 64
POOL_DIM = 256
POOL_WINDOWS = (2, 4, 8, 16)
POOL_BUF = 15
KVH = 4
REP = 3
NSA_DIM = KVH * REP * HD
KV_DIM = KVH * HD
CMP_LEN, CMP_STRIDE = 32, 16
SLC_LEN, SLC_TOP = 64, 16
WINDOW = 512
QBLK = 128
KEY_TILE = 256
PAGE = 128
RMS_EPS = 1e-6
LN_X_EPS = 64e-5
NEG = -1e30
BIG = 1e9
WKV_CHUNK = 64
LORA_PAD = 128
IN_PAD = 2688
VMEM_LIMIT = 56 * 1024 * 1024


def _cp(*sem):
    return pltpu.CompilerParams(dimension_semantics=sem, vmem_limit_bytes=VMEM_LIMIT)


def _alibi_slopes(n):
    p = 2 ** int(math.floor(math.log2(n)))
    s = [2.0 ** (-8.0 * (i + 1) / p) for i in range(p)]
    if p < n:
        s += [2.0 ** (-8.0 * (i + 1) / (2 * p)) for i in range(0, 2 * p, 2)][: n - p]
    return np.asarray(s, dtype=np.float32)


def _rms(x, g):
    return x * lax.rsqrt(jnp.mean(x * x, axis=-1, keepdims=True) + RMS_EPS) * g


def _sigmoid(x):
    return 1.0 / (1.0 + jnp.exp(-x))


def _dot(a, b):
    return jnp.dot(a.astype(_MX), b.astype(_MX), preferred_element_type=F32)


def _dot_nt(a, b):
    return lax.dot_general(a.astype(_MX), b.astype(_MX), (((1,), (1,)), ((), ())),
                           preferred_element_type=F32)


def _dot2(a, b):
    hi = a.astype(_MX)
    lo = (a - hi.astype(F32)).astype(_MX)
    b = b.astype(_MX)
    return (jnp.dot(hi, b, preferred_element_type=F32) + jnp.dot(lo, b, preferred_element_type=F32))


def _dot2r(a, b):
    hi = b.astype(_MX)
    lo = (b - hi.astype(F32)).astype(_MX)
    a = a.astype(_MX)
    return (jnp.dot(a, hi, preferred_element_type=F32) + jnp.dot(a, lo, preferred_element_type=F32))


def _seg64_sum(y):
    left = lax.broadcasted_iota(jnp.int32, y.shape, 1) < HD
    sa = jnp.sum(jnp.where(left, y, 0.0), axis=-1, keepdims=True)
    sb = jnp.sum(jnp.where(left, 0.0, y), axis=-1, keepdims=True)
    return jnp.where(left, sa, sb)


def _head_rms_tile(zt, gain2):
    ms = _seg64_sum(zt * zt) * (1.0 / HD)
    return zt * lax.rsqrt(ms + RMS_EPS) * gain2


def _masked_softmax(s, valid):
    s = jnp.where(valid, s, NEG)
    e = jnp.where(valid, jnp.exp(s - jnp.max(s, axis=-1, keepdims=True)), 0.0)
    return e / jnp.maximum(jnp.sum(e, axis=-1, keepdims=True), 1e-30)


def _online_update(s, valid, v, m_ref, l_ref, acc_ref):
    s = jnp.where(valid, s, NEG)
    m_old = m_ref[...]
    m_new = jnp.maximum(m_old, jnp.max(s, axis=-1, keepdims=True))
    p = jnp.where(valid, jnp.exp(s - m_new), 0.0)
    alpha = jnp.exp(m_old - m_new)
    l_ref[...] = alpha * l_ref[...] + jnp.sum(p, axis=-1, keepdims=True)
    acc_ref[...] = alpha * acc_ref[...] + _dot(p, v)
    m_ref[...] = m_new


def _topk_mask(score, n_cand, n_top):
    col_id = lax.broadcasted_iota(jnp.int32, score.shape, 1)
    rank = jnp.zeros(score.shape, F32)
    for j0 in range(n_cand):
        cj = score[:, j0:j0 + 1]
        beats = jnp.where(cj > score, 1.0, jnp.where((cj == score) & (col_id > j0), 1.0, 0.0))
        rank = rank + beats
    return jnp.where((rank < n_top) & (col_id < n_cand), 1.0, 0.0)


def _norm_kernel(x_ref, g_ref, o_ref):
    o_ref[...] = _rms(x_ref[...], g_ref[...])


def _norm(x, g, tm):
    m, d = x.shape
    return pl.pallas_call(
        _norm_kernel, grid=(m // tm,),
        in_specs=[pl.BlockSpec((tm, d), lambda i: (i, 0)), pl.BlockSpec((1, d), lambda i: (0, 0))],
        out_specs=pl.BlockSpec((tm, d), lambda i: (i, 0)),
        out_shape=jax.ShapeDtypeStruct((m, d), F32), compiler_params=_cp("parallel"))(x, g)


def _mlp_kernel(h_ref, g_ref, w1_ref, w2_ref, o_ref, xn_ref):
    @pl.when(pl.program_id(1) == 0)
    def _():
        x = h_ref[...]
        xn_ref[...] = _rms(x, g_ref[...]).astype(_MX)
        o_ref[...] = x

    a = jnp.dot(xn_ref[...], w1_ref[...], preferred_element_type=F32)
    a = jnp.square(jnp.maximum(a, 0.0))
    o_ref[...] += _dot(a, w2_ref[...])


def _mlp(h, g, w1, w2, tm, tf):
    m, d = h.shape
    dff = w1.shape[1]
    return pl.pallas_call(
        _mlp_kernel, grid=(m // tm, dff // tf),
        in_specs=[pl.BlockSpec((tm, d), lambda i, j: (i, 0)), pl.BlockSpec((1, d), lambda i, j: (0, 0)),
                  pl.BlockSpec((d, tf), lambda i, j: (0, j)), pl.BlockSpec((tf, d), lambda i, j: (j, 0))],
        out_specs=pl.BlockSpec((tm, d), lambda i, j: (i, 0)),
        out_shape=jax.ShapeDtypeStruct((m, d), F32),
        scratch_shapes=[pltpu.VMEM((tm, d), _MX)],
        compiler_params=_cp("parallel", "arbitrary"))(h, g, w1, w2)


def _ple_kernel(h_ref, p_ref, g_ref, wg_ref, wp_ref, o_ref):
    h = h_ref[...]
    gate = _sigmoid(_dot(_rms(h, g_ref[...]), wg_ref[...]))
    o_ref[...] = h + gate * _dot(p_ref[...], wp_ref[...])


def _ple(h, p, g, wg, wp, tm):
    m, d = h.shape
    pd = p.shape[1]
    return pl.pallas_call(
        _ple_kernel, grid=(m // tm,),
        in_specs=[pl.BlockSpec((tm, d), lambda i: (i, 0)), pl.BlockSpec((tm, pd), lambda i: (i, 0)),
                  pl.BlockSpec((1, d), lambda i: (0, 0)), pl.BlockSpec((d, d), lambda i: (0, 0)),
                  pl.BlockSpec((pd, d), lambda i: (0, 0))],
        out_specs=pl.BlockSpec((tm, d), lambda i: (i, 0)),
        out_shape=jax.ShapeDtypeStruct((m, d), F32), compiler_params=_cp("parallel"))(h, p, g, wg, wp)


_SEG_U = 0
_SEG_Q = POOL_DIM
_SEG_KV = POOL_DIM + NSA_DIM
_SEG_GL = _SEG_KV + 6 * KV_DIM


def _inproj_kernel(x_ref, gn_ref, w_ref, qg_ref, ksg_ref, kwg_ref,
                   u_ref, q_ref, nk_ref, nv_ref, kw_ref, vw_ref, gate_ref, z_ref):
    xn = _rms(x_ref[...], gn_ref[...])
    z_ref[...] = _dot(xn, w_ref[...])
    u_ref[...] = z_ref[:, _SEG_U:_SEG_U + POOL_DIM]
    for c in range(NSA_DIM // 128):
        zt = z_ref[:, _SEG_Q + c * 128:_SEG_Q + (c + 1) * 128]
        q_ref[:, c * 128:(c + 1) * 128] = (_head_rms_tile(zt, qg_ref[...]) * (HD ** -0.5)).astype(q_ref.dtype)
    kv = _SEG_KV
    nk_ref[:, 0:KV_DIM] = z_ref[:, kv:kv + KV_DIM]
    nv_ref[:, 0:KV_DIM] = z_ref[:, kv + KV_DIM:kv + 2 * KV_DIM]
    nv_ref[:, KV_DIM:2 * KV_DIM] = z_ref[:, kv + 3 * KV_DIM:kv + 4 * KV_DIM]
    vw_ref[...] = z_ref[:, kv + 5 * KV_DIM:kv + 6 * KV_DIM]
    for c in range(KV_DIM // 128):
        zs = z_ref[:, kv + 2 * KV_DIM + c * 128:kv + 2 * KV_DIM + (c + 1) * 128]
        nk_ref[:, KV_DIM + c * 128:KV_DIM + (c + 1) * 128] = _head_rms_tile(zs, ksg_ref[...])
        zw = z_ref[:, kv + 4 * KV_DIM + c * 128:kv + 4 * KV_DIM + (c + 1) * 128]
        kw_ref[:, c * 128:(c + 1) * 128] = _head_rms_tile(zw, kwg_ref[...])
    gate_ref[...] = _sigmoid(z_ref[:, _SEG_GL:_SEG_GL + 128])


def _inproj(x, gn, w, qg, ksg, kwg, tm):
    m, d = x.shape
    row = lambda i: (i, 0)
    fix = lambda i: (0, 0)
    outs = [(POOL_DIM, F32), (NSA_DIM, _MX), (2 * KV_DIM, F32), (2 * KV_DIM, F32), (KV_DIM, F32), (KV_DIM, F32),
            (128, F32)]
    return pl.pallas_call(
        _inproj_kernel, grid=(m // tm,),
        in_specs=[pl.BlockSpec((tm, d), row), pl.BlockSpec((1, d), fix), pl.BlockSpec((d, IN_PAD), fix),
                  pl.BlockSpec((1, 128), fix), pl.BlockSpec((1, 128), fix), pl.BlockSpec((1, 128), fix)],
        out_specs=[pl.BlockSpec((tm, c), row) for c, _ in outs],
        out_shape=[jax.ShapeDtypeStruct((m, c), dt) for c, dt in outs],
        scratch_shapes=[pltpu.VMEM((tm, IN_PAD), F32)],
        compiler_params=_cp("parallel"))(x, gn, w, qg, ksg, kwg)


def _pool_select(sums, u, cnt):
    grp = lax.broadcasted_iota(jnp.int32, u.shape, 1) >> 6
    ssel = jnp.where(grp == 0, sums[2], jnp.where(grp == 1, sums[4], jnp.where(grp == 2, sums[8], sums[16])))
    return ssel / cnt - u, grp


def _pool_kernel(u_ref, wbd_ref, sc_ref, y_ref, ext_ref, *, tm):
    i = pl.program_id(1)

    @pl.when(i == 0)
    def _():
        ext_ref[0:16, :] = jnp.zeros((16, POOL_DIM), F32)

    u = u_ref[...]
    ext_ref[16:16 + tm, :] = u
    acc = u
    sums = {}
    for s in range(1, 16):
        acc = acc + ext_ref[16 - s:16 - s + tm, :]
        if s + 1 in POOL_WINDOWS:
            sums[s + 1] = acc
    grp = lax.broadcasted_iota(jnp.int32, u.shape, 1) >> 6
    win = jnp.where(grp == 0, 2, jnp.where(grp == 1, 4, jnp.where(grp == 2, 8, 16)))
    pos = i * tm + lax.broadcasted_iota(jnp.int32, u.shape, 0)
    cnt = jnp.minimum(win, pos + 1).astype(F32)
    mixed, _ = _pool_select(sums, u, cnt)
    y_ref[...] = _dot(mixed, wbd_ref[...]) * sc_ref[...]
    ext_ref[0:16, :] = ext_ref[tm:tm + 16, :]


def _pool_prompt(u, wbd, scale, tm):
    n, t, c = u.shape
    return pl.pallas_call(
        functools.partial(_pool_kernel, tm=tm), grid=(n, t // tm),
        in_specs=[pl.BlockSpec((None, tm, c), lambda b, i: (b, i, 0)), pl.BlockSpec((c, c), lambda b, i: (0, 0)),
                  pl.BlockSpec((1, c), lambda b, i: (0, 0))],
        out_specs=pl.BlockSpec((None, tm, c), lambda b, i: (b, i, 0)),
        out_shape=jax.ShapeDtypeStruct((n, t, c), F32),
        scratch_shapes=[pltpu.VMEM((tm + 16, c), F32)],
        compiler_params=_cp("parallel", "arbitrary"))(u, wbd, scale)


def _pool_step_kernel(buf_ref, u_ref, wbd_ref, sc_ref, y_ref, *, pos0):
    u = u_ref[...]
    acc = u
    sums = {}
    for s in range(1, 16):
        acc = acc + buf_ref[:, POOL_BUF - s, :]
        if s + 1 in POOL_WINDOWS:
            sums[s + 1] = acc
    grp = lax.broadcasted_iota(jnp.int32, u.shape, 1) >> 6
    win = jnp.where(grp == 0, 2, jnp.where(grp == 1, 4, jnp.where(grp == 2, 8, 16)))
    cnt = jnp.minimum(win, pos0 + 1).astype(F32)
    mixed, _ = _pool_select(sums, u, cnt)
    y_ref[...] = _dot(mixed, wbd_ref[...]) * sc_ref[...]


def _pool_step(buf, u, wbd, scale, pos0):
    n, c = u.shape
    full = lambda *shape: pl.BlockSpec(shape, lambda i: (0,) * len(shape))
    return pl.pallas_call(
        functools.partial(_pool_step_kernel, pos0=pos0), grid=(1,),
        in_specs=[full(n, POOL_BUF, c), full(n, c), full(c, c), full(1, c)],
        out_specs=full(n, c), out_shape=jax.ShapeDtypeStruct((n, c), F32),
        compiler_params=_cp("arbitrary"))(buf, u, wbd, scale)


def _compress_kernel(pt_ref, xk_ref, xv_ref, pek_ref, pev_ref, wk_ref, wv_ref, kcg_ref,
                     kc_ref, vc_ref, xs_k, xs_v, bsh, *, n_pages):
    j = pl.program_id(1)
    row0 = pl.multiple_of(j * PAGE, PAGE)
    n_tiles = KV_DIM // 128
    for c in range(n_tiles):
        xs_k[c, pl.ds(row0, PAGE), :] = xk_ref[:, c * 128:(c + 1) * 128]
        xs_v[c, pl.ds(row0, PAGE), :] = xv_ref[:, c * 128:(c + 1) * 128]

    @pl.when(j == n_pages - 1)
    def _():
        nh = n_pages * (PAGE // CMP_STRIDE)

        def run(xs, c, pe_ref, w_ref):
            a = jnp.zeros((nh, 128), F32)
            b = jnp.zeros((nh, 128), F32)
            for l in range(CMP_STRIDE):
                xl = xs[c, pl.ds(l, nh, stride=CMP_STRIDE), :]
                a = a + _dot(xl + pe_ref[l:l + 1, :], w_ref[l])
                b = b + _dot(xl + pe_ref[CMP_STRIDE + l:CMP_STRIDE + l + 1, :], w_ref[CMP_STRIDE + l])
            bsh[0:nh, :] = b
            bsh[nh:nh + 8, :] = jnp.zeros((8, 128), F32)
            return a + bsh[1:nh + 1, :]

        for c in range(n_tiles):
            kc_ref[:, c * 128:(c + 1) * 128] = _head_rms_tile(run(xs_k, c, pek_ref, wk_ref), kcg_ref[...])
            vc_ref[:, c * 128:(c + 1) * 128] = run(xs_v, c, pev_ref, wv_ref)


def _compress(cache_k, cache_v, page_table, e, pek, pev, wk, wv, kcg):
    nb, n_pages = page_table.shape
    nh = n_pages * (PAGE // CMP_STRIDE)
    page = pl.BlockSpec((None, None, PAGE, KV_DIM), lambda b, j, pt: (e, pt[b, j], 0, 0))
    fix2 = lambda b, j, pt: (0, 0)
    fix3 = lambda b, j, pt: (0, 0, 0)
    out = pl.BlockSpec((None, nh, KV_DIM), lambda b, j, pt: (b, 0, 0))
    gs = pltpu.PrefetchScalarGridSpec(
        num_scalar_prefetch=1, grid=(nb, n_pages),
        in_specs=[page, page, pl.BlockSpec((CMP_LEN, 128), fix2), pl.BlockSpec((CMP_LEN, 128), fix2),
                  pl.BlockSpec((CMP_LEN, 128, 128), fix3), pl.BlockSpec((CMP_LEN, 128, 128), fix3),
                  pl.BlockSpec((1, 128), fix2)],
        out_specs=[out, out],
        scratch_shapes=[pltpu.VMEM((KV_DIM // 128, n_pages * PAGE, 128), F32),
                        pltpu.VMEM((KV_DIM // 128, n_pages * PAGE, 128), F32),
                        pltpu.VMEM((nh + 8, 128), F32)])
    return pl.pallas_call(
        functools.partial(_compress_kernel, n_pages=n_pages), grid_spec=gs,
        out_shape=[jax.ShapeDtypeStruct((nb, nh, KV_DIM), F32)] * 2,
        compiler_params=_cp("parallel", "arbitrary"))(page_table, cache_k, cache_v, pek, pev, wk, wv, kcg)


def _topk_rows(score, n_top):
    rid = lax.broadcasted_iota(jnp.int32, score.shape, 0).astype(F32)
    sel = jnp.zeros(score.shape, F32)
    for _ in range(n_top):
        m = jnp.max(score, axis=0, keepdims=True)
        first = jnp.min(jnp.where(score == m, rid, 1e9), axis=0, keepdims=True)
        hit = rid == first
        sel = jnp.where(hit, 1.0, sel)
        score = jnp.where(hit, -jnp.inf, score)
    return sel


def _nsa_prompt_kernel(sl_ref, q_ref, kc_ref, vct_ref, ks_ref, vst_ref, kw_ref, vwt_ref, gt_ref, mimpt_ref,
                       o_ref, sel_ref, s_ref, acc_ref, pv_ref, *, n_cmp, n_slc, n_top):
    g = pl.program_id(1)
    i = pl.program_id(2)
    nh = kc_ref.shape[0]
    q = q_ref[...].reshape(REP * QBLK, HD)
    qpos0 = i * QBLK
    pos = qpos0 + lax.broadcasted_iota(jnp.int32, (1, QBLK), 1)
    slope = [sl_ref[g * REP + r] for r in range(REP)]

    cid = lax.broadcasted_iota(jnp.int32, (nh, QBLK), 0)
    dist_c = pos - (cid * CMP_STRIDE + (CMP_LEN - 1))
    valid_c = (dist_c >= 0) & (cid < n_cmp)
    dist_cf = dist_c.astype(F32)
    s_c = _dot_nt(kc_ref[...], q)
    vct = vct_ref[...]
    imp = jnp.zeros((nh, QBLK), F32)
    o_cmp = []
    for r in range(REP):
        s = jnp.where(valid_c, s_c[:, r * QBLK:(r + 1) * QBLK] - slope[r] * dist_cf, NEG)
        e = jnp.where(valid_c, jnp.exp(s - jnp.max(s, axis=0, keepdims=True)), 0.0)
        p = e / jnp.maximum(jnp.sum(e, axis=0, keepdims=True), 1e-30)
        imp = imp + p
        o_cmp.append(_dot(vct, p))
    imp_slc = _dot2r(mimpt_ref[...], imp)
    blk = lax.broadcasted_iota(jnp.int32, imp_slc.shape, 0)
    cur = pos >> 6
    forced = (blk == 0) | (blk == cur) | (blk == cur - 1)
    causal = blk * SLC_LEN <= pos
    score = jnp.where(forced, BIG, jnp.where(causal, imp_slc, -BIG))
    score = jnp.where(blk < n_slc, score, -jnp.inf)
    sel_ref[...] = _topk_rows(score, n_top)

    d0 = (lax.broadcasted_iota(jnp.int32, (KEY_TILE, QBLK), 1)
          - lax.broadcasted_iota(jnp.int32, (KEY_TILE, QBLK), 0))
    hi = i // (KEY_TILE // QBLK)

    def sweep(lo, k_ref, vt_ref, valid_fn):
        def scores(c):
            k0 = pl.multiple_of(c * KEY_TILE, KEY_TILE)
            return _dot_nt(k_ref[pl.ds(k0, KEY_TILE), :], q)

        def body(c, carry):
            m, l = carry
            s_ref[(c + 1) % 2] = scores(jnp.minimum(c + 1, hi))
            dist = d0 + (qpos0 - c * KEY_TILE)
            mask = jnp.where(valid_fn(c, dist), 0.0, NEG)
            dist_f = dist.astype(F32)
            s = s_ref[c % 2] + jnp.concatenate([mask - slope[r] * dist_f for r in range(REP)], axis=1)
            m_new = jnp.maximum(m, jnp.max(s, axis=0, keepdims=True))
            p = jnp.exp(s - m_new)
            alpha = jnp.exp(m - m_new)
            l = alpha * l + jnp.sum(p, axis=0, keepdims=True)
            acc_ref[...] = alpha * (acc_ref[...] + pv_ref[...])
            pv_ref[...] = _dot(vt_ref[c], p)
            return m_new, l

        s_ref[lo % 2] = scores(lo)
        acc_ref[...] = jnp.zeros(acc_ref.shape, F32)
        pv_ref[...] = jnp.zeros(pv_ref.shape, F32)
        init = (jnp.full((1, REP * QBLK), NEG, F32), jnp.zeros((1, REP * QBLK), F32))
        _, l = lax.fori_loop(lo, hi + 1, body, init)
        return (acc_ref[...] + pv_ref[...]) / l

    def picked(c, dist):
        per_tile = KEY_TILE // SLC_LEN
        rows = [jnp.broadcast_to(sel_ref[pl.ds(per_tile * c + b, 1), :], (SLC_LEN, QBLK)) for b in range(per_tile)]
        return (jnp.concatenate(rows, axis=0) > 0.5) & (dist >= 0)

    o_slc = sweep(0, ks_ref, vst_ref, picked)
    lo_win = jnp.maximum(i - WINDOW // QBLK, 0) // (KEY_TILE // QBLK)
    o_win = sweep(lo_win, kw_ref, vwt_ref, lambda c, dist: (dist >= 0) & (dist < WINDOW))

    gates = gt_ref[...]
    for r in range(REP):
        sl = slice(r * QBLK, (r + 1) * QBLK)
        o_ref[r * HD:(r + 1) * HD, :] = (gates[3 * r:3 * r + 1, :] * o_cmp[r] + gates[3 * r + 1:3 * r + 2, :] * o_slc[:, sl]
                                         + gates[3 * r + 2:3 * r + 3, :] * o_win[:, sl])
    o_ref[REP * HD:, :] = jnp.zeros((o_ref.shape[0] - REP * HD, QBLK), F32)


def _nsa_prompt(slopes, q_h, kc_h, vc_t, ks_h, vs_t, kw_h, vw_t, gates_t, mimp_t, n_cmp):
    n, _, t, _ = q_h.shape
    assert t % KEY_TILE == 0
    nh = kc_h.shape[2]
    n_slc = t // SLC_LEN
    nq = t // QBLK
    nkt = t // KEY_TILE
    seq = lambda *shape: pl.BlockSpec((None, None) + shape, lambda b, g, i: (b, g) + (0,) * len(shape))
    return pl.pallas_call(
        functools.partial(_nsa_prompt_kernel, n_cmp=n_cmp, n_slc=n_slc, n_top=min(SLC_TOP, n_slc)),
        grid=(n, KVH, nq),
        in_specs=[pl.BlockSpec(memory_space=pltpu.SMEM),
                  pl.BlockSpec((None, REP, QBLK, HD), lambda b, g, i: (b, g, i, 0)),
                  seq(nh, HD), seq(HD, nh), seq(t, HD), seq(nkt, HD, KEY_TILE), seq(t, HD), seq(nkt, HD, KEY_TILE),
                  pl.BlockSpec((None, None, 9, QBLK), lambda b, g, i: (b, g, 0, i)),
                  pl.BlockSpec(mimp_t.shape, lambda b, g, i: (0, 0))],
        out_specs=pl.BlockSpec((None, None, 256, QBLK), lambda b, g, i: (g, b, 0, i)),
        out_shape=jax.ShapeDtypeStruct((KVH, n, 256, t), F32),
        scratch_shapes=[pltpu.VMEM((mimp_t.shape[0], QBLK), F32), pltpu.VMEM((2, KEY_TILE, REP * QBLK), F32),
                        pltpu.VMEM((HD, REP * QBLK), F32), pltpu.VMEM((HD, REP * QBLK), F32)],
        compiler_params=_cp("parallel", "parallel", "arbitrary"),
    )(slopes, q_h, kc_h, vc_t, ks_h, vs_t, kw_h, vw_t, gates_t, mimp_t)


def _nsa_sample_kernel(pt_ref, q_ref, kc_ref, vc_ref, ck_ref, cv_ref, new_ref, wk_ref, wv_ref, g_ref,
                       sl_ref, grp_ref, mimp_ref, o_ref, m_ref, l_ref, acc_ref, sel_ref, ocmp_ref, ofull_ref,
                       *, n_pages, n_cmp, n_slc, n_top, wbuf):
    j = pl.program_id(1)
    pos0 = n_pages * PAGE
    q = q_ref[...]
    slope = sl_ref[:, 0:1]
    nrow = q.shape[0]

    @pl.when(j == 0)
    def _():
        nh = kc_ref.shape[0]
        cid = lax.broadcasted_iota(jnp.int32, (nrow, nh), 1)
        dist = pos0 - (cid * CMP_STRIDE + (CMP_LEN - 1))
        valid = (dist >= 0) & (cid < n_cmp)
        p = _masked_softmax(_dot_nt(q, kc_ref[...]) - slope * dist.astype(F32), valid)
        ocmp_ref[...] = _dot(p, vc_ref[...])
        imp = _dot2(_dot2r(grp_ref[...], p), mimp_ref[...])
        blk = lax.broadcasted_iota(jnp.int32, imp.shape, 1)
        cur = pos0 // SLC_LEN
        forced = (blk == 0) | (blk == cur) | (blk == cur - 1)
        causal = blk * SLC_LEN <= pos0
        score = jnp.where(forced, BIG, jnp.where(causal, imp, -BIG))
        sel_ref[...] = _topk_mask(score, n_slc, n_top)
        m_ref[...] = jnp.full(m_ref.shape, NEG, F32)
        l_ref[...] = jnp.zeros(l_ref.shape, F32)
        acc_ref[...] = jnp.zeros(acc_ref.shape, F32)

    sel = sel_ref[...]
    e_row = lax.broadcasted_iota(jnp.int32, (sel.shape[1], PAGE), 0)
    e_col = lax.broadcasted_iota(jnp.int32, (sel.shape[1], PAGE), 1) >> 6
    picked = _dot(sel, jnp.where(e_row == 2 * j + e_col, 1.0, 0.0)) > 0.5
    kpos = j * PAGE + lax.broadcasted_iota(jnp.int32, (nrow, PAGE), 1)
    dist = pos0 - kpos
    s = _dot_nt(q, ck_ref[...]) - slope * dist.astype(F32)
    _online_update(s, picked & (dist >= 0), cv_ref[...], m_ref, l_ref, acc_ref)

    @pl.when(j == n_pages - 1)
    def _():
        qf = q.astype(F32)
        new = new_ref[...]
        s_new = jnp.sum(qf * new[0:1, :], axis=-1, keepdims=True)
        ok = sel[:, n_slc - 1:n_slc] > 0.5
        s_new = jnp.where(ok, s_new, NEG)
        m_old = m_ref[...]
        m_new = jnp.maximum(m_old, s_new)
        p_new = jnp.where(ok, jnp.exp(s_new - m_new), 0.0)
        alpha = jnp.exp(m_old - m_new)
        l_tot = alpha * l_ref[...] + p_new
        acc = alpha * acc_ref[...] + p_new * new[1:2, :]
        o_slc = acc / jnp.maximum(l_tot, 1e-30)

        idx = lax.broadcasted_iota(jnp.int32, (nrow, wbuf), 1)
        dw = wbuf - idx
        valid_w = (dw >= 0) & (dw < WINDOW)
        s_w = jnp.where(valid_w, _dot_nt(q, wk_ref[...]) - slope * dw.astype(F32), NEG)
        s_wn = jnp.sum(qf * new[2:3, :], axis=-1, keepdims=True)
        m_w = jnp.maximum(jnp.max(s_w, axis=-1, keepdims=True), s_wn)
        p_w = jnp.where(valid_w, jnp.exp(s_w - m_w), 0.0)
        p_wn = jnp.exp(s_wn - m_w)
        den = jnp.maximum(jnp.sum(p_w, axis=-1, keepdims=True) + p_wn, 1e-30)
        o_win = (_dot(p_w, wv_ref[...]) + p_wn * new[3:4, :]) / den

        gates = g_ref[...]
        ofull_ref[...] = gates[:, 0:1] * ocmp_ref[...] + gates[:, 1:2] * o_slc + gates[:, 2:3] * o_win
        row_g = lax.broadcasted_iota(jnp.int32, (nrow, HD), 0) >> 2
        o = jnp.zeros((nrow, HD), F32)
        for gg in range(KVH):
            o = o + jnp.where(row_g == gg, ofull_ref[:, gg * HD:(gg + 1) * HD], 0.0)
        o_ref[...] = o


def _nsa_sample(page_table, e, qbd, kc, vc, cache_k, cache_v, new_rows, win_k, win_v, gates16, slopes16, grp16,
                mimp, n_cmp, n_slc):
    nb, n_pages = page_table.shape
    nh = kc.shape[1]
    wbuf = win_k.shape[2]
    per_b = lambda *shape: pl.BlockSpec((None,) + shape, lambda b, j, pt: (b,) + (0,) * len(shape))
    fix = lambda *shape: pl.BlockSpec(shape, lambda b, j, pt: (0,) * len(shape))
    page = pl.BlockSpec((None, None, PAGE, KV_DIM), lambda b, j, pt: (e, pt[b, j], 0, 1))
    win = pl.BlockSpec((None, None, wbuf, KV_DIM), lambda b, j, pt: (e, b, 0, 0))
    nrow = qbd.shape[1]
    gs = pltpu.PrefetchScalarGridSpec(
        num_scalar_prefetch=1, grid=(nb, n_pages),
        in_specs=[per_b(nrow, KV_DIM), per_b(nh, KV_DIM), per_b(nh, KV_DIM), page, page, per_b(8, KV_DIM),
                  win, win, per_b(nrow, 128), fix(nrow, 128), fix(nrow, nrow), fix(*mimp.shape)],
        out_specs=per_b(nrow, HD),
        scratch_shapes=[pltpu.VMEM((nrow, 1), F32), pltpu.VMEM((nrow, 1), F32), pltpu.VMEM((nrow, KV_DIM), F32),
                        pltpu.VMEM((nrow, mimp.shape[1]), F32), pltpu.VMEM((nrow, KV_DIM), F32),
                        pltpu.VMEM((nrow, KV_DIM), F32)])
    return pl.pallas_call(
        functools.partial(_nsa_sample_kernel, n_pages=n_pages, n_cmp=n_cmp, n_slc=n_slc,
                          n_top=min(SLC_TOP, n_slc), wbuf=wbuf),
        grid_spec=gs, out_shape=jax.ShapeDtypeStruct((nb, nrow, HD), F32),
        compiler_params=_cp("parallel", "arbitrary"),
    )(page_table, qbd, kc, vc, cache_k, cache_v, new_rows, win_k, win_v, gates16, slopes16, grp16, mimp)


def _outproj_kernel(h_ref, y_ref, o_ref, wp_ref, wn_ref, out_ref):
    acc = h_ref[...] + _dot(y_ref[...], wp_ref[...])
    for g in range(KVH):
        acc = acc + _dot(o_ref[g], wn_ref[g])
    out_ref[...] = acc


def _outproj(h, y_pool, o4, wp, wn, tm):
    m, d = h.shape
    return pl.pallas_call(
        _outproj_kernel, grid=(m // tm,),
        in_specs=[pl.BlockSpec((tm, d), lambda i: (i, 0)), pl.BlockSpec((tm, POOL_DIM), lambda i: (i, 0)),
                  pl.BlockSpec((KVH, tm, 256), lambda i: (0, i, 0)), pl.BlockSpec((POOL_DIM, d), lambda i: (0, 0)),
                  pl.BlockSpec((KVH, 256, d), lambda i: (0, 0, 0))],
        out_specs=pl.BlockSpec((tm, d), lambda i: (i, 0)),
        out_shape=jax.ShapeDtypeStruct((m, d), F32), compiler_params=_cp("parallel"))(h, y_pool, o4, wp, wn)


def _rwkv_proj_kernel(*refs, has_vres):
    if has_vres:
        (xn_ref, xp_ref, mu_ref, wr_ref, wk_ref, wv_ref, w1_ref, w2_ref, a1_ref, a2_ref, g1_ref, g2_ref,
         w0_ref, a0_ref, kk_ref, ka_ref, vf_ref, v0_ref, v1_ref, v2_ref,
         r_out, w_out, k_out, v_out, kk_out, b_out, g_out) = refs
    else:
        (xn_ref, xp_ref, mu_ref, wr_ref, wk_ref, wv_ref, w1_ref, w2_ref, a1_ref, a2_ref, g1_ref, g2_ref,
         w0_ref, a0_ref, kk_ref, ka_ref,
         r_out, w_out, k_out, v_out, kk_out, b_out, g_out) = refs
    xn = xn_ref[...]
    xx = xp_ref[...] - xn
    mix = lambda j: xn + xx * mu_ref[j:j + 1, :]
    xr, xw, xk, xv, xa, xg = [mix(j) for j in range(6)]
    r_out[...] = _dot(xr, wr_ref[...])
    k = _dot(xk, wk_ref[...])
    v = _dot(xv, wv_ref[...])
    z = w0_ref[...] + _dot(jnp.tanh(_dot(xw, w1_ref[...])), w2_ref[...])
    w_log = -(jnp.maximum(-z, 0.0) + jnp.log(1.0 + jnp.exp(-jnp.abs(z)))) - 0.5
    w_out[...] = -jnp.exp(w_log)
    if has_vres:
        v = v + (vf_ref[...] - v) * _sigmoid(v0_ref[...] + _dot(_dot(xv, v1_ref[...]), v2_ref[...]))
    v_out[...] = v
    a = _sigmoid(a0_ref[...] + _dot(_dot(xa, a1_ref[...]), a2_ref[...]))
    g_out[...] = _dot(_sigmoid(_dot(xg, g1_ref[...])), g2_ref[...])
    kk = k * kk_ref[...]
    for c in range(kk.shape[1] // 128):
        sl = slice(c * 128, (c + 1) * 128)
        kt = kk[:, sl]
        kn = kt / jnp.maximum(jnp.sqrt(_seg64_sum(kt * kt)), 1e-12)
        kk_out[:, sl] = kn
        b_out[:, sl] = kn * a[:, sl]
    k_out[...] = k * (1.0 + (a - 1.0) * ka_ref[...])


def _rwkv_proj(xn, xprev, p, vres, tm):
    m, d = xn.shape
    row = pl.BlockSpec((tm, d), lambda i: (i, 0))
    fix = lambda a: pl.BlockSpec(a.shape, lambda i: (0,) * a.ndim)
    args = [xn, xprev, p['mu'], p['wr'], p['wk'], p['wv'], p['w1'], p['w2'], p['a1'], p['a2'], p['g1'], p['g2'],
            p['w0'], p['a0'], p['k_k'], p['k_a']]
    specs = [row, row] + [fix(a) for a in args[2:]]
    if vres is not None:
        vf, v0, v1, v2 = vres
        args += [vf, v0, v1, v2]
        specs += [row, fix(v0), fix(v1), fix(v2)]
    return pl.pallas_call(
        functools.partial(_rwkv_proj_kernel, has_vres=vres is not None), grid=(m // tm,),
        in_specs=specs, out_specs=[row] * 7, out_shape=[jax.ShapeDtypeStruct((m, d), F32)] * 7,
        compiler_params=_cp("parallel"))(*args)


_NN = (((1,), (0,)), ((), ()))
_NT = (((1,), (1,)), ((), ()))
_TN = (((0,), (0,)), ((), ()))


def _split(a):
    hi = a.astype(_MX)
    return hi, (a - hi.astype(F32)).astype(_MX)


def _dot3(a, b, dims=_NN):
    dg = lambda x, y: lax.dot_general(x, y, dims, preferred_element_type=F32)
    a_hi, a_lo = _split(a)
    b_hi, b_lo = _split(b)
    return dg(a_hi, b_hi) + dg(a_lo, b_hi) + dg(a_hi, b_lo)


def _wkv_kernel(r_ref, lw_ref, k_ref, v_ref, kk_ref, b_ref, s0_ref, o_ref, s_ref, *, hb):
    @pl.when(pl.program_id(2) == 0)
    def _():
        s_ref[...] = s0_ref[...]

    c = r_ref.shape[1]
    row = lax.broadcasted_iota(jnp.int32, (c, c), 0)
    col = lax.broadcasted_iota(jnp.int32, (c, c), 1)
    incl = col <= row
    strict = col < row
    ltri = jnp.where(incl, 1.0, 0.0)
    levels = int(math.log2(c))
    heads = range(hb)
    lw = [lw_ref[h] for h in heads]
    cum = [_dot3(ltri, lw[h]) for h in heads]
    g_in = [jnp.exp(cum[h]) for h in heads]
    g_inv = [jnp.exp(-cum[h]) for h in heads]
    b_t = [b_ref[h] * g_inv[h] for h in heads]
    k_t = [k_ref[h] * g_inv[h] for h in heads]
    ar = [jnp.concatenate([-kk_ref[h] * jnp.exp(cum[h] - lw[h]), r_ref[h] * g_in[h]], axis=0) for h in heads]
    pb = [_dot3(ar[h], b_t[h], _NT) for h in heads]
    pk = [_dot3(ar[h], k_t[h], _NT) for h in heads]
    xs = [_dot3(ar[h], s_ref[h], _NT) for h in heads]
    n_mat = [jnp.where(strict, pb[h][:c], 0.0) for h in heads]
    x = [xs[h][:c] + _dot3(jnp.where(strict, pk[h][:c], 0.0), v_ref[h]) for h in heads]
    for lvl in range(levels):
        x = [x[h] + _dot3(n_mat[h], x[h]) for h in heads]
        if lvl + 1 < levels:
            n_mat = [_dot3(n_mat[h], n_mat[h]) for h in heads]
    for h in heads:
        o_ref[h] = (xs[h][c:] + _dot3(jnp.where(incl, pb[h][c:], 0.0), x[h])
                    + _dot3(jnp.where(incl, pk[h][c:], 0.0), v_ref[h]))
    for h in heads:
        s_ref[h] = (s_ref[h] + _dot3(x[h], b_t[h], _TN) + _dot3(v_ref[h], k_t[h], _TN)) * g_in[h][c - 1:c, :]


def _wkv_scan(r, lw, k, v, kk, b, s0, tc, hb):
    n, nh_, t, _ = r.shape
    tok = pl.BlockSpec((None, hb, tc, HD), lambda bb, hh, tt: (bb, hh, tt, 0))
    st = pl.BlockSpec((None, hb, HD, HD), lambda bb, hh, tt: (bb, hh, 0, 0))
    return pl.pallas_call(
        functools.partial(_wkv_kernel, hb=hb), grid=(n, nh_ // hb, t // tc),
        in_specs=[tok] * 6 + [st], out_specs=[tok, st],
        out_shape=[jax.ShapeDtypeStruct((n, nh_, t, HD), F32), jax.ShapeDtypeStruct((n, nh_, HD, HD), F32)],
        compiler_params=_cp("parallel", "parallel", "arbitrary"))(r, lw, k, v, kk, b, s0)


def _wkv_step_kernel(r_ref, lw_ref, k_ref, kk_ref, b_ref, vc_ref, s0_ref, o_ref, s_ref):
    s = s0_ref[...]
    sa = -jnp.sum(s * kk_ref[...], axis=-1, keepdims=True)
    s = s * jnp.exp(lw_ref[...]) + sa * b_ref[...] + vc_ref[...] * k_ref[...]
    s_ref[...] = s
    o_ref[...] = jnp.sum(s * r_ref[...], axis=-1, keepdims=True)


def _wkv_step(r, w, k, kk, b, v_col, s0):
    n, nh_ = r.shape[:2]
    rowv = pl.BlockSpec((None, nh_, 1, HD), lambda i: (i, 0, 0, 0))
    colv = pl.BlockSpec((None, nh_, HD, 1), lambda i: (i, 0, 0, 0))
    st = pl.BlockSpec((None, nh_, HD, HD), lambda i: (i, 0, 0, 0))
    return pl.pallas_call(
        _wkv_step_kernel, grid=(n,), in_specs=[rowv] * 5 + [colv, st], out_specs=[colv, st],
        out_shape=[jax.ShapeDtypeStruct((n, nh_, HD, 1), F32), jax.ShapeDtypeStruct((n, nh_, HD, HD), F32)],
        compiler_params=_cp("parallel"))(r, w, k, kk, b, v_col, s0)


def _rwkv_out_kernel(h_ref, o_ref, r_ref, k_ref, v_ref, g_ref, lnw_ref, lnb_ref, rk_ref, wo_ref, out_ref, y_ref):
    for c in range(h_ref.shape[1] // 128):
        sl = slice(c * 128, (c + 1) * 128)
        o = o_ref[:, sl]
        dlt = o - _seg64_sum(o) * (1.0 / HD)
        var = _seg64_sum(dlt * dlt) * (1.0 / HD)
        on = dlt * lax.rsqrt(var + LN_X_EPS) * lnw_ref[:, sl] + lnb_ref[:, sl]
        bonus = _seg64_sum(r_ref[:, sl] * k_ref[:, sl] * rk_ref[:, sl])
        y_ref[:, sl] = ((on + bonus * v_ref[:, sl]) * g_ref[:, sl]).astype(y_ref.dtype)
    out_ref[...] = h_ref[...] + jnp.dot(y_ref[...], wo_ref[...], preferred_element_type=F32)


def _rwkv_out(h, o, r, k, v, g, lnw, lnb, rk, wo, tm):
    m, d = h.shape
    row = pl.BlockSpec((tm, d), lambda i: (i, 0))
    vec = pl.BlockSpec((1, d), lambda i: (0, 0))
    return pl.pallas_call(
        _rwkv_out_kernel, grid=(m // tm,),
        in_specs=[row] * 6 + [vec] * 3 + [pl.BlockSpec((d, d), lambda i: (0, 0))],
        out_specs=row, out_shape=jax.ShapeDtypeStruct((m, d), F32),
        scratch_shapes=[pltpu.VMEM((tm, d), _MX)],
        compiler_params=_cp("parallel"))(h, o, r, k, v, g, lnw, lnb, rk, wo)


def _row_tile(m, pref):
    return pref if m % pref == 0 else m


def _block_diag(w):
    g, a, b = w.shape
    eye = jnp.eye(g, dtype=w.dtype)
    return (eye[:, None, :, None] * w[:, :, None, :]).reshape(g * a, g * b)


def _imp_matrix(nh, n_cmp, n_slc, width):
    per = SLC_LEN // CMP_STRIDE
    lead = CMP_LEN // CMP_STRIDE - 1
    c = np.arange(nh)[:, None]
    j = np.arange(width)[None, :]
    m = (c - per * j >= -lead) & (c - per * j < per) & (c < n_cmp) & (j < n_slc)
    return jnp.asarray(m.astype(np.float32), dtype=_MX)


def _pad_lanes(n):
    return -(-n // 128) * 128


def _heads(x, n, t):
    return x.reshape(n, t, -1, HD).transpose(0, 2, 1, 3).astype(_MX)


def kernel(x_prompt, x_sample, cache_k, cache_v, page_table, state_win_k, state_win_v, state_pool, state_shift, state_wkv, p_prompt, p_sample, norm_mix, norm_mlp, norm_ple, mlp_w1, mlp_w2, ple_proj, ple_gate, even_w_in, even_w_out, pool_w, pool_scale, q_gain, k_gain, cmp_pe, cmp_w, rwkv_mu, rwkv_wr, rwkv_wk, rwkv_wv, rwkv_wo, rwkv_w0, rwkv_w1, rwkv_w2, rwkv_a0, rwkv_a1, rwkv_a2, rwkv_v0, rwkv_v1, rwkv_v2, rwkv_g1, rwkv_g2, rwkv_kk, rwkv_ka, rwkv_rk, rwkv_lnw, rwkv_lnb):
    nb, t, d = x_prompt.shape
    ns = x_sample.shape[0]
    depth = norm_mix.shape[0]
    n_even = even_w_in.shape[0]
    n_pages = page_table.shape[1]
    past_len = n_pages * PAGE
    wbuf = state_win_k.shape[2]
    n_phys = cache_k.shape[1]
    mp = nb * t
    tm_p = _row_tile(mp, 512)
    tm_a = _row_tile(mp, 256)
    slopes = jnp.asarray(_alibi_slopes(KVH * REP))
    mx = lambda a: a.astype(_MX)
    row1 = lambda a: a.reshape(1, -1)
    two = lambda a: jnp.tile(a.reshape(1, HD), (1, 2))

    rid = np.arange(16)
    real = (rid % 4) < REP
    slopes16 = jnp.asarray(np.where(real, _alibi_slopes(KVH * REP)[np.minimum((rid // 4) * REP + rid % 4, 11)],
                                    0.0).astype(np.float32))[:, None] * jnp.ones((1, 128), F32)
    grp16 = jnp.asarray(((rid[:, None] // 4 == rid[None, :] // 4) & real[None, :]).astype(np.float32))

    cache_k5 = cache_k.reshape(n_even, n_phys, PAGE, 2 * KV_DIM)
    cache_v5 = cache_v.reshape(n_even, n_phys, PAGE, 2 * KV_DIM)
    win_k4 = state_win_k.reshape(n_even, ns, wbuf, KV_DIM)
    win_v4 = state_win_v.reshape(n_even, ns, wbuf, KV_DIM)

    h_p = x_prompt.reshape(mp, d)
    h_s = x_sample.reshape(ns, d)
    outs = {k_: [] for k_ in ('nk_p', 'nv_p', 'nk_s', 'nv_s', 'wk_p', 'wv_p', 'wk_s', 'wv_s', 'pl_p', 'pl_s',
                              'sh_p', 'sh_s', 'st_p', 'st_s')}
    vf_p = vf_s = None

    for i in range(depth):
        gn = row1(norm_mix[i])
        if i % 2 == 0:
            e = i // 2
            w_in = mx(jnp.pad(even_w_in[e], ((0, 0), (0, IN_PAD - even_w_in.shape[2]))))
            wbd = mx(_block_diag(pool_w[e]))
            psc = row1(pool_scale[e])
            qg, ksg, kwg, kcg = two(q_gain[e]), two(k_gain[e, 1]), two(k_gain[e, 2]), two(k_gain[e, 0])
            w_out = even_w_out[e]
            wo_pool = mx(w_out[:POOL_DIM])
            wo_nsa = mx(jnp.pad(w_out[POOL_DIM:].reshape(KVH, REP * HD, d), ((0, 0), (0, 256 - REP * HD), (0, 0))))
            pe_k = jnp.tile(cmp_pe[e, 0], (1, 2))
            pe_v = jnp.tile(cmp_pe[e, 1], (1, 2))
            eye = jnp.eye(2, dtype=F32)
            bd = lambda w: mx((eye[None, :, None, :, None] * w[:, None, :, None, :]).reshape(CMP_LEN, 128, 128))
            cw_k, cw_v = bd(cmp_w[e, 0]), bd(cmp_w[e, 1])

            u, q, nk, nv, kw, vw, gate = _inproj(h_p, gn, w_in, qg, ksg, kwg, tm_a)
            y_pool = _pool_prompt(u.reshape(nb, t, POOL_DIM), wbd, psc, _row_tile(t, 512))
            nh_p = t // CMP_STRIDE
            pt_p = jnp.arange(nb * (t // PAGE), dtype=jnp.int32).reshape(nb, t // PAGE)
            kc, vc = _compress(nk.reshape(1, mp // PAGE, PAGE, 2 * KV_DIM), nv.reshape(1, mp // PAGE, PAGE, 2 * KV_DIM),
                               pt_p, 0, pe_k, pe_v, cw_k, cw_v, kcg)
            n_cmp_p = nh_p - (CMP_LEN // CMP_STRIDE - 1)
            n_slc_p = t // SLC_LEN
            gates_t = gate[:, :KVH * REP * 3].reshape(nb, t, KVH, REP * 3).transpose(0, 2, 3, 1)
            chunks_t = lambda x: (x.reshape(nb, t // KEY_TILE, KEY_TILE, KVH, HD).transpose(0, 3, 1, 4, 2).astype(_MX))
            o4t = _nsa_prompt(slopes, _heads(q, nb, t), _heads(kc, nb, nh_p),
                              vc.reshape(nb, nh_p, KVH, HD).transpose(0, 2, 3, 1).astype(_MX),
                              _heads(nk[:, KV_DIM:], nb, t), chunks_t(nv[:, KV_DIM:]),
                              _heads(kw, nb, t), chunks_t(vw), gates_t,
                              _imp_matrix(nh_p, n_cmp_p, n_slc_p, -(-n_slc_p // 8) * 8).T, n_cmp_p)
            o4 = o4t.transpose(0, 1, 3, 2).reshape(KVH, mp, 256)
            h_p = _outproj(h_p, y_pool.reshape(mp, POOL_DIM), o4, wo_pool, wo_nsa, tm_p)
            outs['nk_p'].append(nk.reshape(nb, t, 2, KVH, HD))
            outs['nv_p'].append(nv.reshape(nb, t, 2, KVH, HD))
            kw3 = kw.reshape(nb, t, KVH, HD)
            vw3 = vw.reshape(nb, t, KVH, HD)
            if t < wbuf:
                zpad = jnp.zeros((nb, wbuf - t, KVH, HD), F32)
                kw3, vw3 = jnp.concatenate([zpad, kw3], 1), jnp.concatenate([zpad, vw3], 1)
            outs['wk_p'].append(kw3[:, -wbuf:])
            outs['wv_p'].append(vw3[:, -wbuf:])
            outs['pl_p'].append(u.reshape(nb, t, POOL_DIM)[:, -POOL_BUF:])

            u, q, nk, nv, kw, vw, gate = _inproj(h_s, gn, w_in, qg, ksg, kwg, ns)
            y_pool = _pool_step(state_pool[e], u, wbd, psc, past_len)
            kc, vc = _compress(cache_k5, cache_v5, page_table, e, pe_k, pe_v, cw_k, cw_v, kcg)
            nh_s = kc.shape[1]
            n_cmp_s = nh_s - (CMP_LEN // CMP_STRIDE - 1)
            n_slc_s = -(-(past_len + 1) // SLC_LEN)
            q16 = jnp.pad(q.reshape(ns, KVH, REP, HD), ((0, 0), (0, 0), (0, 1), (0, 0))).reshape(ns, 16, 1, HD)
            gsel = jnp.asarray((np.arange(16)[:, None] // 4 == np.arange(KVH)[None, :]).astype(np.float32))
            qbd = (q16.astype(F32) * gsel[None, :, :, None]).reshape(ns, 16, KV_DIM).astype(_MX)
            new_rows = jnp.pad(jnp.stack([nk[:, KV_DIM:], nv[:, KV_DIM:], kw, vw], axis=1), ((0, 0), (0, 4), (0, 0)))
            g16 = jnp.pad(gate[:, :KVH * REP * 3].reshape(ns, KVH, REP, 3), ((0, 0), (0, 0), (0, 1), (0, 125)))
            o16 = _nsa_sample(page_table, e, qbd, mx(kc), mx(vc), cache_k5, cache_v5, new_rows, win_k4, win_v4,
                              g16.reshape(ns, 16, 128), slopes16, grp16,
                              _imp_matrix(nh_s, n_cmp_s, n_slc_s, _pad_lanes(n_slc_s)), n_cmp_s, n_slc_s)
            o4 = o16.reshape(ns, KVH, 4, HD)[:, :, :REP].reshape(ns, KVH, REP * HD).transpose(1, 0, 2)
            o4 = jnp.pad(o4, ((0, 0), (0, 0), (0, 256 - REP * HD)))
            h_s = _outproj(h_s, y_pool, o4, wo_pool, wo_nsa, ns)
            outs['nk_s'].append(nk.reshape(ns, 1, 2, KVH, HD))
            outs['nv_s'].append(nv.reshape(ns, 1, 2, KVH, HD))
            outs['wk_s'].append(jnp.concatenate([state_win_k[e], kw.reshape(ns, 1, KVH, HD)], axis=1)[:, -wbuf:])
            outs['wv_s'].append(jnp.concatenate([state_win_v[e], vw.reshape(ns, 1, KVH, HD)], axis=1)[:, -wbuf:])
            outs['pl_s'].append(jnp.concatenate([state_pool[e], u[:, None]], axis=1)[:, -POOL_BUF:])
        else:
            o = i // 2
            lora_in = lambda a: mx(jnp.pad(a, ((0, 0), (0, LORA_PAD - a.shape[1]))))
            lora_out = lambda a: mx(jnp.pad(a, ((0, LORA_PAD - a.shape[0]), (0, 0))))
            p = dict(mu=rwkv_mu[o], wr=mx(rwkv_wr[o]), wk=mx(rwkv_wk[o]), wv=mx(rwkv_wv[o]),
                     w1=lora_in(rwkv_w1[o]), w2=lora_out(rwkv_w2[o]), a1=lora_in(rwkv_a1[o]), a2=lora_out(rwkv_a2[o]),
                     g1=lora_in(rwkv_g1[o]), g2=lora_out(rwkv_g2[o]), w0=row1(rwkv_w0[o]), a0=row1(rwkv_a0[o]),
                     k_k=row1(rwkv_kk[o]), k_a=row1(rwkv_ka[o]))
            vparams = None if o == 0 else (row1(rwkv_v0[o - 1]), lora_in(rwkv_v1[o - 1]), lora_out(rwkv_v2[o - 1]))
            lnw, lnb, rk, wo = row1(rwkv_lnw[o]), row1(rwkv_lnb[o]), row1(rwkv_rk[o]), mx(rwkv_wo[o])
            nhd = d // HD

            xn = _norm(h_p, gn, tm_p)
            xn3 = xn.reshape(nb, t, d)
            xprev = jnp.concatenate([jnp.zeros((nb, 1, d), F32), xn3[:, :-1]], axis=1).reshape(mp, d)
            vres = None if o == 0 else (vf_p,) + vparams
            r, w, k, v, kk, b, g = _rwkv_proj(xn, xprev, p, vres, tm_a)
            if o == 0:
                vf_p = v
            hm = lambda a: a.reshape(nb, t, nhd, HD).transpose(0, 2, 1, 3)
            o_hm, s_fin = _wkv_scan(hm(r), hm(w), hm(k), hm(v), hm(kk), hm(b), jnp.zeros((nb, nhd, HD, HD), F32),
                                    WKV_CHUNK, 8)
            o_tok = o_hm.transpose(0, 2, 1, 3).reshape(mp, d)
            h_p = _rwkv_out(h_p, o_tok, r, k, v, g, lnw, lnb, rk, wo, tm_a)
            outs['sh_p'].append(xn3[:, -1])
            outs['st_p'].append(s_fin)

            xn = _norm(h_s, gn, ns)
            vres = None if o == 0 else (vf_s,) + vparams
            r, w, k, v, kk, b, g = _rwkv_proj(xn, state_shift[o], p, vres, ns)
            if o == 0:
                vf_s = v
            rows = lambda a: a.reshape(ns, nhd, 1, HD)
            o_col, s_fin = _wkv_step(rows(r), rows(w), rows(k), rows(kk), rows(b), v.reshape(ns, nhd, HD, 1),
                                     state_wkv[o])
            h_s = _rwkv_out(h_s, o_col.reshape(ns, d), r, k, v, g, lnw, lnb, rk, wo, ns)
            outs['sh_s'].append(xn)
            outs['st_s'].append(s_fin)

        w1, w2 = mx(mlp_w1[i]), mx(mlp_w2[i])
        gm, gp, wg, wp = row1(norm_mlp[i]), row1(norm_ple[i]), mx(ple_gate[i]), mx(ple_proj[i])
        h_p = _mlp(h_p, gm, w1, w2, tm_p, 1024)
        h_p = _ple(h_p, p_prompt[i].reshape(mp, -1), gp, wg, wp, tm_p)
        h_s = _mlp(h_s, gm, w1, w2, ns, 1024)
        h_s = _ple(h_s, p_sample[i].reshape(ns, -1), gp, wg, wp, ns)

    st = lambda name: jnp.stack(outs[name])
    return (h_p.reshape(nb, t, d), h_s.reshape(ns, 1, d), st('nk_p'), st('nv_p'), st('nk_s'), st('nv_s'),
            st('wk_p'), st('wv_p'), st('wk_s'), st('wv_s'), st('pl_p'), st('pl_s'),
            st('sh_p'), st('sh_s'), st('st_p'), st('st_s'))
```

```python
import functools
import math

import numpy as np
import jax
import jax.numpy as jnp
from jax import lax
from jax.experimental import pallas as pl
from jax.experimental.pallas import tpu as pltpu

F32 = jnp.float32
_MX = jnp.bfloat16

HD = 64
POOL_DIM = 256
POOL_WINDOWS = (2, 4, 8, 16)
POOL_BUF = 15
KVH = 4
REP = 3
NSA_DIM = KVH * REP * HD
KV_DIM = KVH * HD
CMP_LEN, CMP_STRIDE = 32, 16
SLC_LEN, SLC_TOP = 64, 16
WINDOW = 512
QBLK = 256
KEY_TILE = 256
PAGE = 128
RMS_EPS = 1e-6
LN_X_EPS = 64e-5
NEG = -1e30
BIG = 1e9
WKV_CHUNK = 64
LORA_PAD = 128
IN_PAD = 2688
VMEM_LIMIT = 56 * 1024 * 1024


def _cp(*sem):
    return pltpu.CompilerParams(dimension_semantics=sem, vmem_limit_bytes=VMEM_LIMIT)


def _alibi_slopes(n):
    p = 2 ** int(math.floor(math.log2(n)))
    s = [2.0 ** (-8.0 * (i + 1) / p) for i in range(p)]
    if p < n:
        s += [2.0 ** (-8.0 * (i + 1) / (2 * p)) for i in range(0, 2 * p, 2)][: n - p]
    return np.asarray(s, dtype=np.float32)


def _rms(x, g):
    return x * lax.rsqrt(jnp.mean(x * x, axis=-1, keepdims=True) + RMS_EPS) * g


def _sigmoid(x):
    return 1.0 / (1.0 + jnp.exp(-x))


def _dot(a, b):
    return jnp.dot(a.astype(_MX), b.astype(_MX), preferred_element_type=F32)


def _dot_nt(a, b):
    return lax.dot_general(a.astype(_MX), b.astype(_MX), (((1,), (1,)), ((), ())),
                           preferred_element_type=F32)


def _dot2(a, b):
    hi = a.astype(_MX)
    lo = (a - hi.astype(F32)).astype(_MX)
    b = b.astype(_MX)
    return (jnp.dot(hi, b, preferred_element_type=F32) + jnp.dot(lo, b, preferred_element_type=F32))


def _dot2r(a, b):
    hi = b.astype(_MX)
    lo = (b - hi.astype(F32)).astype(_MX)
    a = a.astype(_MX)
    return (jnp.dot(a, hi, preferred_element_type=F32) + jnp.dot(a, lo, preferred_element_type=F32))


def _seg64_sum(y):
    left = lax.broadcasted_iota(jnp.int32, y.shape, 1) < HD
    sa = jnp.sum(jnp.where(left, y, 0.0), axis=-1, keepdims=True)
    sb = jnp.sum(jnp.where(left, 0.0, y), axis=-1, keepdims=True)
    return jnp.where(left, sa, sb)


def _head_rms_tile(zt, gain2):
    ms = _seg64_sum(zt * zt) * (1.0 / HD)
    return zt * lax.rsqrt(ms + RMS_EPS) * gain2


def _masked_softmax(s, valid):
    s = jnp.where(valid, s, NEG)
    e = jnp.where(valid, jnp.exp(s - jnp.max(s, axis=-1, keepdims=True)), 0.0)
    return e / jnp.maximum(jnp.sum(e, axis=-1, keepdims=True), 1e-30)


def _online_update(s, valid, v, m_ref, l_ref, acc_ref):
    s = jnp.where(valid, s, NEG)
    m_old = m_ref[...]
    m_new = jnp.maximum(m_old, jnp.max(s, axis=-1, keepdims=True))
    p = jnp.where(valid, jnp.exp(s - m_new), 0.0)
    alpha = jnp.exp(m_old - m_new)
    l_ref[...] = alpha * l_ref[...] + jnp.sum(p, axis=-1, keepdims=True)
    acc_ref[...] = alpha * acc_ref[...] + _dot(p, v)
    m_ref[...] = m_new


def _topk_mask(score, n_cand, n_top):
    col_id = lax.broadcasted_iota(jnp.int32, score.shape, 1)
    rank = jnp.zeros(score.shape, F32)
    for j0 in range(n_cand):
        cj = score[:, j0:j0 + 1]
        beats = jnp.where(cj > score, 1.0, jnp.where((cj == score) & (col_id > j0), 1.0, 0.0))
        rank = rank + beats
    return jnp.where((rank < n_top) & (col_id < n_cand), 1.0, 0.0)


def _norm_kernel(x_ref, g_ref, o_ref):
    o_ref[...] = _rms(x_ref[...], g_ref[...])


def _norm(x, g, tm):
    m, d = x.shape
    return pl.pallas_call(
        _norm_kernel, grid=(m // tm,),
        in_specs=[pl.BlockSpec((tm, d), lambda i: (i, 0)), pl.BlockSpec((1, d), lambda i: (0, 0))],
        out_specs=pl.BlockSpec((tm, d), lambda i: (i, 0)),
        out_shape=jax.ShapeDtypeStruct((m, d), F32), compiler_params=_cp("parallel"))(x, g)


def _mlp_kernel(h_ref, g_ref, w1_ref, w2_ref, o_ref, xn_ref):
    @pl.when(pl.program_id(1) == 0)
    def _():
        x = h_ref[...]
        xn_ref[...] = _rms(x, g_ref[...]).astype(_MX)
        o_ref[...] = x

    a = jnp.dot(xn_ref[...], w1_ref[...], preferred_element_type=F32)
    a = jnp.square(jnp.maximum(a, 0.0))
    o_ref[...] += _dot(a, w2_ref[...])


def _mlp(h, g, w1, w2, tm, tf):
    m, d = h.shape
    dff = w1.shape[1]
    return pl.pallas_call(
        _mlp_kernel, grid=(m // tm, dff // tf),
        in_specs=[pl.BlockSpec((tm, d), lambda i, j: (i, 0)), pl.BlockSpec((1, d), lambda i, j: (0, 0)),
                  pl.BlockSpec((d, tf), lambda i, j: (0, j)), pl.BlockSpec((tf, d), lambda i, j: (j, 0))],
        out_specs=pl.BlockSpec((tm, d), lambda i, j: (i, 0)),
        out_shape=jax.ShapeDtypeStruct((m, d), F32),
        scratch_shapes=[pltpu.VMEM((tm, d), _MX)],
        compiler_params=_cp("parallel", "arbitrary"))(h, g, w1, w2)


def _ple_kernel(h_ref, p_ref, g_ref, wg_ref, wp_ref, o_ref):
    h = h_ref[...]
    gate = _sigmoid(_dot(_rms(h, g_ref[...]), wg_ref[...]))
    o_ref[...] = h + gate * _dot(p_ref[...], wp_ref[...])


def _ple(h, p, g, wg, wp, tm):
    m, d = h.shape
    pd = p.shape[1]
    return pl.pallas_call(
        _ple_kernel, grid=(m // tm,),
        in_specs=[pl.BlockSpec((tm, d), lambda i: (i, 0)), pl.BlockSpec((tm, pd), lambda i: (i, 0)),
                  pl.BlockSpec((1, d), lambda i: (0, 0)), pl.BlockSpec((d, d), lambda i: (0, 0)),
                  pl.BlockSpec((pd, d), lambda i: (0, 0))],
        out_specs=pl.BlockSpec((tm, d), lambda i: (i, 0)),
        out_shape=jax.ShapeDtypeStruct((m, d), F32), compiler_params=_cp("parallel"))(h, p, g, wg, wp)


_SEG_U = 0
_SEG_Q = POOL_DIM
_SEG_KV = POOL_DIM + NSA_DIM
_SEG_GL = _SEG_KV + 6 * KV_DIM


def _inproj_kernel(x_ref, gn_ref, w_ref, qg_ref, ksg_ref, kwg_ref,
                   u_ref, q_ref, nk_ref, nv_ref, kw_ref, vw_ref, gate_ref, z_ref):
    xn = _rms(x_ref[...], gn_ref[...])
    z_ref[...] = _dot(xn, w_ref[...])
    u_ref[...] = z_ref[:, _SEG_U:_SEG_U + POOL_DIM]
    for c in range(NSA_DIM // 128):
        zt = z_ref[:, _SEG_Q + c * 128:_SEG_Q + (c + 1) * 128]
        q_ref[:, c * 128:(c + 1) * 128] = (_head_rms_tile(zt, qg_ref[...]) * (HD ** -0.5)).astype(q_ref.dtype)
    kv = _SEG_KV
    nk_ref[:, 0:KV_DIM] = z_ref[:, kv:kv + KV_DIM]
    nv_ref[:, 0:KV_DIM] = z_ref[:, kv + KV_DIM:kv + 2 * KV_DIM]
    nv_ref[:, KV_DIM:2 * KV_DIM] = z_ref[:, kv + 3 * KV_DIM:kv + 4 * KV_DIM]
    vw_ref[...] = z_ref[:, kv + 5 * KV_DIM:kv + 6 * KV_DIM]
    for c in range(KV_DIM // 128):
        zs = z_ref[:, kv + 2 * KV_DIM + c * 128:kv + 2 * KV_DIM + (c + 1) * 128]
        nk_ref[:, KV_DIM + c * 128:KV_DIM + (c + 1) * 128] = _head_rms_tile(zs, ksg_ref[...])
        zw = z_ref[:, kv + 4 * KV_DIM + c * 128:kv + 4 * KV_DIM + (c + 1) * 128]
        kw_ref[:, c * 128:(c + 1) * 128] = _head_rms_tile(zw, kwg_ref[...])
    gate_ref[...] = _sigmoid(z_ref[:, _SEG_GL:_SEG_GL + 128])


def _inproj(x, gn, w, qg, ksg, kwg, tm):
    m, d = x.shape
    row = lambda i: (i, 0)
    fix = lambda i: (0, 0)
    outs = [(POOL_DIM, F32), (NSA_DIM, _MX), (2 * KV_DIM, F32), (2 * KV_DIM, F32), (KV_DIM, F32), (KV_DIM, F32),
            (128, F32)]
    return pl.pallas_call(
        _inproj_kernel, grid=(m // tm,),
        in_specs=[pl.BlockSpec((tm, d), row), pl.BlockSpec((1, d), fix), pl.BlockSpec((d, IN_PAD), fix),
                  pl.BlockSpec((1, 128), fix), pl.BlockSpec((1, 128), fix), pl.BlockSpec((1, 128), fix)],
        out_specs=[pl.BlockSpec((tm, c), row) for c, _ in outs],
        out_shape=[jax.ShapeDtypeStruct((m, c), dt) for c, dt in outs],
        scratch_shapes=[pltpu.VMEM((tm, IN_PAD), F32)],
        compiler_params=_cp("parallel"))(x, gn, w, qg, ksg, kwg)


def _pool_select(sums, u, cnt):
    grp = lax.broadcasted_iota(jnp.int32, u.shape, 1) >> 6
    ssel = jnp.where(grp == 0, sums[2], jnp.where(grp == 1, sums[4], jnp.where(grp == 2, sums[8], sums[16])))
    return ssel / cnt - u, grp


def _pool_kernel(u_ref, wbd_ref, sc_ref, y_ref, ext_ref, *, tm):
    i = pl.program_id(1)

    @pl.when(i == 0)
    def _():
        ext_ref[0:16, :] = jnp.zeros((16, POOL_DIM), F32)

    u = u_ref[...]
    ext_ref[16:16 + tm, :] = u
    acc = u
    sums = {}
    for s in range(1, 16):
        acc = acc + ext_ref[16 - s:16 - s + tm, :]
        if s + 1 in POOL_WINDOWS:
            sums[s + 1] = acc
    grp = lax.broadcasted_iota(jnp.int32, u.shape, 1) >> 6
    win = jnp.where(grp == 0, 2, jnp.where(grp == 1, 4, jnp.where(grp == 2, 8, 16)))
    pos = i * tm + lax.broadcasted_iota(jnp.int32, u.shape, 0)
    cnt = jnp.minimum(win, pos + 1).astype(F32)
    mixed, _ = _pool_select(sums, u, cnt)
    y_ref[...] = _dot(mixed, wbd_ref[...]) * sc_ref[...]
    ext_ref[0:16, :] = ext_ref[tm:tm + 16, :]


def _pool_prompt(u, wbd, scale, tm):
    n, t, c = u.shape
    return pl.pallas_call(
        functools.partial(_pool_kernel, tm=tm), grid=(n, t // tm),
        in_specs=[pl.BlockSpec((None, tm, c), lambda b, i: (b, i, 0)), pl.BlockSpec((c, c), lambda b, i: (0, 0)),
                  pl.BlockSpec((1, c), lambda b, i: (0, 0))],
        out_specs=pl.BlockSpec((None, tm, c), lambda b, i: (b, i, 0)),
        out_shape=jax.ShapeDtypeStruct((n, t, c), F32),
        scratch_shapes=[pltpu.VMEM((tm + 16, c), F32)],
        compiler_params=_cp("parallel", "arbitrary"))(u, wbd, scale)


def _pool_step_kernel(buf_ref, u_ref, wbd_ref, sc_ref, y_ref, *, pos0):
    u = u_ref[...]
    acc = u
    sums = {}
    for s in range(1, 16):
        acc = acc + buf_ref[:, POOL_BUF - s, :]
        if s + 1 in POOL_WINDOWS:
            sums[s + 1] = acc
    grp = lax.broadcasted_iota(jnp.int32, u.shape, 1) >> 6
    win = jnp.where(grp == 0, 2, jnp.where(grp == 1, 4, jnp.where(grp == 2, 8, 16)))
    cnt = jnp.minimum(win, pos0 + 1).astype(F32)
    mixed, _ = _pool_select(sums, u, cnt)
    y_ref[...] = _dot(mixed, wbd_ref[...]) * sc_ref[...]


def _pool_step(buf, u, wbd, scale, pos0):
    n, c = u.shape
    full = lambda *shape: pl.BlockSpec(shape, lambda i: (0,) * len(shape))
    return pl.pallas_call(
        functools.partial(_pool_step_kernel, pos0=pos0), grid=(1,),
        in_specs=[full(n, POOL_BUF, c), full(n, c), full(c, c), full(1, c)],
        out_specs=full(n, c), out_shape=jax.ShapeDtypeStruct((n, c), F32),
        compiler_params=_cp("arbitrary"))(buf, u, wbd, scale)


def _pages_per_step(n_pages):
    return next(p for p in (8, 4, 2, 1) if n_pages % p == 0)


def _compress_kernel(pt_ref, *refs, n_pages, pps):
    xk_refs, xv_refs = refs[:pps], refs[pps:2 * pps]
    pek_ref, pev_ref, wk_ref, wv_ref, kcg_ref, kc_ref, vc_ref, xs_k, xs_v, bsh = refs[2 * pps:]
    j = pl.program_id(1)
    n_tiles = KV_DIM // 128
    for p in range(pps):
        row0 = pl.multiple_of((j * pps + p) * PAGE, PAGE)
        for c in range(n_tiles):
            xs_k[c, pl.ds(row0, PAGE), :] = xk_refs[p][:, c * 128:(c + 1) * 128]
            xs_v[c, pl.ds(row0, PAGE), :] = xv_refs[p][:, c * 128:(c + 1) * 128]

    @pl.when(j == n_pages // pps - 1)
    def _():
        nh = n_pages * (PAGE // CMP_STRIDE)

        def run(xs, c, pe_ref, w_ref):
            a = jnp.zeros((nh, 128), F32)
            b = jnp.zeros((nh, 128), F32)
            for l in range(CMP_STRIDE):
                xl = xs[c, pl.ds(l, nh, stride=CMP_STRIDE), :]
                a = a + _dot(xl + pe_ref[l:l + 1, :], w_ref[l])
                b = b + _dot(xl + pe_ref[CMP_STRIDE + l:CMP_STRIDE + l + 1, :], w_ref[CMP_STRIDE + l])
            bsh[0:nh, :] = b
            bsh[nh:nh + 8, :] = jnp.zeros((8, 128), F32)
            return a + bsh[1:nh + 1, :]

        for c in range(n_tiles):
            kc_ref[:, c * 128:(c + 1) * 128] = _head_rms_tile(run(xs_k, c, pek_ref, wk_ref), kcg_ref[...])
            vc_ref[:, c * 128:(c + 1) * 128] = run(xs_v, c, pev_ref, wv_ref)


def _compress(cache_k, cache_v, page_table, e, pek, pev, wk, wv, kcg):
    nb, n_pages = page_table.shape
    nh = n_pages * (PAGE // CMP_STRIDE)
    pps = _pages_per_step(n_pages)
    page = lambda p: pl.BlockSpec((None, None, PAGE, KV_DIM), lambda b, j, pt: (e, pt[b, j * pps + p], 0, 0))
    fix2 = lambda b, j, pt: (0, 0)
    fix3 = lambda b, j, pt: (0, 0, 0)
    out = pl.BlockSpec((None, nh, KV_DIM), lambda b, j, pt: (b, 0, 0))
    gs = pltpu.PrefetchScalarGridSpec(
        num_scalar_prefetch=1, grid=(nb, n_pages // pps),
        in_specs=[page(p) for p in range(pps)] * 2
                 + [pl.BlockSpec((CMP_LEN, 128), fix2), pl.BlockSpec((CMP_LEN, 128), fix2),
                    pl.BlockSpec((CMP_LEN, 128, 128), fix3), pl.BlockSpec((CMP_LEN, 128, 128), fix3),
                    pl.BlockSpec((1, 128), fix2)],
        out_specs=[out, out],
        scratch_shapes=[pltpu.VMEM((KV_DIM // 128, n_pages * PAGE, 128), F32),
                        pltpu.VMEM((KV_DIM // 128, n_pages * PAGE, 128), F32),
                        pltpu.VMEM((nh + 8, 128), F32)])
    return pl.pallas_call(
        functools.partial(_compress_kernel, n_pages=n_pages, pps=pps), grid_spec=gs,
        out_shape=[jax.ShapeDtypeStruct((nb, nh, KV_DIM), F32)] * 2,
        compiler_params=_cp("parallel", "arbitrary"),
    )(page_table, *([cache_k] * pps), *([cache_v] * pps), pek, pev, wk, wv, kcg)


def _topk_rows(score, n_top):
    rid = lax.broadcasted_iota(jnp.int32, score.shape, 0).astype(F32)
    sel = jnp.zeros(score.shape, F32)
    for _ in range(n_top):
        m = jnp.max(score, axis=0, keepdims=True)
        first = jnp.min(jnp.where(score == m, rid, 1e9), axis=0, keepdims=True)
        hit = rid == first
        sel = jnp.where(hit, 1.0, sel)
        score = jnp.where(hit, -jnp.inf, score)
    return sel


def _nsa_prompt_kernel(sl_ref, q_ref, kc_ref, vct_ref, ks_ref, vst_ref, kw_ref, vwt_ref, gt_ref, mimpt_ref,
                       o_ref, sel_ref, s_ref, acc_ref, pv_ref, *, n_cmp, n_slc, n_top):
    g = pl.program_id(1)
    i = pl.program_id(2)
    nh = kc_ref.shape[0]
    q = q_ref[...].reshape(REP * QBLK, HD)
    qpos0 = i * QBLK
    pos = qpos0 + lax.broadcasted_iota(jnp.int32, (1, QBLK), 1)
    slope = [sl_ref[g * REP + r] for r in range(REP)]

    cid = lax.broadcasted_iota(jnp.int32, (nh, QBLK), 0)
    dist_c = pos - (cid * CMP_STRIDE + (CMP_LEN - 1))
    valid_c = (dist_c >= 0) & (cid < n_cmp)
    dist_cf = dist_c.astype(F32)
    s_c = _dot_nt(kc_ref[...], q)
    vct = vct_ref[...]
    imp = jnp.zeros((nh, QBLK), F32)
    o_cmp = []
    for r in range(REP):
        s = jnp.where(valid_c, s_c[:, r * QBLK:(r + 1) * QBLK] - slope[r] * dist_cf, NEG)
        e = jnp.where(valid_c, jnp.exp(s - jnp.max(s, axis=0, keepdims=True)), 0.0)
        p = e / jnp.maximum(jnp.sum(e, axis=0, keepdims=True), 1e-30)
        imp = imp + p
        o_cmp.append(_dot(vct, p))
    imp_slc = _dot2r(mimpt_ref[...], imp)
    blk = lax.broadcasted_iota(jnp.int32, imp_slc.shape, 0)
    cur = pos >> 6
    forced = (blk == 0) | (blk == cur) | (blk == cur - 1)
    causal = blk * SLC_LEN <= pos
    score = jnp.where(forced, BIG, jnp.where(causal, imp_slc, -BIG))
    score = jnp.where(blk < n_slc, score, -jnp.inf)
    sel_ref[...] = _topk_rows(score, n_top)

    d0 = (lax.broadcasted_iota(jnp.int32, (KEY_TILE, QBLK), 1)
          - lax.broadcasted_iota(jnp.int32, (KEY_TILE, QBLK), 0))
    hi = (qpos0 + QBLK - 1) // KEY_TILE

    def sweep(lo, k_ref, vt_ref, valid_fn):
        def scores(c):
            k0 = pl.multiple_of(c * KEY_TILE, KEY_TILE)
            return _dot_nt(k_ref[pl.ds(k0, KEY_TILE), :], q)

        def body(c, carry):
            m, l = carry
            s_ref[(c + 1) % 2] = scores(jnp.minimum(c + 1, hi))
            dist = d0 + (qpos0 - c * KEY_TILE)
            mask = jnp.where(valid_fn(c, dist), 0.0, NEG)
            dist_f = dist.astype(F32)
            s = s_ref[c % 2] + jnp.concatenate([mask - slope[r] * dist_f for r in range(REP)], axis=1)
            m_new = jnp.maximum(m, jnp.max(s, axis=0, keepdims=True))
            p = jnp.exp(s - m_new)
            alpha = jnp.exp(m - m_new)
            l = alpha * l + jnp.sum(p, axis=0, keepdims=True)
            acc_ref[...] = alpha * (acc_ref[...] + pv_ref[...])
            pv_ref[...] = _dot(vt_ref[c], p)
            return m_new, l

        s_ref[lo % 2] = scores(lo)
        acc_ref[...] = jnp.zeros(acc_ref.shape, F32)
        pv_ref[...] = jnp.zeros(pv_ref.shape, F32)
        init = (jnp.full((1, REP * QBLK), NEG, F32), jnp.zeros((1, REP * QBLK), F32))
        _, l = lax.fori_loop(lo, hi + 1, body, init)
        return (acc_ref[...] + pv_ref[...]) / l

    def picked(c, dist):
        per_tile = KEY_TILE // SLC_LEN
        rows = [jnp.broadcast_to(sel_ref[pl.ds(per_tile * c + b, 1), :], (SLC_LEN, QBLK)) for b in range(per_tile)]
        return (jnp.concatenate(rows, axis=0) > 0.5) & (dist >= 0)

    o_slc = sweep(0, ks_ref, vst_ref, picked)
    lo_win = jnp.maximum(qpos0 - (WINDOW - 1), 0) // KEY_TILE
    o_win = sweep(lo_win, kw_ref, vwt_ref, lambda c, dist: (dist >= 0) & (dist < WINDOW))

    gates = gt_ref[...]
    for r in range(REP):
        sl = slice(r * QBLK, (r + 1) * QBLK)
        o_ref[r * HD:(r + 1) * HD, :] = (gates[3 * r:3 * r + 1, :] * o_cmp[r] + gates[3 * r + 1:3 * r + 2, :] * o_slc[:, sl]
                                         + gates[3 * r + 2:3 * r + 3, :] * o_win[:, sl])
    o_ref[REP * HD:, :] = jnp.zeros((o_ref.shape[0] - REP * HD, QBLK), F32)


def _nsa_prompt(slopes, q_h, kc_h, vc_t, ks_h, vs_t, kw_h, vw_t, gates_t, mimp_t, n_cmp):
    n, _, t, _ = q_h.shape
    assert t % KEY_TILE == 0
    nh = kc_h.shape[2]
    n_slc = t // SLC_LEN
    nq = t // QBLK
    nkt = t // KEY_TILE
    seq = lambda *shape: pl.BlockSpec((None, None) + shape, lambda b, g, i: (b, g) + (0,) * len(shape))
    return pl.pallas_call(
        functools.partial(_nsa_prompt_kernel, n_cmp=n_cmp, n_slc=n_slc, n_top=min(SLC_TOP, n_slc)),
        grid=(n, KVH, nq),
        in_specs=[pl.BlockSpec(memory_space=pltpu.SMEM),
                  pl.BlockSpec((None, REP, QBLK, HD), lambda b, g, i: (b, g, i, 0)),
                  seq(nh, HD), seq(HD, nh), seq(t, HD), seq(nkt, HD, KEY_TILE), seq(t, HD), seq(nkt, HD, KEY_TILE),
                  pl.BlockSpec((None, None, 9, QBLK), lambda b, g, i: (b, g, 0, i)),
                  pl.BlockSpec(mimp_t.shape, lambda b, g, i: (0, 0))],
        out_specs=pl.BlockSpec((None, None, 256, QBLK), lambda b, g, i: (g, b, 0, i)),
        out_shape=jax.ShapeDtypeStruct((KVH, n, 256, t), F32),
        scratch_shapes=[pltpu.VMEM((mimp_t.shape[0], QBLK), F32), pltpu.VMEM((2, KEY_TILE, REP * QBLK), F32),
                        pltpu.VMEM((HD, REP * QBLK), F32), pltpu.VMEM((HD, REP * QBLK), F32)],
        compiler_params=_cp("parallel", "parallel", "arbitrary"),
    )(slopes, q_h, kc_h, vc_t, ks_h, vs_t, kw_h, vw_t, gates_t, mimp_t)


def _nsa_sample_kernel(pt_ref, *refs, n_pages, pps, n_cmp, n_slc, n_top, wbuf):
    q_ref, kc_ref, vc_ref = refs[:3]
    ck_refs, cv_refs = refs[3:3 + pps], refs[3 + pps:3 + 2 * pps]
    (new_ref, wk_ref, wv_ref, g_ref, sl_ref, grp_ref, mimp_ref, o_ref, m_ref, l_ref, acc_ref, sel_ref, ocmp_ref,
     ofull_ref) = refs[3 + 2 * pps:]
    j = pl.program_id(1)
    span = pps * PAGE
    pos0 = n_pages * PAGE
    q = q_ref[...]
    slope = sl_ref[:, 0:1]
    nrow = q.shape[0]

    @pl.when(j == 0)
    def _():
        nh = kc_ref.shape[0]
        cid = lax.broadcasted_iota(jnp.int32, (nrow, nh), 1)
        dist = pos0 - (cid * CMP_STRIDE + (CMP_LEN - 1))
        valid = (dist >= 0) & (cid < n_cmp)
        p = _masked_softmax(_dot_nt(q, kc_ref[...]) - slope * dist.astype(F32), valid)
        ocmp_ref[...] = _dot(p, vc_ref[...])
        imp = _dot2(_dot2r(grp_ref[...], p), mimp_ref[...])
        blk = lax.broadcasted_iota(jnp.int32, imp.shape, 1)
        cur = pos0 // SLC_LEN
        forced = (blk == 0) | (blk == cur) | (blk == cur - 1)
        causal = blk * SLC_LEN <= pos0
        score = jnp.where(forced, BIG, jnp.where(causal, imp, -BIG))
        sel_ref[...] = _topk_mask(score, n_slc, n_top)
        m_ref[...] = jnp.full(m_ref.shape, NEG, F32)
        l_ref[...] = jnp.zeros(l_ref.shape, F32)
        acc_ref[...] = jnp.zeros(acc_ref.shape, F32)

    sel = sel_ref[...]
    e_row = lax.broadcasted_iota(jnp.int32, (sel.shape[1], span), 0)
    e_col = lax.broadcasted_iota(jnp.int32, (sel.shape[1], span), 1) >> 6
    picked = _dot(sel, jnp.where(e_row == (span // SLC_LEN) * j + e_col, 1.0, 0.0)) > 0.5
    kpos = j * span + lax.broadcasted_iota(jnp.int32, (nrow, span), 1)
    dist = pos0 - kpos
    keys = jnp.concatenate([r[...].astype(_MX) for r in ck_refs], axis=0)
    vals = jnp.concatenate([r[...].astype(_MX) for r in cv_refs], axis=0)
    s = _dot_nt(q, keys) - slope * dist.astype(F32)
    _online_update(s, picked & (dist >= 0), vals, m_ref, l_ref, acc_ref)

    @pl.when(j == n_pages // pps - 1)
    def _():
        qf = q.astype(F32)
        new = new_ref[...]
        s_new = jnp.sum(qf * new[0:1, :], axis=-1, keepdims=True)
        ok = sel[:, n_slc - 1:n_slc] > 0.5
        s_new = jnp.where(ok, s_new, NEG)
        m_old = m_ref[...]
        m_new = jnp.maximum(m_old, s_new)
        p_new = jnp.where(ok, jnp.exp(s_new - m_new), 0.0)
        alpha = jnp.exp(m_old - m_new)
        l_tot = alpha * l_ref[...] + p_new
        acc = alpha * acc_ref[...] + p_new * new[1:2, :]
        o_slc = acc / jnp.maximum(l_tot, 1e-30)

        idx = lax.broadcasted_iota(jnp.int32, (nrow, wbuf), 1)
        dw = wbuf - idx
        valid_w = (dw >= 0) & (dw < WINDOW)
        s_w = jnp.where(valid_w, _dot_nt(q, wk_ref[...]) - slope * dw.astype(F32), NEG)
        s_wn = jnp.sum(qf * new[2:3, :], axis=-1, keepdims=True)
        m_w = jnp.maximum(jnp.max(s_w, axis=-1, keepdims=True), s_wn)
        p_w = jnp.where(valid_w, jnp.exp(s_w - m_w), 0.0)
        p_wn = jnp.exp(s_wn - m_w)
        den = jnp.maximum(jnp.sum(p_w, axis=-1, keepdims=True) + p_wn, 1e-30)
        o_win = (_dot(p_w, wv_ref[...]) + p_wn * new[3:4, :]) / den

        gates = g_ref[...]
        ofull_ref[...] = gates[:, 0:1] * ocmp_ref[...] + gates[:, 1:2] * o_slc + gates[:, 2:3] * o_win
        row_g = lax.broadcasted_iota(jnp.int32, (nrow, HD), 0) >> 2
        o = jnp.zeros((nrow, HD), F32)
        for gg in range(KVH):
            o = o + jnp.where(row_g == gg, ofull_ref[:, gg * HD:(gg + 1) * HD], 0.0)
        o_ref[...] = o


def _nsa_sample(page_table, e, qbd, kc, vc, cache_k, cache_v, new_rows, win_k, win_v, gates16, slopes16, grp16,
                mimp, n_cmp, n_slc):
    nb, n_pages = page_table.shape
    nh = kc.shape[1]
    wbuf = win_k.shape[2]
    per_b = lambda *shape: pl.BlockSpec((None,) + shape, lambda b, j, pt: (b,) + (0,) * len(shape))
    fix = lambda *shape: pl.BlockSpec(shape, lambda b, j, pt: (0,) * len(shape))
    pps = _pages_per_step(n_pages)
    page = lambda p: pl.BlockSpec((None, None, PAGE, KV_DIM), lambda b, j, pt: (e, pt[b, j * pps + p], 0, 1))
    win = pl.BlockSpec((None, None, wbuf, KV_DIM), lambda b, j, pt: (e, b, 0, 0))
    nrow = qbd.shape[1]
    gs = pltpu.PrefetchScalarGridSpec(
        num_scalar_prefetch=1, grid=(nb, n_pages // pps),
        in_specs=[per_b(nrow, KV_DIM), per_b(nh, KV_DIM), per_b(nh, KV_DIM)] + [page(p) for p in range(pps)] * 2
                 + [per_b(8, KV_DIM), win, win, per_b(nrow, 128), fix(nrow, 128), fix(nrow, nrow), fix(*mimp.shape)],
        out_specs=per_b(nrow, HD),
        scratch_shapes=[pltpu.VMEM((nrow, 1), F32), pltpu.VMEM((nrow, 1), F32), pltpu.VMEM((nrow, KV_DIM), F32),
                        pltpu.VMEM((nrow, mimp.shape[1]), F32), pltpu.VMEM((nrow, KV_DIM), F32),
                        pltpu.VMEM((nrow, KV_DIM), F32)])
    return pl.pallas_call(
        functools.partial(_nsa_sample_kernel, n_pages=n_pages, pps=pps, n_cmp=n_cmp, n_slc=n_slc,
                          n_top=min(SLC_TOP, n_slc), wbuf=wbuf),
        grid_spec=gs, out_shape=jax.ShapeDtypeStruct((nb, nrow, HD), F32),
        compiler_params=_cp("parallel", "arbitrary"),
    )(page_table, qbd, kc, vc, *([cache_k] * pps), *([cache_v] * pps), new_rows, win_k, win_v, gates16, slopes16,
      grp16, mimp)


def _outproj_kernel(h_ref, y_ref, o_ref, wp_ref, wn_ref, out_ref):
    acc = h_ref[...] + _dot(y_ref[...], wp_ref[...])
    for g in range(KVH):
        acc = acc + _dot(o_ref[g], wn_ref[g])
    out_ref[...] = acc


def _outproj(h, y_pool, o4, wp, wn, tm):
    m, d = h.shape
    return pl.pallas_call(
        _outproj_kernel, grid=(m // tm,),
        in_specs=[pl.BlockSpec((tm, d), lambda i: (i, 0)), pl.BlockSpec((tm, POOL_DIM), lambda i: (i, 0)),
                  pl.BlockSpec((KVH, tm, 256), lambda i: (0, i, 0)), pl.BlockSpec((POOL_DIM, d), lambda i: (0, 0)),
                  pl.BlockSpec((KVH, 256, d), lambda i: (0, 0, 0))],
        out_specs=pl.BlockSpec((tm, d), lambda i: (i, 0)),
        out_shape=jax.ShapeDtypeStruct((m, d), F32), compiler_params=_cp("parallel"))(h, y_pool, o4, wp, wn)


def _rwkv_proj_kernel(*refs, has_vres):
    if has_vres:
        (xn_ref, xp_ref, mu_ref, wr_ref, wk_ref, wv_ref, w1_ref, w2_ref, a1_ref, a2_ref, g1_ref, g2_ref,
         w0_ref, a0_ref, kk_ref, ka_ref, vf_ref, v0_ref, v1_ref, v2_ref,
         r_out, w_out, k_out, v_out, kk_out, b_out, g_out) = refs
    else:
        (xn_ref, xp_ref, mu_ref, wr_ref, wk_ref, wv_ref, w1_ref, w2_ref, a1_ref, a2_ref, g1_ref, g2_ref,
         w0_ref, a0_ref, kk_ref, ka_ref,
         r_out, w_out, k_out, v_out, kk_out, b_out, g_out) = refs
    xn = xn_ref[...]
    xx = xp_ref[...] - xn
    mix = lambda j: xn + xx * mu_ref[j:j + 1, :]
    xr, xw, xk, xv, xa, xg = [mix(j) for j in range(6)]
    r_out[...] = _dot(xr, wr_ref[...])
    k = _dot(xk, wk_ref[...])
    v = _dot(xv, wv_ref[...])
    z = w0_ref[...] + _dot(jnp.tanh(_dot(xw, w1_ref[...])), w2_ref[...])
    w_log = -(jnp.maximum(-z, 0.0) + jnp.log(1.0 + jnp.exp(-jnp.abs(z)))) - 0.5
    w_out[...] = -jnp.exp(w_log)
    if has_vres:
        v = v + (vf_ref[...] - v) * _sigmoid(v0_ref[...] + _dot(_dot(xv, v1_ref[...]), v2_ref[...]))
    v_out[...] = v
    a = _sigmoid(a0_ref[...] + _dot(_dot(xa, a1_ref[...]), a2_ref[...]))
    g_out[...] = _dot(_sigmoid(_dot(xg, g1_ref[...])), g2_ref[...])
    kk = k * kk_ref[...]
    for c in range(kk.shape[1] // 128):
        sl = slice(c * 128, (c + 1) * 128)
        kt = kk[:, sl]
        kn = kt / jnp.maximum(jnp.sqrt(_seg64_sum(kt * kt)), 1e-12)
        kk_out[:, sl] = kn
        b_out[:, sl] = kn * a[:, sl]
    k_out[...] = k * (1.0 + (a - 1.0) * ka_ref[...])


def _rwkv_proj(xn, xprev, p, vres, tm):
    m, d = xn.shape
    row = pl.BlockSpec((tm, d), lambda i: (i, 0))
    fix = lambda a: pl.BlockSpec(a.shape, lambda i: (0,) * a.ndim)
    args = [xn, xprev, p['mu'], p['wr'], p['wk'], p['wv'], p['w1'], p['w2'], p['a1'], p['a2'], p['g1'], p['g2'],
            p['w0'], p['a0'], p['k_k'], p['k_a']]
    specs = [row, row] + [fix(a) for a in args[2:]]
    if vres is not None:
        vf, v0, v1, v2 = vres
        args += [vf, v0, v1, v2]
        specs += [row, fix(v0), fix(v1), fix(v2)]
    return pl.pallas_call(
        functools.partial(_rwkv_proj_kernel, has_vres=vres is not None), grid=(m // tm,),
        in_specs=specs, out_specs=[row] * 7, out_shape=[jax.ShapeDtypeStruct((m, d), F32)] * 7,
        compiler_params=_cp("parallel"))(*args)


_NN = (((1,), (0,)), ((), ()))
_NT = (((1,), (1,)), ((), ()))
_TN = (((0,), (0,)), ((), ()))


def _split(a):
    hi = a.astype(_MX)
    return hi, (a - hi.astype(F32)).astype(_MX)


def _dot3(a, b, dims=_NN):
    dg = lambda x, y: lax.dot_general(x, y, dims, preferred_element_type=F32)
    a_hi, a_lo = _split(a)
    b_hi, b_lo = _split(b)
    return dg(a_hi, b_hi) + dg(a_lo, b_hi) + dg(a_hi, b_lo)


def _wkv_kernel(r_ref, lw_ref, k_ref, v_ref, kk_ref, b_ref, s0_ref, o_ref, s_ref, *, hb):
    @pl.when(pl.program_id(2) == 0)
    def _():
        s_ref[...] = s0_ref[...]

    c = r_ref.shape[0]
    hd = lambda ref, h: ref[:, h * HD:(h + 1) * HD]
    row = lax.broadcasted_iota(jnp.int32, (c, c), 0)
    col = lax.broadcasted_iota(jnp.int32, (c, c), 1)
    incl = col <= row
    strict = col < row
    ltri = jnp.where(incl, 1.0, 0.0)
    levels = int(math.log2(c))
    heads = range(hb)
    lw = [hd(lw_ref, h) for h in heads]
    v = [hd(v_ref, h) for h in heads]
    cum = [_dot3(ltri, lw[h]) for h in heads]
    g_in = [jnp.exp(cum[h]) for h in heads]
    g_inv = [jnp.exp(-cum[h]) for h in heads]
    b_t = [hd(b_ref, h) * g_inv[h] for h in heads]
    k_t = [hd(k_ref, h) * g_inv[h] for h in heads]
    ar = [jnp.concatenate([-hd(kk_ref, h) * jnp.exp(cum[h] - lw[h]), hd(r_ref, h) * g_in[h]], axis=0) for h in heads]
    pb = [_dot3(ar[h], b_t[h], _NT) for h in heads]
    pk = [_dot3(ar[h], k_t[h], _NT) for h in heads]
    xs = [_dot3(ar[h], s_ref[h], _NT) for h in heads]
    n_mat = [jnp.where(strict, pb[h][:c], 0.0) for h in heads]
    x = [xs[h][:c] + _dot3(jnp.where(strict, pk[h][:c], 0.0), v[h]) for h in heads]
    for lvl in range(levels):
        x = [x[h] + _dot3(n_mat[h], x[h]) for h in heads]
        if lvl + 1 < levels:
            n_mat = [_dot3(n_mat[h], n_mat[h]) for h in heads]
    for h in heads:
        o_ref[:, h * HD:(h + 1) * HD] = (xs[h][c:] + _dot3(jnp.where(incl, pb[h][c:], 0.0), x[h])
                                         + _dot3(jnp.where(incl, pk[h][c:], 0.0), v[h]))
    for h in heads:
        s_ref[h] = (s_ref[h] + _dot3(x[h], b_t[h], _TN) + _dot3(v[h], k_t[h], _TN)) * g_in[h][c - 1:c, :]


def _wkv_scan(r, lw, k, v, kk, b, s0, tc, hb):
    n, t, d = r.shape
    nh_ = d // HD
    tok = pl.BlockSpec((None, tc, hb * HD), lambda bb, hh, tt: (bb, tt, hh))
    st = pl.BlockSpec((None, hb, HD, HD), lambda bb, hh, tt: (bb, hh, 0, 0))
    return pl.pallas_call(
        functools.partial(_wkv_kernel, hb=hb), grid=(n, nh_ // hb, t // tc),
        in_specs=[tok] * 6 + [st], out_specs=[tok, st],
        out_shape=[jax.ShapeDtypeStruct((n, t, d), F32), jax.ShapeDtypeStruct((n, nh_, HD, HD), F32)],
        compiler_params=_cp("parallel", "parallel", "arbitrary"))(r, lw, k, v, kk, b, s0)


def _wkv_step_kernel(r_ref, lw_ref, k_ref, kk_ref, b_ref, vc_ref, s0_ref, o_ref, s_ref):
    s = s0_ref[...]
    sa = -jnp.sum(s * kk_ref[...], axis=-1, keepdims=True)
    s = s * jnp.exp(lw_ref[...]) + sa * b_ref[...] + vc_ref[...] * k_ref[...]
    s_ref[...] = s
    o_ref[...] = jnp.sum(s * r_ref[...], axis=-1, keepdims=True)


def _wkv_step(r, w, k, kk, b, v_col, s0):
    n, nh_ = r.shape[:2]
    rowv = pl.BlockSpec((None, nh_, 1, HD), lambda i: (i, 0, 0, 0))
    colv = pl.BlockSpec((None, nh_, HD, 1), lambda i: (i, 0, 0, 0))
    st = pl.BlockSpec((None, nh_, HD, HD), lambda i: (i, 0, 0, 0))
    return pl.pallas_call(
        _wkv_step_kernel, grid=(n,), in_specs=[rowv] * 5 + [colv, st], out_specs=[colv, st],
        out_shape=[jax.ShapeDtypeStruct((n, nh_, HD, 1), F32), jax.ShapeDtypeStruct((n, nh_, HD, HD), F32)],
        compiler_params=_cp("parallel"))(r, w, k, kk, b, v_col, s0)


def _rwkv_out_kernel(h_ref, o_ref, r_ref, k_ref, v_ref, g_ref, lnw_ref, lnb_ref, rk_ref, wo_ref, out_ref, y_ref):
    for c in range(h_ref.shape[1] // 128):
        sl = slice(c * 128, (c + 1) * 128)
        o = o_ref[:, sl]
        dlt = o - _seg64_sum(o) * (1.0 / HD)
        var = _seg64_sum(dlt * dlt) * (1.0 / HD)
        on = dlt * lax.rsqrt(var + LN_X_EPS) * lnw_ref[:, sl] + lnb_ref[:, sl]
        bonus = _seg64_sum(r_ref[:, sl] * k_ref[:, sl] * rk_ref[:, sl])
        y_ref[:, sl] = ((on + bonus * v_ref[:, sl]) * g_ref[:, sl]).astype(y_ref.dtype)
    out_ref[...] = h_ref[...] + jnp.dot(y_ref[...], wo_ref[...], preferred_element_type=F32)


def _rwkv_out(h, o, r, k, v, g, lnw, lnb, rk, wo, tm):
    m, d = h.shape
    row = pl.BlockSpec((tm, d), lambda i: (i, 0))
    vec = pl.BlockSpec((1, d), lambda i: (0, 0))
    return pl.pallas_call(
        _rwkv_out_kernel, grid=(m // tm,),
        in_specs=[row] * 6 + [vec] * 3 + [pl.BlockSpec((d, d), lambda i: (0, 0))],
        out_specs=row, out_shape=jax.ShapeDtypeStruct((m, d), F32),
        scratch_shapes=[pltpu.VMEM((tm, d), _MX)],
        compiler_params=_cp("parallel"))(h, o, r, k, v, g, lnw, lnb, rk, wo)


def _row_tile(m, pref):
    return pref if m % pref == 0 else m


def _block_diag(w):
    g, a, b = w.shape
    eye = jnp.eye(g, dtype=w.dtype)
    return (eye[:, None, :, None] * w[:, :, None, :]).reshape(g * a, g * b)


def _imp_matrix(nh, n_cmp, n_slc, width):
    per = SLC_LEN // CMP_STRIDE
    lead = CMP_LEN // CMP_STRIDE - 1
    c = np.arange(nh)[:, None]
    j = np.arange(width)[None, :]
    m = (c - per * j >= -lead) & (c - per * j < per) & (c < n_cmp) & (j < n_slc)
    return jnp.asarray(m.astype(np.float32), dtype=_MX)


def _pad_lanes(n):
    return -(-n // 128) * 128


def _heads(x, n, t):
    return x.reshape(n, t, -1, HD).transpose(0, 2, 1, 3).astype(_MX)


def kernel(x_prompt, x_sample, cache_k, cache_v, page_table, state_win_k, state_win_v, state_pool, state_shift, state_wkv, p_prompt, p_sample, norm_mix, norm_mlp, norm_ple, mlp_w1, mlp_w2, ple_proj, ple_gate, even_w_in, even_w_out, pool_w, pool_scale, q_gain, k_gain, cmp_pe, cmp_w, rwkv_mu, rwkv_wr, rwkv_wk, rwkv_wv, rwkv_wo, rwkv_w0, rwkv_w1, rwkv_w2, rwkv_a0, rwkv_a1, rwkv_a2, rwkv_v0, rwkv_v1, rwkv_v2, rwkv_g1, rwkv_g2, rwkv_kk, rwkv_ka, rwkv_rk, rwkv_lnw, rwkv_lnb):
    nb, t, d = x_prompt.shape
    ns = x_sample.shape[0]
    depth = norm_mix.shape[0]
    n_even = even_w_in.shape[0]
    n_pages = page_table.shape[1]
    past_len = n_pages * PAGE
    wbuf = state_win_k.shape[2]
    n_phys = cache_k.shape[1]
    mp = nb * t
    tm_p = _row_tile(mp, 512)
    tm_a = _row_tile(mp, 256)
    slopes = jnp.asarray(_alibi_slopes(KVH * REP))
    mx = lambda a: a.astype(_MX)
    row1 = lambda a: a.reshape(1, -1)
    two = lambda a: jnp.tile(a.reshape(1, HD), (1, 2))

    rid = np.arange(16)
    real = (rid % 4) < REP
    slopes16 = jnp.asarray(np.where(real, _alibi_slopes(KVH * REP)[np.minimum((rid // 4) * REP + rid % 4, 11)],
                                    0.0).astype(np.float32))[:, None] * jnp.ones((1, 128), F32)
    grp16 = jnp.asarray(((rid[:, None] // 4 == rid[None, :] // 4) & real[None, :]).astype(np.float32))

    cache_k5 = cache_k.reshape(n_even, n_phys, PAGE, 2 * KV_DIM)
    cache_v5 = cache_v.reshape(n_even, n_phys, PAGE, 2 * KV_DIM)
    win_k4 = state_win_k.reshape(n_even, ns, wbuf, KV_DIM)
    win_v4 = state_win_v.reshape(n_even, ns, wbuf, KV_DIM)

    h_p = x_prompt.reshape(mp, d)
    h_s = x_sample.reshape(ns, d)
    outs = {k_: [] for k_ in ('nk_p', 'nv_p', 'nk_s', 'nv_s', 'wk_p', 'wv_p', 'wk_s', 'wv_s', 'pl_p', 'pl_s',
                              'sh_p', 'sh_s', 'st_p', 'st_s')}
    vf_p = vf_s = None

    for i in range(depth):
        gn = row1(norm_mix[i])
        if i % 2 == 0:
            e = i // 2
            w_in = mx(jnp.pad(even_w_in[e], ((0, 0), (0, IN_PAD - even_w_in.shape[2]))))
            wbd = mx(_block_diag(pool_w[e]))
            psc = row1(pool_scale[e])
            qg, ksg, kwg, kcg = two(q_gain[e]), two(k_gain[e, 1]), two(k_gain[e, 2]), two(k_gain[e, 0])
            w_out = even_w_out[e]
            wo_pool = mx(w_out[:POOL_DIM])
            wo_nsa = mx(jnp.pad(w_out[POOL_DIM:].reshape(KVH, REP * HD, d), ((0, 0), (0, 256 - REP * HD), (0, 0))))
            pe_k = jnp.tile(cmp_pe[e, 0], (1, 2))
            pe_v = jnp.tile(cmp_pe[e, 1], (1, 2))
            eye = jnp.eye(2, dtype=F32)
            bd = lambda w: mx((eye[None, :, None, :, None] * w[:, None, :, None, :]).reshape(CMP_LEN, 128, 128))
            cw_k, cw_v = bd(cmp_w[e, 0]), bd(cmp_w[e, 1])

            u, q, nk, nv, kw, vw, gate = _inproj(h_p, gn, w_in, qg, ksg, kwg, tm_a)
            y_pool = _pool_prompt(u.reshape(nb, t, POOL_DIM), wbd, psc, _row_tile(t, 512))
            nh_p = t // CMP_STRIDE
            pt_p = jnp.arange(nb * (t // PAGE), dtype=jnp.int32).reshape(nb, t // PAGE)
            kc, vc = _compress(nk.reshape(1, mp // PAGE, PAGE, 2 * KV_DIM), nv.reshape(1, mp // PAGE, PAGE, 2 * KV_DIM),
                               pt_p, 0, pe_k, pe_v, cw_k, cw_v, kcg)
            n_cmp_p = nh_p - (CMP_LEN // CMP_STRIDE - 1)
            n_slc_p = t // SLC_LEN
            gates_t = gate[:, :KVH * REP * 3].reshape(nb, t, KVH, REP * 3).transpose(0, 2, 3, 1)
            chunks_t = lambda x: (x.reshape(nb, t // KEY_TILE, KEY_TILE, KVH, HD).transpose(0, 3, 1, 4, 2).astype(_MX))
            o4t = _nsa_prompt(slopes, _heads(q, nb, t), _heads(kc, nb, nh_p),
                              vc.reshape(nb, nh_p, KVH, HD).transpose(0, 2, 3, 1).astype(_MX),
                              _heads(nk[:, KV_DIM:], nb, t), chunks_t(nv[:, KV_DIM:]),
                              _heads(kw, nb, t), chunks_t(vw), gates_t,
                              _imp_matrix(nh_p, n_cmp_p, n_slc_p, -(-n_slc_p // 8) * 8).T, n_cmp_p)
            o4 = o4t.transpose(0, 1, 3, 2).reshape(KVH, mp, 256)
            h_p = _outproj(h_p, y_pool.reshape(mp, POOL_DIM), o4, wo_pool, wo_nsa, tm_p)
            outs['nk_p'].append(nk.reshape(nb, t, 2, KVH, HD))
            outs['nv_p'].append(nv.reshape(nb, t, 2, KVH, HD))
            kw3 = kw.reshape(nb, t, KVH, HD)
            vw3 = vw.reshape(nb, t, KVH, HD)
            if t < wbuf:
                zpad = jnp.zeros((nb, wbuf - t, KVH, HD), F32)
                kw3, vw3 = jnp.concatenate([zpad, kw3], 1), jnp.concatenate([zpad, vw3], 1)
            outs['wk_p'].append(kw3[:, -wbuf:])
            outs['wv_p'].append(vw3[:, -wbuf:])
            outs['pl_p'].append(u.reshape(nb, t, POOL_DIM)[:, -POOL_BUF:])

            u, q, nk, nv, kw, vw, gate = _inproj(h_s, gn, w_in, qg, ksg, kwg, ns)
            y_pool = _pool_step(state_pool[e], u, wbd, psc, past_len)
            kc, vc = _compress(cache_k5, cache_v5, page_table, e, pe_k, pe_v, cw_k, cw_v, kcg)
            nh_s = kc.shape[1]
            n_cmp_s = nh_s - (CMP_LEN // CMP_STRIDE - 1)
            n_slc_s = -(-(past_len + 1) // SLC_LEN)
            q16 = jnp.pad(q.reshape(ns, KVH, REP, HD), ((0, 0), (0, 0), (0, 1), (0, 0))).reshape(ns, 16, 1, HD)
            gsel = jnp.asarray((np.arange(16)[:, None] // 4 == np.arange(KVH)[None, :]).astype(np.float32))
            qbd = (q16.astype(F32) * gsel[None, :, :, None]).reshape(ns, 16, KV_DIM).astype(_MX)
            new_rows = jnp.pad(jnp.stack([nk[:, KV_DIM:], nv[:, KV_DIM:], kw, vw], axis=1), ((0, 0), (0, 4), (0, 0)))
            g16 = jnp.pad(gate[:, :KVH * REP * 3].reshape(ns, KVH, REP, 3), ((0, 0), (0, 0), (0, 1), (0, 125)))
            o16 = _nsa_sample(page_table, e, qbd, mx(kc), mx(vc), cache_k5, cache_v5, new_rows, win_k4, win_v4,
                              g16.reshape(ns, 16, 128), slopes16, grp16,
                              _imp_matrix(nh_s, n_cmp_s, n_slc_s, _pad_lanes(n_slc_s)), n_cmp_s, n_slc_s)
            o4 = o16.reshape(ns, KVH, 4, HD)[:, :, :REP].reshape(ns, KVH, REP * HD).transpose(1, 0, 2)
            o4 = jnp.pad(o4, ((0, 0), (0, 0), (0, 256 - REP * HD)))
            h_s = _outproj(h_s, y_pool, o4, wo_pool, wo_nsa, ns)
            outs['nk_s'].append(nk.reshape(ns, 1, 2, KVH, HD))
            outs['nv_s'].append(nv.reshape(ns, 1, 2, KVH, HD))
            outs['wk_s'].append(jnp.concatenate([state_win_k[e], kw.reshape(ns, 1, KVH, HD)], axis=1)[:, -wbuf:])
            outs['wv_s'].append(jnp.concatenate([state_win_v[e], vw.reshape(ns, 1, KVH, HD)], axis=1)[:, -wbuf:])
            outs['pl_s'].append(jnp.concatenate([state_pool[e], u[:, None]], axis=1)[:, -POOL_BUF:])
        else:
            o = i // 2
            lora_in = lambda a: mx(jnp.pad(a, ((0, 0), (0, LORA_PAD - a.shape[1]))))
            lora_out = lambda a: mx(jnp.pad(a, ((0, LORA_PAD - a.shape[0]), (0, 0))))
            p = dict(mu=rwkv_mu[o], wr=mx(rwkv_wr[o]), wk=mx(rwkv_wk[o]), wv=mx(rwkv_wv[o]),
                     w1=lora_in(rwkv_w1[o]), w2=lora_out(rwkv_w2[o]), a1=lora_in(rwkv_a1[o]), a2=lora_out(rwkv_a2[o]),
                     g1=lora_in(rwkv_g1[o]), g2=lora_out(rwkv_g2[o]), w0=row1(rwkv_w0[o]), a0=row1(rwkv_a0[o]),
                     k_k=row1(rwkv_kk[o]), k_a=row1(rwkv_ka[o]))
            vparams = None if o == 0 else (row1(rwkv_v0[o - 1]), lora_in(rwkv_v1[o - 1]), lora_out(rwkv_v2[o - 1]))
            lnw, lnb, rk, wo = row1(rwkv_lnw[o]), row1(rwkv_lnb[o]), row1(rwkv_rk[o]), mx(rwkv_wo[o])
            nhd = d // HD

            xn = _norm(h_p, gn, tm_p)
            xn3 = xn.reshape(nb, t, d)
            xprev = jnp.concatenate([jnp.zeros((nb, 1, d), F32), xn3[:, :-1]], axis=1).reshape(mp, d)
            vres = None if o == 0 else (vf_p,) + vparams
            r, w, k, v, kk, b, g = _rwkv_proj(xn, xprev, p, vres, tm_a)
            if o == 0:
                vf_p = v
            seq = lambda a: a.reshape(nb, t, d)
            o_seq, s_fin = _wkv_scan(seq(r), seq(w), seq(k), seq(v), seq(kk), seq(b),
                                     jnp.zeros((nb, nhd, HD, HD), F32), WKV_CHUNK, 8)
            o_tok = o_seq.reshape(mp, d)
            h_p = _rwkv_out(h_p, o_tok, r, k, v, g, lnw, lnb, rk, wo, tm_a)
            outs['sh_p'].append(xn3[:, -1])
            outs['st_p'].append(s_fin)

            xn = _norm(h_s, gn, ns)
            vres = None if o == 0 else (vf_s,) + vparams
            r, w, k, v, kk, b, g = _rwkv_proj(xn, state_shift[o], p, vres, ns)
            if o == 0:
                vf_s = v
            rows = lambda a: a.reshape(ns, nhd, 1, HD)
            o_col, s_fin = _wkv_step(rows(r), rows(w), rows(k), rows(kk), rows(b), v.reshape(ns, nhd, HD, 1),
                                     state_wkv[o])
            h_s = _rwkv_out(h_s, o_col.reshape(ns, d), r, k, v, g, lnw, lnb, rk, wo, ns)
            outs['sh_s'].append(xn)
            outs['st_s'].append(s_fin)

        w1, w2 = mx(mlp_w1[i]), mx(mlp_w2[i])
        gm, gp, wg, wp = row1(norm_mlp[i]), row1(norm_ple[i]), mx(ple_gate[i]), mx(ple_proj[i])
        h_p = _mlp(h_p, gm, w1, w2, tm_p, 1024)
        h_p = _ple(h_p, p_prompt[i].reshape(mp, -1), gp, wg, wp, tm_p)
        h_s = _mlp(h_s, gm, w1, w2, ns, 1024)
        h_s = _ple(h_s, p_sample[i].reshape(ns, -1), gp, wg, wp, ns)

    st = lambda name: jnp.stack(outs[name])
    return (h_p.reshape(nb, t, d), h_s.reshape(ns, 1, d), st('nk_p'), st('nv_p'), st('nk_s'), st('nv_s'),
            st('wk_p'), st('wv_p'), st('wk_s'), st('wv_s'), st('pl_p'), st('pl_s'),
            st('sh_p'), st('sh_s'), st('st_p'), st('st_s'))
```

```python
import functools
import math

import numpy as np
import jax
import jax.numpy as jnp
from jax import lax
from jax.experimental import pallas as pl
from jax.experimental.pallas import tpu as pltpu

F32 = jnp.float32
_MX = jnp.bfloat16

HD = 64
POOL_DIM = 256
POOL_WINDOWS = (2, 4, 8, 16)
POOL_BUF = 15
KVH = 4
REP = 3
NSA_DIM = KVH * REP * HD
KV_DIM = KVH * HD
CMP_LEN, CMP_STRIDE = 32, 16
SLC_LEN, SLC_TOP = 64, 16
WINDOW = 512
QBLK = 256
KEY_TILE = 256
PAGE = 128
RMS_EPS = 1e-6
LN_X_EPS = 64e-5
NEG = -1e30
BIG = 1e9
WKV_CHUNK = 64
WKV_PASSES = (1, 3, 1)
LORA_PAD = 128
IN_PAD = 2688
VMEM_LIMIT = 56 * 1024 * 1024


def _cp(*sem):
    return pltpu.CompilerParams(dimension_semantics=sem, vmem_limit_bytes=VMEM_LIMIT)


def _alibi_slopes(n):
    p = 2 ** int(math.floor(math.log2(n)))
    s = [2.0 ** (-8.0 * (i + 1) / p) for i in range(p)]
    if p < n:
        s += [2.0 ** (-8.0 * (i + 1) / (2 * p)) for i in range(0, 2 * p, 2)][: n - p]
    return np.asarray(s, dtype=np.float32)


def _rms(x, g):
    return x * lax.rsqrt(jnp.mean(x * x, axis=-1, keepdims=True) + RMS_EPS) * g


def _sigmoid(x):
    return 1.0 / (1.0 + jnp.exp(-x))


def _dot(a, b):
    return jnp.dot(a.astype(_MX), b.astype(_MX), preferred_element_type=F32)


def _dot_nt(a, b):
    return lax.dot_general(a.astype(_MX), b.astype(_MX), (((1,), (1,)), ((), ())),
                           preferred_element_type=F32)


def _dot2(a, b):
    hi = a.astype(_MX)
    lo = (a - hi.astype(F32)).astype(_MX)
    b = b.astype(_MX)
    return (jnp.dot(hi, b, preferred_element_type=F32) + jnp.dot(lo, b, preferred_element_type=F32))


def _dot2r(a, b):
    hi = b.astype(_MX)
    lo = (b - hi.astype(F32)).astype(_MX)
    a = a.astype(_MX)
    return (jnp.dot(a, hi, preferred_element_type=F32) + jnp.dot(a, lo, preferred_element_type=F32))


def _seg64_sum(y):
    left = lax.broadcasted_iota(jnp.int32, y.shape, 1) < HD
    sa = jnp.sum(jnp.where(left, y, 0.0), axis=-1, keepdims=True)
    sb = jnp.sum(jnp.where(left, 0.0, y), axis=-1, keepdims=True)
    return jnp.where(left, sa, sb)


def _head_rms_tile(zt, gain2):
    ms = _seg64_sum(zt * zt) * (1.0 / HD)
    return zt * lax.rsqrt(ms + RMS_EPS) * gain2


def _masked_softmax(s, valid):
    s = jnp.where(valid, s, NEG)
    e = jnp.where(valid, jnp.exp(s - jnp.max(s, axis=-1, keepdims=True)), 0.0)
    return e / jnp.maximum(jnp.sum(e, axis=-1, keepdims=True), 1e-30)


def _online_update(s, valid, v, m_ref, l_ref, acc_ref):
    s = jnp.where(valid, s, NEG)
    m_old = m_ref[...]
    m_new = jnp.maximum(m_old, jnp.max(s, axis=-1, keepdims=True))
    p = jnp.where(valid, jnp.exp(s - m_new), 0.0)
    alpha = jnp.exp(m_old - m_new)
    l_ref[...] = alpha * l_ref[...] + jnp.sum(p, axis=-1, keepdims=True)
    acc_ref[...] = alpha * acc_ref[...] + _dot(p, v)
    m_ref[...] = m_new


def _topk_mask(score, n_cand, n_top):
    col_id = lax.broadcasted_iota(jnp.int32, score.shape, 1)
    rank = jnp.zeros(score.shape, F32)
    for j0 in range(n_cand):
        cj = score[:, j0:j0 + 1]
        beats = jnp.where(cj > score, 1.0, jnp.where((cj == score) & (col_id > j0), 1.0, 0.0))
        rank = rank + beats
    return jnp.where((rank < n_top) & (col_id < n_cand), 1.0, 0.0)


def _norm_kernel(x_ref, g_ref, o_ref):
    o_ref[...] = _rms(x_ref[...], g_ref[...])


def _norm(x, g, tm):
    m, d = x.shape
    return pl.pallas_call(
        _norm_kernel, grid=(m // tm,),
        in_specs=[pl.BlockSpec((tm, d), lambda i: (i, 0)), pl.BlockSpec((1, d), lambda i: (0, 0))],
        out_specs=pl.BlockSpec((tm, d), lambda i: (i, 0)),
        out_shape=jax.ShapeDtypeStruct((m, d), F32), compiler_params=_cp("parallel"))(x, g)


def _mlp_kernel(h_ref, g_ref, w1_ref, w2_ref, o_ref, xn_ref):
    @pl.when(pl.program_id(1) == 0)
    def _():
        x = h_ref[...]
        xn_ref[...] = _rms(x, g_ref[...]).astype(_MX)
        o_ref[...] = x

    a = jnp.dot(xn_ref[...], w1_ref[...], preferred_element_type=F32)
    a = jnp.square(jnp.maximum(a, 0.0))
    o_ref[...] += _dot(a, w2_ref[...])


def _mlp(h, g, w1, w2, tm, tf):
    m, d = h.shape
    dff = w1.shape[1]
    return pl.pallas_call(
        _mlp_kernel, grid=(m // tm, dff // tf),
        in_specs=[pl.BlockSpec((tm, d), lambda i, j: (i, 0)), pl.BlockSpec((1, d), lambda i, j: (0, 0)),
                  pl.BlockSpec((d, tf), lambda i, j: (0, j)), pl.BlockSpec((tf, d), lambda i, j: (j, 0))],
        out_specs=pl.BlockSpec((tm, d), lambda i, j: (i, 0)),
        out_shape=jax.ShapeDtypeStruct((m, d), F32),
        scratch_shapes=[pltpu.VMEM((tm, d), _MX)],
        compiler_params=_cp("parallel", "arbitrary"))(h, g, w1, w2)


def _ple_kernel(h_ref, p_ref, g_ref, wg_ref, wp_ref, o_ref):
    h = h_ref[...]
    gate = _sigmoid(_dot(_rms(h, g_ref[...]), wg_ref[...]))
    o_ref[...] = h + gate * _dot(p_ref[...], wp_ref[...])


def _ple(h, p, g, wg, wp, tm):
    m, d = h.shape
    pd = p.shape[1]
    return pl.pallas_call(
        _ple_kernel, grid=(m // tm,),
        in_specs=[pl.BlockSpec((tm, d), lambda i: (i, 0)), pl.BlockSpec((tm, pd), lambda i: (i, 0)),
                  pl.BlockSpec((1, d), lambda i: (0, 0)), pl.BlockSpec((d, d), lambda i: (0, 0)),
                  pl.BlockSpec((pd, d), lambda i: (0, 0))],
        out_specs=pl.BlockSpec((tm, d), lambda i: (i, 0)),
        out_shape=jax.ShapeDtypeStruct((m, d), F32), compiler_params=_cp("parallel"))(h, p, g, wg, wp)


_SEG_U = 0
_SEG_Q = POOL_DIM
_SEG_KV = POOL_DIM + NSA_DIM
_SEG_GL = _SEG_KV + 6 * KV_DIM


def _inproj_kernel(x_ref, gn_ref, w_ref, qg_ref, ksg_ref, kwg_ref,
                   u_ref, q_ref, nk_ref, nv_ref, kw_ref, vw_ref, gate_ref, z_ref):
    xn = _rms(x_ref[...], gn_ref[...])
    z_ref[...] = _dot(xn, w_ref[...])
    u_ref[...] = z_ref[:, _SEG_U:_SEG_U + POOL_DIM]
    for c in range(NSA_DIM // 128):
        zt = z_ref[:, _SEG_Q + c * 128:_SEG_Q + (c + 1) * 128]
        q_ref[:, c * 128:(c + 1) * 128] = (_head_rms_tile(zt, qg_ref[...]) * (HD ** -0.5)).astype(q_ref.dtype)
    kv = _SEG_KV
    nk_ref[:, 0:KV_DIM] = z_ref[:, kv:kv + KV_DIM]
    nv_ref[:, 0:KV_DIM] = z_ref[:, kv + KV_DIM:kv + 2 * KV_DIM]
    nv_ref[:, KV_DIM:2 * KV_DIM] = z_ref[:, kv + 3 * KV_DIM:kv + 4 * KV_DIM]
    vw_ref[...] = z_ref[:, kv + 5 * KV_DIM:kv + 6 * KV_DIM]
    for c in range(KV_DIM // 128):
        zs = z_ref[:, kv + 2 * KV_DIM + c * 128:kv + 2 * KV_DIM + (c + 1) * 128]
        nk_ref[:, KV_DIM + c * 128:KV_DIM + (c + 1) * 128] = _head_rms_tile(zs, ksg_ref[...])
        zw = z_ref[:, kv + 4 * KV_DIM + c * 128:kv + 4 * KV_DIM + (c + 1) * 128]
        kw_ref[:, c * 128:(c + 1) * 128] = _head_rms_tile(zw, kwg_ref[...])
    gate_ref[...] = _sigmoid(z_ref[:, _SEG_GL:_SEG_GL + 128])


def _inproj(x, gn, w, qg, ksg, kwg, tm):
    m, d = x.shape
    row = lambda i: (i, 0)
    fix = lambda i: (0, 0)
    outs = [(POOL_DIM, F32), (NSA_DIM, _MX), (2 * KV_DIM, F32), (2 * KV_DIM, F32), (KV_DIM, F32), (KV_DIM, F32),
            (128, F32)]
    return pl.pallas_call(
        _inproj_kernel, grid=(m // tm,),
        in_specs=[pl.BlockSpec((tm, d), row), pl.BlockSpec((1, d), fix), pl.BlockSpec((d, IN_PAD), fix),
                  pl.BlockSpec((1, 128), fix), pl.BlockSpec((1, 128), fix), pl.BlockSpec((1, 128), fix)],
        out_specs=[pl.BlockSpec((tm, c), row) for c, _ in outs],
        out_shape=[jax.ShapeDtypeStruct((m, c), dt) for c, dt in outs],
        scratch_shapes=[pltpu.VMEM((tm, IN_PAD), F32)],
        compiler_params=_cp("parallel"))(x, gn, w, qg, ksg, kwg)


def _pool_select(sums, u, cnt):
    grp = lax.broadcasted_iota(jnp.int32, u.shape, 1) >> 6
    ssel = jnp.where(grp == 0, sums[2], jnp.where(grp == 1, sums[4], jnp.where(grp == 2, sums[8], sums[16])))
    return ssel / cnt - u, grp


def _pool_kernel(u_ref, wbd_ref, sc_ref, y_ref, ext_ref, *, tm):
    i = pl.program_id(1)

    @pl.when(i == 0)
    def _():
        ext_ref[0:16, :] = jnp.zeros((16, POOL_DIM), F32)

    u = u_ref[...]
    ext_ref[16:16 + tm, :] = u
    acc = u
    sums = {}
    for s in range(1, 16):
        acc = acc + ext_ref[16 - s:16 - s + tm, :]
        if s + 1 in POOL_WINDOWS:
            sums[s + 1] = acc
    grp = lax.broadcasted_iota(jnp.int32, u.shape, 1) >> 6
    win = jnp.where(grp == 0, 2, jnp.where(grp == 1, 4, jnp.where(grp == 2, 8, 16)))
    pos = i * tm + lax.broadcasted_iota(jnp.int32, u.shape, 0)
    cnt = jnp.minimum(win, pos + 1).astype(F32)
    mixed, _ = _pool_select(sums, u, cnt)
    y_ref[...] = _dot(mixed, wbd_ref[...]) * sc_ref[...]
    ext_ref[0:16, :] = ext_ref[tm:tm + 16, :]


def _pool_prompt(u, wbd, scale, tm):
    n, t, c = u.shape
    return pl.pallas_call(
        functools.partial(_pool_kernel, tm=tm), grid=(n, t // tm),
        in_specs=[pl.BlockSpec((None, tm, c), lambda b, i: (b, i, 0)), pl.BlockSpec((c, c), lambda b, i: (0, 0)),
                  pl.BlockSpec((1, c), lambda b, i: (0, 0))],
        out_specs=pl.BlockSpec((None, tm, c), lambda b, i: (b, i, 0)),
        out_shape=jax.ShapeDtypeStruct((n, t, c), F32),
        scratch_shapes=[pltpu.VMEM((tm + 16, c), F32)],
        compiler_params=_cp("parallel", "arbitrary"))(u, wbd, scale)


def _pool_step_kernel(buf_ref, u_ref, wbd_ref, sc_ref, y_ref, *, pos0):
    u = u_ref[...]
    acc = u
    sums = {}
    for s in range(1, 16):
        acc = acc + buf_ref[:, POOL_BUF - s, :]
        if s + 1 in POOL_WINDOWS:
            sums[s + 1] = acc
    grp = lax.broadcasted_iota(jnp.int32, u.shape, 1) >> 6
    win = jnp.where(grp == 0, 2, jnp.where(grp == 1, 4, jnp.where(grp == 2, 8, 16)))
    cnt = jnp.minimum(win, pos0 + 1).astype(F32)
    mixed, _ = _pool_select(sums, u, cnt)
    y_ref[...] = _dot(mixed, wbd_ref[...]) * sc_ref[...]


def _pool_step(buf, u, wbd, scale, pos0):
    n, c = u.shape
    full = lambda *shape: pl.BlockSpec(shape, lambda i: (0,) * len(shape))
    return pl.pallas_call(
        functools.partial(_pool_step_kernel, pos0=pos0), grid=(1,),
        in_specs=[full(n, POOL_BUF, c), full(n, c), full(c, c), full(1, c)],
        out_specs=full(n, c), out_shape=jax.ShapeDtypeStruct((n, c), F32),
        compiler_params=_cp("arbitrary"))(buf, u, wbd, scale)


def _pages_per_step(n_pages):
    return next(p for p in (8, 4, 2, 1) if n_pages % p == 0)


def _compress_kernel(pt_ref, *refs, n_pages, pps):
    xk_refs, xv_refs = refs[:pps], refs[pps:2 * pps]
    pek_ref, pev_ref, wk_ref, wv_ref, kcg_ref, kc_ref, vc_ref, xs_k, xs_v, bsh = refs[2 * pps:]
    j = pl.program_id(1)
    n_tiles = KV_DIM // 128
    for p in range(pps):
        row0 = pl.multiple_of((j * pps + p) * PAGE, PAGE)
        for c in range(n_tiles):
            xs_k[c, pl.ds(row0, PAGE), :] = xk_refs[p][:, c * 128:(c + 1) * 128]
            xs_v[c, pl.ds(row0, PAGE), :] = xv_refs[p][:, c * 128:(c + 1) * 128]

    @pl.when(j == n_pages // pps - 1)
    def _():
        nh = n_pages * (PAGE // CMP_STRIDE)

        def run(xs, c, pe_ref, w_ref):
            a = jnp.zeros((nh, 128), F32)
            b = jnp.zeros((nh, 128), F32)
            for l in range(CMP_STRIDE):
                xl = xs[c, pl.ds(l, nh, stride=CMP_STRIDE), :]
                a = a + _dot(xl + pe_ref[l:l + 1, :], w_ref[l])
                b = b + _dot(xl + pe_ref[CMP_STRIDE + l:CMP_STRIDE + l + 1, :], w_ref[CMP_STRIDE + l])
            bsh[0:nh, :] = b
            bsh[nh:nh + 8, :] = jnp.zeros((8, 128), F32)
            return a + bsh[1:nh + 1, :]

        for c in range(n_tiles):
            kc_ref[:, c * 128:(c + 1) * 128] = _head_rms_tile(run(xs_k, c, pek_ref, wk_ref), kcg_ref[...])
            vc_ref[:, c * 128:(c + 1) * 128] = run(xs_v, c, pev_ref, wv_ref)


def _compress(cache_k, cache_v, page_table, e, pek, pev, wk, wv, kcg):
    nb, n_pages = page_table.shape
    nh = n_pages * (PAGE // CMP_STRIDE)
    pps = _pages_per_step(n_pages)
    page = lambda p: pl.BlockSpec((None, None, PAGE, KV_DIM), lambda b, j, pt: (e, pt[b, j * pps + p], 0, 0))
    fix2 = lambda b, j, pt: (0, 0)
    fix3 = lambda b, j, pt: (0, 0, 0)
    out = pl.BlockSpec((None, nh, KV_DIM), lambda b, j, pt: (b, 0, 0))
    gs = pltpu.PrefetchScalarGridSpec(
        num_scalar_prefetch=1, grid=(nb, n_pages // pps),
        in_specs=[page(p) for p in range(pps)] * 2
                 + [pl.BlockSpec((CMP_LEN, 128), fix2), pl.BlockSpec((CMP_LEN, 128), fix2),
                    pl.BlockSpec((CMP_LEN, 128, 128), fix3), pl.BlockSpec((CMP_LEN, 128, 128), fix3),
                    pl.BlockSpec((1, 128), fix2)],
        out_specs=[out, out],
        scratch_shapes=[pltpu.VMEM((KV_DIM // 128, n_pages * PAGE, 128), F32),
                        pltpu.VMEM((KV_DIM // 128, n_pages * PAGE, 128), F32),
                        pltpu.VMEM((nh + 8, 128), F32)])
    return pl.pallas_call(
        functools.partial(_compress_kernel, n_pages=n_pages, pps=pps), grid_spec=gs,
        out_shape=[jax.ShapeDtypeStruct((nb, nh, KV_DIM), F32)] * 2,
        compiler_params=_cp("parallel", "arbitrary"),
    )(page_table, *([cache_k] * pps), *([cache_v] * pps), pek, pev, wk, wv, kcg)


def _topk_rows(score, n_top):
    rid = lax.broadcasted_iota(jnp.int32, score.shape, 0).astype(F32)
    sel = jnp.zeros(score.shape, F32)
    for _ in range(n_top):
        m = jnp.max(score, axis=0, keepdims=True)
        first = jnp.min(jnp.where(score == m, rid, 1e9), axis=0, keepdims=True)
        hit = rid == first
        sel = jnp.where(hit, 1.0, sel)
        score = jnp.where(hit, -jnp.inf, score)
    return sel


def _nsa_prompt_kernel(sl_ref, q_ref, kc_ref, vct_ref, ks_ref, vst_ref, kw_ref, vwt_ref, gt_ref, mimpt_ref,
                       o_ref, sel_ref, s_ref, acc_ref, p_ref, bias_ref, mask_ref, *, n_cmp, n_slc, n_top):
    g = pl.program_id(1)
    i = pl.program_id(2)
    nh = kc_ref.shape[0]
    q = q_ref[...].reshape(REP * QBLK, HD)
    qpos0 = i * QBLK
    pos = qpos0 + lax.broadcasted_iota(jnp.int32, (1, QBLK), 1)
    slope = [sl_ref[g * REP + r] for r in range(REP)]

    cid = lax.broadcasted_iota(jnp.int32, (nh, QBLK), 0)
    dist_c = pos - (cid * CMP_STRIDE + (CMP_LEN - 1))
    valid_c = (dist_c >= 0) & (cid < n_cmp)
    dist_cf = dist_c.astype(F32)
    s_c = _dot_nt(kc_ref[...], q)
    vct = vct_ref[...]
    imp = jnp.zeros((nh, QBLK), F32)
    o_cmp = []
    for r in range(REP):
        s = jnp.where(valid_c, s_c[:, r * QBLK:(r + 1) * QBLK] - slope[r] * dist_cf, NEG)
        e = jnp.where(valid_c, jnp.exp(s - jnp.max(s, axis=0, keepdims=True)), 0.0)
        p = e / jnp.maximum(jnp.sum(e, axis=0, keepdims=True), 1e-30)
        imp = imp + p
        o_cmp.append(_dot(vct, p))
    imp_slc = _dot2r(mimpt_ref[...], imp)
    blk = lax.broadcasted_iota(jnp.int32, imp_slc.shape, 0)
    cur = pos >> 6
    forced = (blk == 0) | (blk == cur) | (blk == cur - 1)
    causal = blk * SLC_LEN <= pos
    score = jnp.where(forced, BIG, jnp.where(causal, imp_slc, -BIG))
    score = jnp.where(blk < n_slc, score, -jnp.inf)
    sel_ref[...] = _topk_rows(score, n_top)

    d0 = (lax.broadcasted_iota(jnp.int32, (KEY_TILE, QBLK), 1)
          - lax.broadcasted_iota(jnp.int32, (KEY_TILE, QBLK), 0))
    d0f = d0.astype(F32)
    for r in range(REP):
        bias_ref[r] = -slope[r] * d0f
    mask_ref[0] = jnp.where(d0 >= 0, 0.0, NEG)
    mask_ref[1] = jnp.where(d0 < 0, 0.0, NEG)
    hi = (qpos0 + QBLK - 1) // KEY_TILE

    def sweep(lo, k_ref, vt_ref, mask_fn):
        def scores(c):
            k0 = pl.multiple_of(c * KEY_TILE, KEY_TILE)
            return _dot_nt(k_ref[pl.ds(k0, KEY_TILE), :], q)

        def body(c, carry):
            m, l, p_prev = carry
            pv = _dot(vt_ref[jnp.maximum(c - 1, lo)], p_prev)
            s_next = scores(jnp.minimum(c + 1, hi))
            off = qpos0 - c * KEY_TILE
            madd = mask_fn(c, off)
            off_f = off.astype(F32)
            cur = c % 2
            s, shift = [], []
            for r in range(REP):
                s_r = s_ref[cur, :, r * QBLK:(r + 1) * QBLK] + bias_ref[r] + madd
                s.append(s_r)
                shift.append(jnp.max(s_r, axis=0, keepdims=True) - slope[r] * off_f)
            m_new = jnp.maximum(m, jnp.concatenate(shift, axis=1))
            p = jnp.concatenate([jnp.exp(s[r] - (m_new[:, r * QBLK:(r + 1) * QBLK] + slope[r] * off_f))
                                 for r in range(REP)], axis=1)
            alpha = jnp.exp(m - m_new)
            l = alpha * l + jnp.sum(p, axis=0, keepdims=True)
            acc_ref[...] = alpha * (acc_ref[...] + pv)
            s_ref[(c + 1) % 2] = s_next
            return m_new, l, p.astype(_MX)

        s_ref[lo % 2] = scores(lo)
        acc_ref[...] = jnp.zeros(acc_ref.shape, F32)
        p_ref[...] = jnp.zeros(p_ref.shape, p_ref.dtype)
        init = (jnp.full((1, REP * QBLK), NEG, F32), jnp.zeros((1, REP * QBLK), F32), p_ref[...])
        _, l, p_last = lax.fori_loop(lo, hi + 1, body, init)
        return (acc_ref[...] + _dot(vt_ref[hi], p_last)) / l

    def slc_mask(c, off):
        per_tile = KEY_TILE // SLC_LEN
        rows = [jnp.broadcast_to(sel_ref[pl.ds(per_tile * c + b, 1), :], (SLC_LEN, QBLK)) for b in range(per_tile)]
        picked = jnp.where(jnp.concatenate(rows, axis=0) > 0.5, 0.0, NEG)
        return picked + jnp.where(off == 0, mask_ref[0], 0.0)

    def win_mask(c, off):
        return jnp.where(off == 0, mask_ref[0], jnp.where(off == WINDOW, mask_ref[1], 0.0))

    o_slc = sweep(0, ks_ref, vst_ref, slc_mask)
    lo_win = jnp.maximum(qpos0 - WINDOW, 0) // KEY_TILE
    o_win = sweep(lo_win, kw_ref, vwt_ref, win_mask)

    gates = gt_ref[...]
    for r in range(REP):
        sl = slice(r * QBLK, (r + 1) * QBLK)
        o_ref[r * HD:(r + 1) * HD, :] = (gates[3 * r:3 * r + 1, :] * o_cmp[r] + gates[3 * r + 1:3 * r + 2, :] * o_slc[:, sl]
                                         + gates[3 * r + 2:3 * r + 3, :] * o_win[:, sl])
    o_ref[REP * HD:, :] = jnp.zeros((o_ref.shape[0] - REP * HD, QBLK), F32)


def _nsa_prompt(slopes, q_h, kc_h, vc_t, ks_h, vs_t, kw_h, vw_t, gates_t, mimp_t, n_cmp):
    n, _, t, _ = q_h.shape
    assert t % KEY_TILE == 0 and QBLK == KEY_TILE and WINDOW % KEY_TILE == 0
    nh = kc_h.shape[2]
    n_slc = t // SLC_LEN
    nq = t // QBLK
    nkt = t // KEY_TILE
    seq = lambda *shape: pl.BlockSpec((None, None) + shape, lambda b, g, i: (b, g) + (0,) * len(shape))
    return pl.pallas_call(
        functools.partial(_nsa_prompt_kernel, n_cmp=n_cmp, n_slc=n_slc, n_top=min(SLC_TOP, n_slc)),
        grid=(n, KVH, nq),
        in_specs=[pl.BlockSpec(memory_space=pltpu.SMEM),
                  pl.BlockSpec((None, REP, QBLK, HD), lambda b, g, i: (b, g, i, 0)),
                  seq(nh, HD), seq(HD, nh), seq(t, HD), seq(nkt, HD, KEY_TILE), seq(t, HD), seq(nkt, HD, KEY_TILE),
                  pl.BlockSpec((None, None, 9, QBLK), lambda b, g, i: (b, g, 0, i)),
                  pl.BlockSpec(mimp_t.shape, lambda b, g, i: (0, 0))],
        out_specs=pl.BlockSpec((None, None, 256, QBLK), lambda b, g, i: (g, b, 0, i)),
        out_shape=jax.ShapeDtypeStruct((KVH, n, 256, t), F32),
        scratch_shapes=[pltpu.VMEM((mimp_t.shape[0], QBLK), F32), pltpu.VMEM((2, KEY_TILE, REP * QBLK), F32),
                        pltpu.VMEM((HD, REP * QBLK), F32), pltpu.VMEM((KEY_TILE, REP * QBLK), _MX),
                        pltpu.VMEM((REP, KEY_TILE, QBLK), F32), pltpu.VMEM((2, KEY_TILE, QBLK), F32)],
        compiler_params=_cp("parallel", "parallel", "arbitrary"),
    )(slopes, q_h, kc_h, vc_t, ks_h, vs_t, kw_h, vw_t, gates_t, mimp_t)


def _nsa_sample_kernel(pt_ref, *refs, n_pages, pps, n_cmp, n_slc, n_top, wbuf):
    q_ref, kc_ref, vc_ref = refs[:3]
    ck_refs, cv_refs = refs[3:3 + pps], refs[3 + pps:3 + 2 * pps]
    (new_ref, wk_ref, wv_ref, g_ref, sl_ref, grp_ref, mimp_ref, o_ref, m_ref, l_ref, acc_ref, sel_ref, ocmp_ref,
     ofull_ref) = refs[3 + 2 * pps:]
    j = pl.program_id(1)
    span = pps * PAGE
    pos0 = n_pages * PAGE
    q = q_ref[...]
    slope = sl_ref[:, 0:1]
    nrow = q.shape[0]

    @pl.when(j == 0)
    def _():
        nh = kc_ref.shape[0]
        cid = lax.broadcasted_iota(jnp.int32, (nrow, nh), 1)
        dist = pos0 - (cid * CMP_STRIDE + (CMP_LEN - 1))
        valid = (dist >= 0) & (cid < n_cmp)
        p = _masked_softmax(_dot_nt(q, kc_ref[...]) - slope * dist.astype(F32), valid)
        ocmp_ref[...] = _dot(p, vc_ref[...])
        imp = _dot2(_dot2r(grp_ref[...], p), mimp_ref[...])
        blk = lax.broadcasted_iota(jnp.int32, imp.shape, 1)
        cur = pos0 // SLC_LEN
        forced = (blk == 0) | (blk == cur) | (blk == cur - 1)
        causal = blk * SLC_LEN <= pos0
        score = jnp.where(forced, BIG, jnp.where(causal, imp, -BIG))
        sel_ref[...] = _topk_mask(score, n_slc, n_top)
        m_ref[...] = jnp.full(m_ref.shape, NEG, F32)
        l_ref[...] = jnp.zeros(l_ref.shape, F32)
        acc_ref[...] = jnp.zeros(acc_ref.shape, F32)

    sel = sel_ref[...]
    e_row = lax.broadcasted_iota(jnp.int32, (sel.shape[1], span), 0)
    e_col = lax.broadcasted_iota(jnp.int32, (sel.shape[1], span), 1) >> 6
    picked = _dot(sel, jnp.where(e_row == (span // SLC_LEN) * j + e_col, 1.0, 0.0)) > 0.5
    kpos = j * span + lax.broadcasted_iota(jnp.int32, (nrow, span), 1)
    dist = pos0 - kpos
    keys = jnp.concatenate([r[...].astype(_MX) for r in ck_refs], axis=0)
    vals = jnp.concatenate([r[...].astype(_MX) for r in cv_refs], axis=0)
    s = _dot_nt(q, keys) - slope * dist.astype(F32)
    _online_update(s, picked & (dist >= 0), vals, m_ref, l_ref, acc_ref)

    @pl.when(j == n_pages // pps - 1)
    def _():
        qf = q.astype(F32)
        new = new_ref[...]
        s_new = jnp.sum(qf * new[0:1, :], axis=-1, keepdims=True)
        ok = sel[:, n_slc - 1:n_slc] > 0.5
        s_new = jnp.where(ok, s_new, NEG)
        m_old = m_ref[...]
        m_new = jnp.maximum(m_old, s_new)
        p_new = jnp.where(ok, jnp.exp(s_new - m_new), 0.0)
        alpha = jnp.exp(m_old - m_new)
        l_tot = alpha * l_ref[...] + p_new
        acc = alpha * acc_ref[...] + p_new * new[1:2, :]
        o_slc = acc / jnp.maximum(l_tot, 1e-30)

        idx = lax.broadcasted_iota(jnp.int32, (nrow, wbuf), 1)
        dw = wbuf - idx
        valid_w = (dw >= 0) & (dw < WINDOW)
        s_w = jnp.where(valid_w, _dot_nt(q, wk_ref[...]) - slope * dw.astype(F32), NEG)
        s_wn = jnp.sum(qf * new[2:3, :], axis=-1, keepdims=True)
        m_w = jnp.maximum(jnp.max(s_w, axis=-1, keepdims=True), s_wn)
        p_w = jnp.where(valid_w, jnp.exp(s_w - m_w), 0.0)
        p_wn = jnp.exp(s_wn - m_w)
        den = jnp.maximum(jnp.sum(p_w, axis=-1, keepdims=True) + p_wn, 1e-30)
        o_win = (_dot(p_w, wv_ref[...]) + p_wn * new[3:4, :]) / den

        gates = g_ref[...]
        ofull_ref[...] = gates[:, 0:1] * ocmp_ref[...] + gates[:, 1:2] * o_slc + gates[:, 2:3] * o_win
        row_g = lax.broadcasted_iota(jnp.int32, (nrow, HD), 0) >> 2
        o = jnp.zeros((nrow, HD), F32)
        for gg in range(KVH):
            o = o + jnp.where(row_g == gg, ofull_ref[:, gg * HD:(gg + 1) * HD], 0.0)
        o_ref[...] = o


def _nsa_sample(page_table, e, qbd, kc, vc, cache_k, cache_v, new_rows, win_k, win_v, gates16, slopes16, grp16,
                mimp, n_cmp, n_slc):
    nb, n_pages = page_table.shape
    nh = kc.shape[1]
    wbuf = win_k.shape[2]
    per_b = lambda *shape: pl.BlockSpec((None,) + shape, lambda b, j, pt: (b,) + (0,) * len(shape))
    fix = lambda *shape: pl.BlockSpec(shape, lambda b, j, pt: (0,) * len(shape))
    pps = _pages_per_step(n_pages)
    page = lambda p: pl.BlockSpec((None, None, PAGE, KV_DIM), lambda b, j, pt: (e, pt[b, j * pps + p], 0, 1))
    win = pl.BlockSpec((None, None, wbuf, KV_DIM), lambda b, j, pt: (e, b, 0, 0))
    nrow = qbd.shape[1]
    gs = pltpu.PrefetchScalarGridSpec(
        num_scalar_prefetch=1, grid=(nb, n_pages // pps),
        in_specs=[per_b(nrow, KV_DIM), per_b(nh, KV_DIM), per_b(nh, KV_DIM)] + [page(p) for p in range(pps)] * 2
                 + [per_b(8, KV_DIM), win, win, per_b(nrow, 128), fix(nrow, 128), fix(nrow, nrow), fix(*mimp.shape)],
        out_specs=per_b(nrow, HD),
        scratch_shapes=[pltpu.VMEM((nrow, 1), F32), pltpu.VMEM((nrow, 1), F32), pltpu.VMEM((nrow, KV_DIM), F32),
                        pltpu.VMEM((nrow, mimp.shape[1]), F32), pltpu.VMEM((nrow, KV_DIM), F32),
                        pltpu.VMEM((nrow, KV_DIM), F32)])
    return pl.pallas_call(
        functools.partial(_nsa_sample_kernel, n_pages=n_pages, pps=pps, n_cmp=n_cmp, n_slc=n_slc,
                          n_top=min(SLC_TOP, n_slc), wbuf=wbuf),
        grid_spec=gs, out_shape=jax.ShapeDtypeStruct((nb, nrow, HD), F32),
        compiler_params=_cp("parallel", "arbitrary"),
    )(page_table, qbd, kc, vc, *([cache_k] * pps), *([cache_v] * pps), new_rows, win_k, win_v, gates16, slopes16,
      grp16, mimp)


def _outproj_kernel(h_ref, y_ref, o_ref, wp_ref, wn_ref, out_ref):
    acc = h_ref[...] + _dot(y_ref[...], wp_ref[...])
    for g in range(KVH):
        acc = acc + _dot(o_ref[g], wn_ref[g])
    out_ref[...] = acc


def _outproj(h, y_pool, o4, wp, wn, tm):
    m, d = h.shape
    return pl.pallas_call(
        _outproj_kernel, grid=(m // tm,),
        in_specs=[pl.BlockSpec((tm, d), lambda i: (i, 0)), pl.BlockSpec((tm, POOL_DIM), lambda i: (i, 0)),
                  pl.BlockSpec((KVH, tm, 256), lambda i: (0, i, 0)), pl.BlockSpec((POOL_DIM, d), lambda i: (0, 0)),
                  pl.BlockSpec((KVH, 256, d), lambda i: (0, 0, 0))],
        out_specs=pl.BlockSpec((tm, d), lambda i: (i, 0)),
        out_shape=jax.ShapeDtypeStruct((m, d), F32), compiler_params=_cp("parallel"))(h, y_pool, o4, wp, wn)


def _rwkv_proj_kernel(*refs, has_vres):
    if has_vres:
        (xn_ref, xp_ref, mu_ref, wr_ref, wk_ref, wv_ref, w1_ref, w2_ref, a1_ref, a2_ref, g1_ref, g2_ref,
         w0_ref, a0_ref, kk_ref, ka_ref, vf_ref, v0_ref, v1_ref, v2_ref,
         r_out, w_out, k_out, v_out, kk_out, b_out, g_out) = refs
    else:
        (xn_ref, xp_ref, mu_ref, wr_ref, wk_ref, wv_ref, w1_ref, w2_ref, a1_ref, a2_ref, g1_ref, g2_ref,
         w0_ref, a0_ref, kk_ref, ka_ref,
         r_out, w_out, k_out, v_out, kk_out, b_out, g_out) = refs
    xn = xn_ref[...]
    xx = xp_ref[...] - xn
    mix = lambda j: xn + xx * mu_ref[j:j + 1, :]
    xr, xw, xk, xv, xa, xg = [mix(j) for j in range(6)]
    r_out[...] = _dot(xr, wr_ref[...])
    k = _dot(xk, wk_ref[...])
    v = _dot(xv, wv_ref[...])
    z = w0_ref[...] + _dot(jnp.tanh(_dot(xw, w1_ref[...])), w2_ref[...])
    w_log = -(jnp.maximum(-z, 0.0) + jnp.log(1.0 + jnp.exp(-jnp.abs(z)))) - 0.5
    w_out[...] = -jnp.exp(w_log)
    if has_vres:
        v = v + (vf_ref[...] - v) * _sigmoid(v0_ref[...] + _dot(_dot(xv, v1_ref[...]), v2_ref[...]))
    v_out[...] = v
    a = _sigmoid(a0_ref[...] + _dot(_dot(xa, a1_ref[...]), a2_ref[...]))
    g_out[...] = _dot(_sigmoid(_dot(xg, g1_ref[...])), g2_ref[...])
    kk = k * kk_ref[...]
    for c in range(kk.shape[1] // 128):
        sl = slice(c * 128, (c + 1) * 128)
        kt = kk[:, sl]
        kn = kt / jnp.maximum(jnp.sqrt(_seg64_sum(kt * kt)), 1e-12)
        kk_out[:, sl] = kn
        b_out[:, sl] = kn * a[:, sl]
    k_out[...] = k * (1.0 + (a - 1.0) * ka_ref[...])


def _rwkv_proj(xn, xprev, p, vres, tm):
    m, d = xn.shape
    row = pl.BlockSpec((tm, d), lambda i: (i, 0))
    fix = lambda a: pl.BlockSpec(a.shape, lambda i: (0,) * a.ndim)
    args = [xn, xprev, p['mu'], p['wr'], p['wk'], p['wv'], p['w1'], p['w2'], p['a1'], p['a2'], p['g1'], p['g2'],
            p['w0'], p['a0'], p['k_k'], p['k_a']]
    specs = [row, row] + [fix(a) for a in args[2:]]
    if vres is not None:
        vf, v0, v1, v2 = vres
        args += [vf, v0, v1, v2]
        specs += [row, fix(v0), fix(v1), fix(v2)]
    return pl.pallas_call(
        functools.partial(_rwkv_proj_kernel, has_vres=vres is not None), grid=(m // tm,),
        in_specs=specs, out_specs=[row] * 7, out_shape=[jax.ShapeDtypeStruct((m, d), F32)] * 7,
        compiler_params=_cp("parallel"))(*args)


_NN = (((1,), (0,)), ((), ()))
_NT = (((1,), (1,)), ((), ()))
_TN = (((0,), (0,)), ((), ()))


def _split(a):
    hi = a.astype(_MX)
    return hi, (a - hi.astype(F32)).astype(_MX)


def _dot3(a, b, dims=_NN, passes=3):
    dg = lambda x, y: lax.dot_general(x, y, dims, preferred_element_type=F32)
    a_hi, a_lo = _split(a)
    if passes == 1:
        return dg(a_hi, b.astype(_MX))
    b_hi, b_lo = _split(b)
    out = dg(a_hi, b_hi) + dg(a_hi, b_lo)
    return out if passes == 2 else out + dg(a_lo, b_hi)


def _wkv_kernel(r_ref, lw_ref, k_ref, v_ref, kk_ref, b_ref, s0_ref, o_ref, s_ref, *, hb):
    @pl.when(pl.program_id(2) == 0)
    def _():
        s_ref[...] = s0_ref[...]

    c = r_ref.shape[0]
    hd = lambda ref, h: ref[:, h * HD:(h + 1) * HD]
    row = lax.broadcasted_iota(jnp.int32, (c, c), 0)
    col = lax.broadcasted_iota(jnp.int32, (c, c), 1)
    incl = col <= row
    strict = col < row
    ltri = jnp.where(incl, 1.0, 0.0)
    levels = int(math.log2(c))
    heads = range(hb)
    lw = [hd(lw_ref, h) for h in heads]
    v = [hd(v_ref, h) for h in heads]
    p_in, p_solve, p_out = WKV_PASSES
    cum = [_dot3(ltri, lw[h], passes=2) for h in heads]
    g_in = [jnp.exp(cum[h]) for h in heads]
    g_inv = [jnp.exp(-cum[h]) for h in heads]
    b_t = [hd(b_ref, h) * g_inv[h] for h in heads]
    k_t = [hd(k_ref, h) * g_inv[h] for h in heads]
    ar = [jnp.concatenate([-hd(kk_ref, h) * jnp.exp(cum[h] - lw[h]), hd(r_ref, h) * g_in[h]], axis=0) for h in heads]
    pb = [_dot3(ar[h], b_t[h], _NT, p_in) for h in heads]
    pk = [_dot3(ar[h], k_t[h], _NT, p_in) for h in heads]
    xs = [_dot3(ar[h], s_ref[h], _NT, p_in) for h in heads]
    n_mat = [jnp.where(strict, pb[h][:c], 0.0) for h in heads]
    x = [xs[h][:c] + _dot3(jnp.where(strict, pk[h][:c], 0.0), v[h], _NN, p_in) for h in heads]
    for lvl in range(levels):
        x = [x[h] + _dot3(n_mat[h], x[h], _NN, p_solve) for h in heads]
        if lvl + 1 < levels:
            n_mat = [_dot3(n_mat[h], n_mat[h], _NN, p_solve) for h in heads]
    for h in heads:
        o_ref[:, h * HD:(h + 1) * HD] = (xs[h][c:] + _dot3(jnp.where(incl, pb[h][c:], 0.0), x[h], _NN, p_out)
                                         + _dot3(jnp.where(incl, pk[h][c:], 0.0), v[h], _NN, p_out))
    for h in heads:
        s_ref[h] = ((s_ref[h] + _dot3(x[h], b_t[h], _TN, p_out) + _dot3(v[h], k_t[h], _TN, p_out))
                    * g_in[h][c - 1:c, :])


def _wkv_scan(r, lw, k, v, kk, b, s0, tc, hb):
    n, t, d = r.shape
    nh_ = d // HD
    tok = pl.BlockSpec((None, tc, hb * HD), lambda bb, hh, tt: (bb, tt, hh))
    st = pl.BlockSpec((None, hb, HD, HD), lambda bb, hh, tt: (bb, hh, 0, 0))
    return pl.pallas_call(
        functools.partial(_wkv_kernel, hb=hb), grid=(n, nh_ // hb, t // tc),
        in_specs=[tok] * 6 + [st], out_specs=[tok, st],
        out_shape=[jax.ShapeDtypeStruct((n, t, d), F32), jax.ShapeDtypeStruct((n, nh_, HD, HD), F32)],
        compiler_params=_cp("parallel", "parallel", "arbitrary"))(r, lw, k, v, kk, b, s0)


def _wkv_step_kernel(r_ref, lw_ref, k_ref, kk_ref, b_ref, vc_ref, s0_ref, o_ref, s_ref):
    s = s0_ref[...]
    sa = -jnp.sum(s * kk_ref[...], axis=-1, keepdims=True)
    s = s * jnp.exp(lw_ref[...]) + sa * b_ref[...] + vc_ref[...] * k_ref[...]
    s_ref[...] = s
    o_ref[...] = jnp.sum(s * r_ref[...], axis=-1, keepdims=True)


def _wkv_step(r, w, k, kk, b, v_col, s0):
    n, nh_ = r.shape[:2]
    rowv = pl.BlockSpec((None, nh_, 1, HD), lambda i: (i, 0, 0, 0))
    colv = pl.BlockSpec((None, nh_, HD, 1), lambda i: (i, 0, 0, 0))
    st = pl.BlockSpec((None, nh_, HD, HD), lambda i: (i, 0, 0, 0))
    return pl.pallas_call(
        _wkv_step_kernel, grid=(n,), in_specs=[rowv] * 5 + [colv, st], out_specs=[colv, st],
        out_shape=[jax.ShapeDtypeStruct((n, nh_, HD, 1), F32), jax.ShapeDtypeStruct((n, nh_, HD, HD), F32)],
        compiler_params=_cp("parallel"))(r, w, k, kk, b, v_col, s0)


def _rwkv_out_kernel(h_ref, o_ref, r_ref, k_ref, v_ref, g_ref, lnw_ref, lnb_ref, rk_ref, wo_ref, out_ref, y_ref):
    for c in range(h_ref.shape[1] // 128):
        sl = slice(c * 128, (c + 1) * 128)
        o = o_ref[:, sl]
        dlt = o - _seg64_sum(o) * (1.0 / HD)
        var = _seg64_sum(dlt * dlt) * (1.0 / HD)
        on = dlt * lax.rsqrt(var + LN_X_EPS) * lnw_ref[:, sl] + lnb_ref[:, sl]
        bonus = _seg64_sum(r_ref[:, sl] * k_ref[:, sl] * rk_ref[:, sl])
        y_ref[:, sl] = ((on + bonus * v_ref[:, sl]) * g_ref[:, sl]).astype(y_ref.dtype)
    out_ref[...] = h_ref[...] + jnp.dot(y_ref[...], wo_ref[...], preferred_element_type=F32)


def _rwkv_out(h, o, r, k, v, g, lnw, lnb, rk, wo, tm):
    m, d = h.shape
    row = pl.BlockSpec((tm, d), lambda i: (i, 0))
    vec = pl.BlockSpec((1, d), lambda i: (0, 0))
    return pl.pallas_call(
        _rwkv_out_kernel, grid=(m // tm,),
        in_specs=[row] * 6 + [vec] * 3 + [pl.BlockSpec((d, d), lambda i: (0, 0))],
        out_specs=row, out_shape=jax.ShapeDtypeStruct((m, d), F32),
        scratch_shapes=[pltpu.VMEM((tm, d), _MX)],
        compiler_params=_cp("parallel"))(h, o, r, k, v, g, lnw, lnb, rk, wo)


def _row_tile(m, pref):
    return pref if m % pref == 0 else m


def _block_diag(w):
    g, a, b = w.shape
    eye = jnp.eye(g, dtype=w.dtype)
    return (eye[:, None, :, None] * w[:, :, None, :]).reshape(g * a, g * b)


def _imp_matrix(nh, n_cmp, n_slc, width):
    per = SLC_LEN // CMP_STRIDE
    lead = CMP_LEN // CMP_STRIDE - 1
    c = np.arange(nh)[:, None]
    j = np.arange(width)[None, :]
    m = (c - per * j >= -lead) & (c - per * j < per) & (c < n_cmp) & (j < n_slc)
    return jnp.asarray(m.astype(np.float32), dtype=_MX)


def _pad_lanes(n):
    return -(-n // 128) * 128


def _heads(x, n, t):
    return x.reshape(n, t, -1, HD).transpose(0, 2, 1, 3).astype(_MX)


def kernel(x_prompt, x_sample, cache_k, cache_v, page_table, state_win_k, state_win_v, state_pool, state_shift, state_wkv, p_prompt, p_sample, norm_mix, norm_mlp, norm_ple, mlp_w1, mlp_w2, ple_proj, ple_gate, even_w_in, even_w_out, pool_w, pool_scale, q_gain, k_gain, cmp_pe, cmp_w, rwkv_mu, rwkv_wr, rwkv_wk, rwkv_wv, rwkv_wo, rwkv_w0, rwkv_w1, rwkv_w2, rwkv_a0, rwkv_a1, rwkv_a2, rwkv_v0, rwkv_v1, rwkv_v2, rwkv_g1, rwkv_g2, rwkv_kk, rwkv_ka, rwkv_rk, rwkv_lnw, rwkv_lnb):
    nb, t, d = x_prompt.shape
    ns = x_sample.shape[0]
    depth = norm_mix.shape[0]
    n_even = even_w_in.shape[0]
    n_pages = page_table.shape[1]
    past_len = n_pages * PAGE
    wbuf = state_win_k.shape[2]
    n_phys = cache_k.shape[1]
    mp = nb * t
    tm_p = _row_tile(mp, 512)
    tm_a = _row_tile(mp, 256)
    slopes = jnp.asarray(_alibi_slopes(KVH * REP))
    mx = lambda a: a.astype(_MX)
    row1 = lambda a: a.reshape(1, -1)
    two = lambda a: jnp.tile(a.reshape(1, HD), (1, 2))

    rid = np.arange(16)
    real = (rid % 4) < REP
    slopes16 = jnp.asarray(np.where(real, _alibi_slopes(KVH * REP)[np.minimum((rid // 4) * REP + rid % 4, 11)],
                                    0.0).astype(np.float32))[:, None] * jnp.ones((1, 128), F32)
    grp16 = jnp.asarray(((rid[:, None] // 4 == rid[None, :] // 4) & real[None, :]).astype(np.float32))

    cache_k5 = cache_k.reshape(n_even, n_phys, PAGE, 2 * KV_DIM)
    cache_v5 = cache_v.reshape(n_even, n_phys, PAGE, 2 * KV_DIM)
    win_k4 = state_win_k.reshape(n_even, ns, wbuf, KV_DIM)
    win_v4 = state_win_v.reshape(n_even, ns, wbuf, KV_DIM)

    h_p = x_prompt.reshape(mp, d)
    h_s = x_sample.reshape(ns, d)
    outs = {k_: [] for k_ in ('nk_p', 'nv_p', 'nk_s', 'nv_s', 'wk_p', 'wv_p', 'wk_s', 'wv_s', 'pl_p', 'pl_s',
                              'sh_p', 'sh_s', 'st_p', 'st_s')}
    vf_p = vf_s = None

    for i in range(depth):
        gn = row1(norm_mix[i])
        if i % 2 == 0:
            e = i // 2
            w_in = mx(jnp.pad(even_w_in[e], ((0, 0), (0, IN_PAD - even_w_in.shape[2]))))
            wbd = mx(_block_diag(pool_w[e]))
            psc = row1(pool_scale[e])
            qg, ksg, kwg, kcg = two(q_gain[e]), two(k_gain[e, 1]), two(k_gain[e, 2]), two(k_gain[e, 0])
            w_out = even_w_out[e]
            wo_pool = mx(w_out[:POOL_DIM])
            wo_nsa = mx(jnp.pad(w_out[POOL_DIM:].reshape(KVH, REP * HD, d), ((0, 0), (0, 256 - REP * HD), (0, 0))))
            pe_k = jnp.tile(cmp_pe[e, 0], (1, 2))
            pe_v = jnp.tile(cmp_pe[e, 1], (1, 2))
            eye = jnp.eye(2, dtype=F32)
            bd = lambda w: mx((eye[None, :, None, :, None] * w[:, None, :, None, :]).reshape(CMP_LEN, 128, 128))
            cw_k, cw_v = bd(cmp_w[e, 0]), bd(cmp_w[e, 1])

            u, q, nk, nv, kw, vw, gate = _inproj(h_p, gn, w_in, qg, ksg, kwg, tm_a)
            y_pool = _pool_prompt(u.reshape(nb, t, POOL_DIM), wbd, psc, _row_tile(t, 512))
            nh_p = t // CMP_STRIDE
            pt_p = jnp.arange(nb * (t // PAGE), dtype=jnp.int32).reshape(nb, t // PAGE)
            kc, vc = _compress(nk.reshape(1, mp // PAGE, PAGE, 2 * KV_DIM), nv.reshape(1, mp // PAGE, PAGE, 2 * KV_DIM),
                               pt_p, 0, pe_k, pe_v, cw_k, cw_v, kcg)
            n_cmp_p = nh_p - (CMP_LEN // CMP_STRIDE - 1)
            n_slc_p = t // SLC_LEN
            gates_t = gate[:, :KVH * REP * 3].reshape(nb, t, KVH, REP * 3).transpose(0, 2, 3, 1)
            chunks_t = lambda x: (x.reshape(nb, t // KEY_TILE, KEY_TILE, KVH, HD).transpose(0, 3, 1, 4, 2).astype(_MX))
            o4t = _nsa_prompt(slopes, _heads(q, nb, t), _heads(kc, nb, nh_p),
                              vc.reshape(nb, nh_p, KVH, HD).transpose(0, 2, 3, 1).astype(_MX),
                              _heads(nk[:, KV_DIM:], nb, t), chunks_t(nv[:, KV_DIM:]),
                              _heads(kw, nb, t), chunks_t(vw), gates_t,
                              _imp_matrix(nh_p, n_cmp_p, n_slc_p, -(-n_slc_p // 8) * 8).T, n_cmp_p)
            o4 = o4t.transpose(0, 1, 3, 2).reshape(KVH, mp, 256)
            h_p = _outproj(h_p, y_pool.reshape(mp, POOL_DIM), o4, wo_pool, wo_nsa, tm_p)
            outs['nk_p'].append(nk.reshape(nb, t, 2, KVH, HD))
            outs['nv_p'].append(nv.reshape(nb, t, 2, KVH, HD))
            kw3 = kw.reshape(nb, t, KVH, HD)
            vw3 = vw.reshape(nb, t, KVH, HD)
            if t < wbuf:
                zpad = jnp.zeros((nb, wbuf - t, KVH, HD), F32)
                kw3, vw3 = jnp.concatenate([zpad, kw3], 1), jnp.concatenate([zpad, vw3], 1)
            outs['wk_p'].append(kw3[:, -wbuf:])
            outs['wv_p'].append(vw3[:, -wbuf:])
            outs['pl_p'].append(u.reshape(nb, t, POOL_DIM)[:, -POOL_BUF:])

            u, q, nk, nv, kw, vw, gate = _inproj(h_s, gn, w_in, qg, ksg, kwg, ns)
            y_pool = _pool_step(state_pool[e], u, wbd, psc, past_len)
            kc, vc = _compress(cache_k5, cache_v5, page_table, e, pe_k, pe_v, cw_k, cw_v, kcg)
            nh_s = kc.shape[1]
            n_cmp_s = nh_s - (CMP_LEN // CMP_STRIDE - 1)
            n_slc_s = -(-(past_len + 1) // SLC_LEN)
            q16 = jnp.pad(q.reshape(ns, KVH, REP, HD), ((0, 0), (0, 0), (0, 1), (0, 0))).reshape(ns, 16, 1, HD)
            gsel = jnp.asarray((np.arange(16)[:, None] // 4 == np.arange(KVH)[None, :]).astype(np.float32))
            qbd = (q16.astype(F32) * gsel[None, :, :, None]).reshape(ns, 16, KV_DIM).astype(_MX)
            new_rows = jnp.pad(jnp.stack([nk[:, KV_DIM:], nv[:, KV_DIM:], kw, vw], axis=1), ((0, 0), (0, 4), (0, 0)))
            g16 = jnp.pad(gate[:, :KVH * REP * 3].reshape(ns, KVH, REP, 3), ((0, 0), (0, 0), (0, 1), (0, 125)))
            o16 = _nsa_sample(page_table, e, qbd, mx(kc), mx(vc), cache_k5, cache_v5, new_rows, win_k4, win_v4,
                              g16.reshape(ns, 16, 128), slopes16, grp16,
                              _imp_matrix(nh_s, n_cmp_s, n_slc_s, _pad_lanes(n_slc_s)), n_cmp_s, n_slc_s)
            o4 = o16.reshape(ns, KVH, 4, HD)[:, :, :REP].reshape(ns, KVH, REP * HD).transpose(1, 0, 2)
            o4 = jnp.pad(o4, ((0, 0), (0, 0), (0, 256 - REP * HD)))
            h_s = _outproj(h_s, y_pool, o4, wo_pool, wo_nsa, ns)
            outs['nk_s'].append(nk.reshape(ns, 1, 2, KVH, HD))
            outs['nv_s'].append(nv.reshape(ns, 1, 2, KVH, HD))
            outs['wk_s'].append(jnp.concatenate([state_win_k[e], kw.reshape(ns, 1, KVH, HD)], axis=1)[:, -wbuf:])
            outs['wv_s'].append(jnp.concatenate([state_win_v[e], vw.reshape(ns, 1, KVH, HD)], axis=1)[:, -wbuf:])
            outs['pl_s'].append(jnp.concatenate([state_pool[e], u[:, None]], axis=1)[:, -POOL_BUF:])
        else:
            o = i // 2
            lora_in = lambda a: mx(jnp.pad(a, ((0, 0), (0, LORA_PAD - a.shape[1]))))
            lora_out = lambda a: mx(jnp.pad(a, ((0, LORA_PAD - a.shape[0]), (0, 0))))
            p = dict(mu=rwkv_mu[o], wr=mx(rwkv_wr[o]), wk=mx(rwkv_wk[o]), wv=mx(rwkv_wv[o]),
                     w1=lora_in(rwkv_w1[o]), w2=lora_out(rwkv_w2[o]), a1=lora_in(rwkv_a1[o]), a2=lora_out(rwkv_a2[o]),
                     g1=lora_in(rwkv_g1[o]), g2=lora_out(rwkv_g2[o]), w0=row1(rwkv_w0[o]), a0=row1(rwkv_a0[o]),
                     k_k=row1(rwkv_kk[o]), k_a=row1(rwkv_ka[o]))
            vparams = None if o == 0 else (row1(rwkv_v0[o - 1]), lora_in(rwkv_v1[o - 1]), lora_out(rwkv_v2[o - 1]))
            lnw, lnb, rk, wo = row1(rwkv_lnw[o]), row1(rwkv_lnb[o]), row1(rwkv_rk[o]), mx(rwkv_wo[o])
            nhd = d // HD

            xn = _norm(h_p, gn, tm_p)
            xn3 = xn.reshape(nb, t, d)
            xprev = jnp.concatenate([jnp.zeros((nb, 1, d), F32), xn3[:, :-1]], axis=1).reshape(mp, d)
            vres = None if o == 0 else (vf_p,) + vparams
            r, w, k, v, kk, b, g = _rwkv_proj(xn, xprev, p, vres, tm_a)
            if o == 0:
                vf_p = v
            seq = lambda a: a.reshape(nb, t, d)
            o_seq, s_fin = _wkv_scan(seq(r), seq(w), seq(k), seq(v), seq(kk), seq(b),
                                     jnp.zeros((nb, nhd, HD, HD), F32), WKV_CHUNK, 8)
            o_tok = o_seq.reshape(mp, d)
            h_p = _rwkv_out(h_p, o_tok, r, k, v, g, lnw, lnb, rk, wo, tm_a)
            outs['sh_p'].append(xn3[:, -1])
            outs['st_p'].append(s_fin)

            xn = _norm(h_s, gn, ns)
            vres = None if o == 0 else (vf_s,) + vparams
            r, w, k, v, kk, b, g = _rwkv_proj(xn, state_shift[o], p, vres, ns)
            if o == 0:
                vf_s = v
            rows = lambda a: a.reshape(ns, nhd, 1, HD)
            o_col, s_fin = _wkv_step(rows(r), rows(w), rows(k), rows(kk), rows(b), v.reshape(ns, nhd, HD, 1),
                                     state_wkv[o])
            h_s = _rwkv_out(h_s, o_col.reshape(ns, d), r, k, v, g, lnw, lnb, rk, wo, ns)
            outs['sh_s'].append(xn)
            outs['st_s'].append(s_fin)

        w1, w2 = mx(mlp_w1[i]), mx(mlp_w2[i])
        gm, gp, wg, wp = row1(norm_mlp[i]), row1(norm_ple[i]), mx(ple_gate[i]), mx(ple_proj[i])
        h_p = _mlp(h_p, gm, w1, w2, tm_p, 1024)
        h_p = _ple(h_p, p_prompt[i].reshape(mp, -1), gp, wg, wp, tm_p)
        h_s = _mlp(h_s, gm, w1, w2, ns, 1024)
        h_s = _ple(h_s, p_sample[i].reshape(ns, -1), gp, wg, wp, ns)

    st = lambda name: jnp.stack(outs[name])
    return (h_p.reshape(nb, t, d), h_s.reshape(ns, 1, d), st('nk_p'), st('nv_p'), st('nk_s'), st('nv_s'),
            st('wk_p'), st('wv_p'), st('wk_s'), st('wv_s'), st('pl_p'), st('pl_s'),
            st('sh_p'), st('sh_s'), st('st_p'), st('st_s'))
```

```python
import functools
import math

import numpy as np
import jax
import jax.numpy as jnp
from jax import lax
from jax.experimental import pallas as pl
from jax.experimental.pallas import tpu as pltpu

F32 = jnp.float32
_MX = jnp.bfloat16

HD = 64
POOL_DIM = 256
POOL_WINDOWS = (2, 4, 8, 16)
POOL_BUF = 15
KVH = 4
REP = 3
NSA_DIM = KVH * REP * HD
KV_DIM = KVH * HD
CMP_LEN, CMP_STRIDE = 32, 16
SLC_LEN, SLC_TOP = 64, 16
WINDOW = 512
QBLK = 256
KEY_TILE = 256
PAGE = 128
RMS_EPS = 1e-6
LN_X_EPS = 64e-5
NEG = -1e30
LOG2E = 1.4426950408889634
BIG = 1e9
WKV_CHUNK = 64
WKV_PASSES = (1, 2, 1)
WKV_HEADS = 16
LORA_PAD = 128
IN_PAD = 2688
VMEM_LIMIT = 56 * 1024 * 1024


def _cp(*sem):
    return pltpu.CompilerParams(dimension_semantics=sem, vmem_limit_bytes=VMEM_LIMIT)


def _alibi_slopes(n):
    p = 2 ** int(math.floor(math.log2(n)))
    s = [2.0 ** (-8.0 * (i + 1) / p) for i in range(p)]
    if p < n:
        s += [2.0 ** (-8.0 * (i + 1) / (2 * p)) for i in range(0, 2 * p, 2)][: n - p]
    return np.asarray(s, dtype=np.float32)


def _rms(x, g):
    return x * lax.rsqrt(jnp.mean(x * x, axis=-1, keepdims=True) + RMS_EPS) * g


def _sigmoid(x):
    return 1.0 / (1.0 + jnp.exp(-x))


def _dot(a, b):
    return jnp.dot(a.astype(_MX), b.astype(_MX), preferred_element_type=F32)


def _dot_nt(a, b):
    return lax.dot_general(a.astype(_MX), b.astype(_MX), (((1,), (1,)), ((), ())),
                           preferred_element_type=F32)


def _dot2(a, b):
    hi = a.astype(_MX)
    lo = (a - hi.astype(F32)).astype(_MX)
    b = b.astype(_MX)
    return (jnp.dot(hi, b, preferred_element_type=F32) + jnp.dot(lo, b, preferred_element_type=F32))


def _dot2r(a, b):
    hi = b.astype(_MX)
    lo = (b - hi.astype(F32)).astype(_MX)
    a = a.astype(_MX)
    return (jnp.dot(a, hi, preferred_element_type=F32) + jnp.dot(a, lo, preferred_element_type=F32))


def _seg64_sum(y):
    left = lax.broadcasted_iota(jnp.int32, y.shape, 1) < HD
    sa = jnp.sum(jnp.where(left, y, 0.0), axis=-1, keepdims=True)
    sb = jnp.sum(jnp.where(left, 0.0, y), axis=-1, keepdims=True)
    return jnp.where(left, sa, sb)


def _head_rms_tile(zt, gain2):
    ms = _seg64_sum(zt * zt) * (1.0 / HD)
    return zt * lax.rsqrt(ms + RMS_EPS) * gain2


def _masked_softmax(s, valid):
    s = jnp.where(valid, s, NEG)
    e = jnp.where(valid, jnp.exp(s - jnp.max(s, axis=-1, keepdims=True)), 0.0)
    return e / jnp.maximum(jnp.sum(e, axis=-1, keepdims=True), 1e-30)


def _online_update(s, valid, v, m_ref, l_ref, acc_ref):
    s = jnp.where(valid, s, NEG)
    m_old = m_ref[...]
    m_new = jnp.maximum(m_old, jnp.max(s, axis=-1, keepdims=True))
    p = jnp.where(valid, jnp.exp(s - m_new), 0.0)
    alpha = jnp.exp(m_old - m_new)
    l_ref[...] = alpha * l_ref[...] + jnp.sum(p, axis=-1, keepdims=True)
    acc_ref[...] = alpha * acc_ref[...] + _dot(p, v)
    m_ref[...] = m_new


def _topk_mask(score, n_cand, n_top):
    col_id = lax.broadcasted_iota(jnp.int32, score.shape, 1)
    rank = jnp.zeros(score.shape, F32)
    for j0 in range(n_cand):
        cj = score[:, j0:j0 + 1]
        beats = jnp.where(cj > score, 1.0, jnp.where((cj == score) & (col_id > j0), 1.0, 0.0))
        rank = rank + beats
    return jnp.where((rank < n_top) & (col_id < n_cand), 1.0, 0.0)


def _norm_kernel(x_ref, g_ref, o_ref):
    o_ref[...] = _rms(x_ref[...], g_ref[...])


def _norm(x, g, tm):
    m, d = x.shape
    return pl.pallas_call(
        _norm_kernel, grid=(m // tm,),
        in_specs=[pl.BlockSpec((tm, d), lambda i: (i, 0)), pl.BlockSpec((1, d), lambda i: (0, 0))],
        out_specs=pl.BlockSpec((tm, d), lambda i: (i, 0)),
        out_shape=jax.ShapeDtypeStruct((m, d), F32), compiler_params=_cp("parallel"))(x, g)


def _mlp_kernel(h_ref, g_ref, w1_ref, w2_ref, o_ref, xn_ref):
    @pl.when(pl.program_id(1) == 0)
    def _():
        x = h_ref[...]
        xn_ref[...] = _rms(x, g_ref[...]).astype(_MX)
        o_ref[...] = x

    a = jnp.dot(xn_ref[...], w1_ref[...], preferred_element_type=F32)
    a = jnp.square(jnp.maximum(a, 0.0))
    o_ref[...] += _dot(a, w2_ref[...])


def _mlp(h, g, w1, w2, tm, tf):
    m, d = h.shape
    dff = w1.shape[1]
    return pl.pallas_call(
        _mlp_kernel, grid=(m // tm, dff // tf),
        in_specs=[pl.BlockSpec((tm, d), lambda i, j: (i, 0)), pl.BlockSpec((1, d), lambda i, j: (0, 0)),
                  pl.BlockSpec((d, tf), lambda i, j: (0, j)), pl.BlockSpec((tf, d), lambda i, j: (j, 0))],
        out_specs=pl.BlockSpec((tm, d), lambda i, j: (i, 0)),
        out_shape=jax.ShapeDtypeStruct((m, d), F32),
        scratch_shapes=[pltpu.VMEM((tm, d), _MX)],
        compiler_params=_cp("parallel", "arbitrary"))(h, g, w1, w2)


def _ple_kernel(h_ref, p_ref, g_ref, wg_ref, wp_ref, o_ref):
    h = h_ref[...]
    gate = _sigmoid(_dot(_rms(h, g_ref[...]), wg_ref[...]))
    o_ref[...] = h + gate * _dot(p_ref[...], wp_ref[...])


def _ple(h, p, g, wg, wp, tm):
    m, d = h.shape
    pd = p.shape[1]
    return pl.pallas_call(
        _ple_kernel, grid=(m // tm,),
        in_specs=[pl.BlockSpec((tm, d), lambda i: (i, 0)), pl.BlockSpec((tm, pd), lambda i: (i, 0)),
                  pl.BlockSpec((1, d), lambda i: (0, 0)), pl.BlockSpec((d, d), lambda i: (0, 0)),
                  pl.BlockSpec((pd, d), lambda i: (0, 0))],
        out_specs=pl.BlockSpec((tm, d), lambda i: (i, 0)),
        out_shape=jax.ShapeDtypeStruct((m, d), F32), compiler_params=_cp("parallel"))(h, p, g, wg, wp)


_SEG_U = 0
_SEG_Q = POOL_DIM
_SEG_KV = POOL_DIM + NSA_DIM
_SEG_GL = _SEG_KV + 6 * KV_DIM


def _inproj_kernel(x_ref, gn_ref, w_ref, qg_ref, ksg_ref, kwg_ref,
                   u_ref, q_ref, nk_ref, nv_ref, kw_ref, vw_ref, gate_ref, z_ref, *, q_scale):
    xn = _rms(x_ref[...], gn_ref[...])
    z_ref[...] = _dot(xn, w_ref[...])
    u_ref[...] = z_ref[:, _SEG_U:_SEG_U + POOL_DIM]
    for c in range(NSA_DIM // 128):
        zt = z_ref[:, _SEG_Q + c * 128:_SEG_Q + (c + 1) * 128]
        q_ref[:, c * 128:(c + 1) * 128] = (_head_rms_tile(zt, qg_ref[...]) * q_scale).astype(q_ref.dtype)
    kv = _SEG_KV
    nk_ref[:, 0:KV_DIM] = z_ref[:, kv:kv + KV_DIM]
    nv_ref[:, 0:KV_DIM] = z_ref[:, kv + KV_DIM:kv + 2 * KV_DIM]
    nv_ref[:, KV_DIM:2 * KV_DIM] = z_ref[:, kv + 3 * KV_DIM:kv + 4 * KV_DIM]
    vw_ref[...] = z_ref[:, kv + 5 * KV_DIM:kv + 6 * KV_DIM]
    for c in range(KV_DIM // 128):
        zs = z_ref[:, kv + 2 * KV_DIM + c * 128:kv + 2 * KV_DIM + (c + 1) * 128]
        nk_ref[:, KV_DIM + c * 128:KV_DIM + (c + 1) * 128] = _head_rms_tile(zs, ksg_ref[...])
        zw = z_ref[:, kv + 4 * KV_DIM + c * 128:kv + 4 * KV_DIM + (c + 1) * 128]
        kw_ref[:, c * 128:(c + 1) * 128] = _head_rms_tile(zw, kwg_ref[...])
    gate_ref[...] = _sigmoid(z_ref[:, _SEG_GL:_SEG_GL + 128])


def _inproj(x, gn, w, qg, ksg, kwg, tm, q_scale):
    m, d = x.shape
    row = lambda i: (i, 0)
    fix = lambda i: (0, 0)
    outs = [(POOL_DIM, F32), (NSA_DIM, _MX), (2 * KV_DIM, F32), (2 * KV_DIM, F32), (KV_DIM, F32), (KV_DIM, F32),
            (128, F32)]
    return pl.pallas_call(
        functools.partial(_inproj_kernel, q_scale=q_scale), grid=(m // tm,),
        in_specs=[pl.BlockSpec((tm, d), row), pl.BlockSpec((1, d), fix), pl.BlockSpec((d, IN_PAD), fix),
                  pl.BlockSpec((1, 128), fix), pl.BlockSpec((1, 128), fix), pl.BlockSpec((1, 128), fix)],
        out_specs=[pl.BlockSpec((tm, c), row) for c, _ in outs],
        out_shape=[jax.ShapeDtypeStruct((m, c), dt) for c, dt in outs],
        scratch_shapes=[pltpu.VMEM((tm, IN_PAD), F32)],
        compiler_params=_cp("parallel"))(x, gn, w, qg, ksg, kwg)


def _pool_select(sums, u, cnt):
    grp = lax.broadcasted_iota(jnp.int32, u.shape, 1) >> 6
    ssel = jnp.where(grp == 0, sums[2], jnp.where(grp == 1, sums[4], jnp.where(grp == 2, sums[8], sums[16])))
    return ssel / cnt - u, grp


def _pool_kernel(u_ref, wbd_ref, sc_ref, y_ref, ext_ref, *, tm):
    i = pl.program_id(1)

    @pl.when(i == 0)
    def _():
        ext_ref[0:16, :] = jnp.zeros((16, POOL_DIM), F32)

    u = u_ref[...]
    ext_ref[16:16 + tm, :] = u
    acc = u
    sums = {}
    for s in range(1, 16):
        acc = acc + ext_ref[16 - s:16 - s + tm, :]
        if s + 1 in POOL_WINDOWS:
            sums[s + 1] = acc
    grp = lax.broadcasted_iota(jnp.int32, u.shape, 1) >> 6
    win = jnp.where(grp == 0, 2, jnp.where(grp == 1, 4, jnp.where(grp == 2, 8, 16)))
    pos = i * tm + lax.broadcasted_iota(jnp.int32, u.shape, 0)
    cnt = jnp.minimum(win, pos + 1).astype(F32)
    mixed, _ = _pool_select(sums, u, cnt)
    y_ref[...] = _dot(mixed, wbd_ref[...]) * sc_ref[...]
    ext_ref[0:16, :] = ext_ref[tm:tm + 16, :]


def _pool_prompt(u, wbd, scale, tm):
    n, t, c = u.shape
    return pl.pallas_call(
        functools.partial(_pool_kernel, tm=tm), grid=(n, t // tm),
        in_specs=[pl.BlockSpec((None, tm, c), lambda b, i: (b, i, 0)), pl.BlockSpec((c, c), lambda b, i: (0, 0)),
                  pl.BlockSpec((1, c), lambda b, i: (0, 0))],
        out_specs=pl.BlockSpec((None, tm, c), lambda b, i: (b, i, 0)),
        out_shape=jax.ShapeDtypeStruct((n, t, c), F32),
        scratch_shapes=[pltpu.VMEM((tm + 16, c), F32)],
        compiler_params=_cp("parallel", "arbitrary"))(u, wbd, scale)


def _pool_step_kernel(buf_ref, u_ref, wbd_ref, sc_ref, y_ref, *, pos0):
    u = u_ref[...]
    acc = u
    sums = {}
    for s in range(1, 16):
        acc = acc + buf_ref[:, POOL_BUF - s, :]
        if s + 1 in POOL_WINDOWS:
            sums[s + 1] = acc
    grp = lax.broadcasted_iota(jnp.int32, u.shape, 1) >> 6
    win = jnp.where(grp == 0, 2, jnp.where(grp == 1, 4, jnp.where(grp == 2, 8, 16)))
    cnt = jnp.minimum(win, pos0 + 1).astype(F32)
    mixed, _ = _pool_select(sums, u, cnt)
    y_ref[...] = _dot(mixed, wbd_ref[...]) * sc_ref[...]


def _pool_step(buf, u, wbd, scale, pos0):
    n, c = u.shape
    full = lambda *shape: pl.BlockSpec(shape, lambda i: (0,) * len(shape))
    return pl.pallas_call(
        functools.partial(_pool_step_kernel, pos0=pos0), grid=(1,),
        in_specs=[full(n, POOL_BUF, c), full(n, c), full(c, c), full(1, c)],
        out_specs=full(n, c), out_shape=jax.ShapeDtypeStruct((n, c), F32),
        compiler_params=_cp("arbitrary"))(buf, u, wbd, scale)


def _pages_per_step(n_pages):
    return next(p for p in (8, 4, 2, 1) if n_pages % p == 0)


def _compress_kernel(pt_ref, *refs, n_pages, pps):
    xk_refs, xv_refs = refs[:pps], refs[pps:2 * pps]
    pek_ref, pev_ref, wk_ref, wv_ref, kcg_ref, kc_ref, vc_ref, xs_k, xs_v, bsh = refs[2 * pps:]
    j = pl.program_id(1)
    n_tiles = KV_DIM // 128
    for p in range(pps):
        row0 = pl.multiple_of((j * pps + p) * PAGE, PAGE)
        for c in range(n_tiles):
            xs_k[c, pl.ds(row0, PAGE), :] = xk_refs[p][:, c * 128:(c + 1) * 128]
            xs_v[c, pl.ds(row0, PAGE), :] = xv_refs[p][:, c * 128:(c + 1) * 128]

    @pl.when(j == n_pages // pps - 1)
    def _():
        nh = n_pages * (PAGE // CMP_STRIDE)

        def run(xs, c, pe_ref, w_ref):
            a = jnp.zeros((nh, 128), F32)
            b = jnp.zeros((nh, 128), F32)
            for l in range(CMP_STRIDE):
                xl = xs[c, pl.ds(l, nh, stride=CMP_STRIDE), :]
                a = a + _dot(xl + pe_ref[l:l + 1, :], w_ref[l])
                b = b + _dot(xl + pe_ref[CMP_STRIDE + l:CMP_STRIDE + l + 1, :], w_ref[CMP_STRIDE + l])
            bsh[0:nh, :] = b
            bsh[nh:nh + 8, :] = jnp.zeros((8, 128), F32)
            return a + bsh[1:nh + 1, :]

        for c in range(n_tiles):
            kc_ref[:, c * 128:(c + 1) * 128] = _head_rms_tile(run(xs_k, c, pek_ref, wk_ref), kcg_ref[...])
            vc_ref[:, c * 128:(c + 1) * 128] = run(xs_v, c, pev_ref, wv_ref)


def _compress(cache_k, cache_v, page_table, e, pek, pev, wk, wv, kcg):
    nb, n_pages = page_table.shape
    nh = n_pages * (PAGE // CMP_STRIDE)
    pps = _pages_per_step(n_pages)
    page = lambda p: pl.BlockSpec((None, None, PAGE, KV_DIM), lambda b, j, pt: (e, pt[b, j * pps + p], 0, 0))
    fix2 = lambda b, j, pt: (0, 0)
    fix3 = lambda b, j, pt: (0, 0, 0)
    out = pl.BlockSpec((None, nh, KV_DIM), lambda b, j, pt: (b, 0, 0))
    gs = pltpu.PrefetchScalarGridSpec(
        num_scalar_prefetch=1, grid=(nb, n_pages // pps),
        in_specs=[page(p) for p in range(pps)] * 2
                 + [pl.BlockSpec((CMP_LEN, 128), fix2), pl.BlockSpec((CMP_LEN, 128), fix2),
                    pl.BlockSpec((CMP_LEN, 128, 128), fix3), pl.BlockSpec((CMP_LEN, 128, 128), fix3),
                    pl.BlockSpec((1, 128), fix2)],
        out_specs=[out, out],
        scratch_shapes=[pltpu.VMEM((KV_DIM // 128, n_pages * PAGE, 128), F32),
                        pltpu.VMEM((KV_DIM // 128, n_pages * PAGE, 128), F32),
                        pltpu.VMEM((nh + 8, 128), F32)])
    return pl.pallas_call(
        functools.partial(_compress_kernel, n_pages=n_pages, pps=pps), grid_spec=gs,
        out_shape=[jax.ShapeDtypeStruct((nb, nh, KV_DIM), F32)] * 2,
        compiler_params=_cp("parallel", "arbitrary"),
    )(page_table, *([cache_k] * pps), *([cache_v] * pps), pek, pev, wk, wv, kcg)


def _topk_rows(score, n_top):
    rid = lax.broadcasted_iota(jnp.int32, score.shape, 0).astype(F32)
    sel = jnp.zeros(score.shape, F32)
    for _ in range(n_top):
        m = jnp.max(score, axis=0, keepdims=True)
        first = jnp.min(jnp.where(score == m, rid, 1e9), axis=0, keepdims=True)
        hit = rid == first
        sel = jnp.where(hit, 1.0, sel)
        score = jnp.where(hit, -jnp.inf, score)
    return sel


def _nsa_prompt_kernel(sl_ref, q_ref, kc_ref, vct_ref, ks_ref, vst_ref, kw_ref, vwt_ref, gt_ref, mimpt_ref,
                       o_ref, sel_ref, s_ref, acc_ref, p_ref, bias_ref, mask_ref, *, n_cmp, n_slc, n_top):
    g = pl.program_id(1)
    i = pl.program_id(2)
    nh = kc_ref.shape[0]
    q = q_ref[...].reshape(REP * QBLK, HD)
    qpos0 = i * QBLK
    pos = qpos0 + lax.broadcasted_iota(jnp.int32, (1, QBLK), 1)
    slope = [sl_ref[g * REP + r] * LOG2E for r in range(REP)]

    cid = lax.broadcasted_iota(jnp.int32, (nh, QBLK), 0)
    dist_c = pos - (cid * CMP_STRIDE + (CMP_LEN - 1))
    valid_c = (dist_c >= 0) & (cid < n_cmp)
    dist_cf = dist_c.astype(F32)
    s_c = _dot_nt(kc_ref[...], q)
    vct = vct_ref[...]
    imp = jnp.zeros((nh, QBLK), F32)
    o_cmp = []
    for r in range(REP):
        s = jnp.where(valid_c, s_c[:, r * QBLK:(r + 1) * QBLK] - slope[r] * dist_cf, NEG)
        e = jnp.where(valid_c, jnp.exp2(s - jnp.max(s, axis=0, keepdims=True)), 0.0)
        p = e / jnp.maximum(jnp.sum(e, axis=0, keepdims=True), 1e-30)
        imp = imp + p
        o_cmp.append(_dot(vct, p))
    imp_slc = _dot2r(mimpt_ref[...], imp)
    blk = lax.broadcasted_iota(jnp.int32, imp_slc.shape, 0)
    cur = pos >> 6
    forced = (blk == 0) | (blk == cur) | (blk == cur - 1)
    causal = blk * SLC_LEN <= pos
    score = jnp.where(forced, BIG, jnp.where(causal, imp_slc, -BIG))
    score = jnp.where(blk < n_slc, score, -jnp.inf)
    sel_ref[...] = _topk_rows(score, n_top)

    d0 = (lax.broadcasted_iota(jnp.int32, (KEY_TILE, QBLK), 1)
          - lax.broadcasted_iota(jnp.int32, (KEY_TILE, QBLK), 0))
    d0f = d0.astype(F32)
    for r in range(REP):
        bias_ref[r] = -slope[r] * d0f
    mask_ref[0] = jnp.where(d0 >= 0, 0.0, NEG)
    mask_ref[1] = jnp.where(d0 < 0, 0.0, NEG)
    hi = (qpos0 + QBLK - 1) // KEY_TILE

    def sweep(lo, k_ref, vt_ref, mask_fn):
        def scores(c):
            k0 = pl.multiple_of(c * KEY_TILE, KEY_TILE)
            return _dot_nt(k_ref[pl.ds(k0, KEY_TILE), :], q)

        def body(c, carry):
            m, l, p_prev = carry
            pv = _dot(vt_ref[jnp.maximum(c - 1, lo)], p_prev)
            s_next = scores(jnp.minimum(c + 1, hi))
            off = qpos0 - c * KEY_TILE
            madd = mask_fn(c, off)
            off_f = off.astype(F32)
            cur = c % 2
            s, shift = [], []
            for r in range(REP):
                s_r = s_ref[cur, :, r * QBLK:(r + 1) * QBLK] + bias_ref[r] + madd
                s.append(s_r)
                shift.append(jnp.max(s_r, axis=0, keepdims=True) - slope[r] * off_f)
            m_new = jnp.maximum(m, jnp.concatenate(shift, axis=1))
            p = jnp.concatenate([jnp.exp2(s[r] - (m_new[:, r * QBLK:(r + 1) * QBLK] + slope[r] * off_f))
                                 for r in range(REP)], axis=1)
            alpha = jnp.exp2(m - m_new)
            l = alpha * l + jnp.sum(p, axis=0, keepdims=True)
            acc_ref[...] = alpha * (acc_ref[...] + pv)
            s_ref[(c + 1) % 2] = s_next
            return m_new, l, p.astype(_MX)

        s_ref[lo % 2] = scores(lo)
        acc_ref[...] = jnp.zeros(acc_ref.shape, F32)
        p_ref[...] = jnp.zeros(p_ref.shape, p_ref.dtype)
        init = (jnp.full((1, REP * QBLK), NEG, F32), jnp.zeros((1, REP * QBLK), F32), p_ref[...])
        _, l, p_last = lax.fori_loop(lo, hi + 1, body, init)
        return (acc_ref[...] + _dot(vt_ref[hi], p_last)) / l

    def slc_mask(c, off):
        per_tile = KEY_TILE // SLC_LEN
        rows = [jnp.broadcast_to(sel_ref[pl.ds(per_tile * c + b, 1), :], (SLC_LEN, QBLK)) for b in range(per_tile)]
        picked = jnp.where(jnp.concatenate(rows, axis=0) > 0.5, 0.0, NEG)
        return picked + jnp.where(off == 0, mask_ref[0], 0.0)

    def win_mask(c, off):
        return jnp.where(off == 0, mask_ref[0], jnp.where(off == WINDOW, mask_ref[1], 0.0))

    o_slc = sweep(0, ks_ref, vst_ref, slc_mask)
    lo_win = jnp.maximum(qpos0 - WINDOW, 0) // KEY_TILE
    o_win = sweep(lo_win, kw_ref, vwt_ref, win_mask)

    gates = gt_ref[...]
    for r in range(REP):
        sl = slice(r * QBLK, (r + 1) * QBLK)
        o_ref[r * HD:(r + 1) * HD, :] = (gates[3 * r:3 * r + 1, :] * o_cmp[r] + gates[3 * r + 1:3 * r + 2, :] * o_slc[:, sl]
                                         + gates[3 * r + 2:3 * r + 3, :] * o_win[:, sl])
    o_ref[REP * HD:, :] = jnp.zeros((o_ref.shape[0] - REP * HD, QBLK), F32)


def _nsa_prompt(slopes, q_h, kc_h, vc_t, ks_h, vs_t, kw_h, vw_t, gates_t, mimp_t, n_cmp):
    n, _, t, _ = q_h.shape
    assert t % KEY_TILE == 0 and QBLK == KEY_TILE and WINDOW % KEY_TILE == 0
    nh = kc_h.shape[2]
    n_slc = t // SLC_LEN
    nq = t // QBLK
    nkt = t // KEY_TILE
    seq = lambda *shape: pl.BlockSpec((None, None) + shape, lambda b, g, i: (b, g) + (0,) * len(shape))
    return pl.pallas_call(
        functools.partial(_nsa_prompt_kernel, n_cmp=n_cmp, n_slc=n_slc, n_top=min(SLC_TOP, n_slc)),
        grid=(n, KVH, nq),
        in_specs=[pl.BlockSpec(memory_space=pltpu.SMEM),
                  pl.BlockSpec((None, REP, QBLK, HD), lambda b, g, i: (b, g, i, 0)),
                  seq(nh, HD), seq(HD, nh), seq(t, HD), seq(nkt, HD, KEY_TILE), seq(t, HD), seq(nkt, HD, KEY_TILE),
                  pl.BlockSpec((None, None, 9, QBLK), lambda b, g, i: (b, g, 0, i)),
                  pl.BlockSpec(mimp_t.shape, lambda b, g, i: (0, 0))],
        out_specs=pl.BlockSpec((None, None, 256, QBLK), lambda b, g, i: (g, b, 0, i)),
        out_shape=jax.ShapeDtypeStruct((KVH, n, 256, t), F32),
        scratch_shapes=[pltpu.VMEM((mimp_t.shape[0], QBLK), F32), pltpu.VMEM((2, KEY_TILE, REP * QBLK), F32),
                        pltpu.VMEM((HD, REP * QBLK), F32), pltpu.VMEM((KEY_TILE, REP * QBLK), _MX),
                        pltpu.VMEM((REP, KEY_TILE, QBLK), F32), pltpu.VMEM((2, KEY_TILE, QBLK), F32)],
        compiler_params=_cp("parallel", "parallel", "arbitrary"),
    )(slopes, q_h, kc_h, vc_t, ks_h, vs_t, kw_h, vw_t, gates_t, mimp_t)


def _nsa_sample_kernel(pt_ref, *refs, n_pages, pps, n_cmp, n_slc, n_top, wbuf):
    q_ref, kc_ref, vc_ref = refs[:3]
    ck_refs, cv_refs = refs[3:3 + pps], refs[3 + pps:3 + 2 * pps]
    (new_ref, wk_ref, wv_ref, g_ref, sl_ref, grp_ref, mimp_ref, o_ref, m_ref, l_ref, acc_ref, sel_ref, ocmp_ref,
     ofull_ref) = refs[3 + 2 * pps:]
    j = pl.program_id(1)
    span = pps * PAGE
    pos0 = n_pages * PAGE
    q = q_ref[...]
    slope = sl_ref[:, 0:1]
    nrow = q.shape[0]

    @pl.when(j == 0)
    def _():
        nh = kc_ref.shape[0]
        cid = lax.broadcasted_iota(jnp.int32, (nrow, nh), 1)
        dist = pos0 - (cid * CMP_STRIDE + (CMP_LEN - 1))
        valid = (dist >= 0) & (cid < n_cmp)
        p = _masked_softmax(_dot_nt(q, kc_ref[...]) - slope * dist.astype(F32), valid)
        ocmp_ref[...] = _dot(p, vc_ref[...])
        imp = _dot2(_dot2r(grp_ref[...], p), mimp_ref[...])
        blk = lax.broadcasted_iota(jnp.int32, imp.shape, 1)
        cur = pos0 // SLC_LEN
        forced = (blk == 0) | (blk == cur) | (blk == cur - 1)
        causal = blk * SLC_LEN <= pos0
        score = jnp.where(forced, BIG, jnp.where(causal, imp, -BIG))
        sel_ref[...] = _topk_mask(score, n_slc, n_top)
        m_ref[...] = jnp.full(m_ref.shape, NEG, F32)
        l_ref[...] = jnp.zeros(l_ref.shape, F32)
        acc_ref[...] = jnp.zeros(acc_ref.shape, F32)

    sel = sel_ref[...]
    e_row = lax.broadcasted_iota(jnp.int32, (sel.shape[1], span), 0)
    e_col = lax.broadcasted_iota(jnp.int32, (sel.shape[1], span), 1) >> 6
    picked = _dot(sel, jnp.where(e_row == (span // SLC_LEN) * j + e_col, 1.0, 0.0)) > 0.5
    kpos = j * span + lax.broadcasted_iota(jnp.int32, (nrow, span), 1)
    dist = pos0 - kpos
    keys = jnp.concatenate([r[...].astype(_MX) for r in ck_refs], axis=0)
    vals = jnp.concatenate([r[...].astype(_MX) for r in cv_refs], axis=0)
    s = _dot_nt(q, keys) - slope * dist.astype(F32)
    _online_update(s, picked & (dist >= 0), vals, m_ref, l_ref, acc_ref)

    @pl.when(j == n_pages // pps - 1)
    def _():
        qf = q.astype(F32)
        new = new_ref[...]
        s_new = jnp.sum(qf * new[0:1, :], axis=-1, keepdims=True)
        ok = sel[:, n_slc - 1:n_slc] > 0.5
        s_new = jnp.where(ok, s_new, NEG)
        m_old = m_ref[...]
        m_new = jnp.maximum(m_old, s_new)
        p_new = jnp.where(ok, jnp.exp(s_new - m_new), 0.0)
        alpha = jnp.exp(m_old - m_new)
        l_tot = alpha * l_ref[...] + p_new
        acc = alpha * acc_ref[...] + p_new * new[1:2, :]
        o_slc = acc / jnp.maximum(l_tot, 1e-30)

        idx = lax.broadcasted_iota(jnp.int32, (nrow, wbuf), 1)
        dw = wbuf - idx
        valid_w = (dw >= 0) & (dw < WINDOW)
        s_w = jnp.where(valid_w, _dot_nt(q, wk_ref[...]) - slope * dw.astype(F32), NEG)
        s_wn = jnp.sum(qf * new[2:3, :], axis=-1, keepdims=True)
        m_w = jnp.maximum(jnp.max(s_w, axis=-1, keepdims=True), s_wn)
        p_w = jnp.where(valid_w, jnp.exp(s_w - m_w), 0.0)
        p_wn = jnp.exp(s_wn - m_w)
        den = jnp.maximum(jnp.sum(p_w, axis=-1, keepdims=True) + p_wn, 1e-30)
        o_win = (_dot(p_w, wv_ref[...]) + p_wn * new[3:4, :]) / den

        gates = g_ref[...]
        ofull_ref[...] = gates[:, 0:1] * ocmp_ref[...] + gates[:, 1:2] * o_slc + gates[:, 2:3] * o_win
        row_g = lax.broadcasted_iota(jnp.int32, (nrow, HD), 0) >> 2
        o = jnp.zeros((nrow, HD), F32)
        for gg in range(KVH):
            o = o + jnp.where(row_g == gg, ofull_ref[:, gg * HD:(gg + 1) * HD], 0.0)
        o_ref[...] = o


def _nsa_sample(page_table, e, qbd, kc, vc, cache_k, cache_v, new_rows, win_k, win_v, gates16, slopes16, grp16,
                mimp, n_cmp, n_slc):
    nb, n_pages = page_table.shape
    nh = kc.shape[1]
    wbuf = win_k.shape[2]
    per_b = lambda *shape: pl.BlockSpec((None,) + shape, lambda b, j, pt: (b,) + (0,) * len(shape))
    fix = lambda *shape: pl.BlockSpec(shape, lambda b, j, pt: (0,) * len(shape))
    pps = _pages_per_step(n_pages)
    page = lambda p: pl.BlockSpec((None, None, PAGE, KV_DIM), lambda b, j, pt: (e, pt[b, j * pps + p], 0, 1))
    win = pl.BlockSpec((None, None, wbuf, KV_DIM), lambda b, j, pt: (e, b, 0, 0))
    nrow = qbd.shape[1]
    gs = pltpu.PrefetchScalarGridSpec(
        num_scalar_prefetch=1, grid=(nb, n_pages // pps),
        in_specs=[per_b(nrow, KV_DIM), per_b(nh, KV_DIM), per_b(nh, KV_DIM)] + [page(p) for p in range(pps)] * 2
                 + [per_b(8, KV_DIM), win, win, per_b(nrow, 128), fix(nrow, 128), fix(nrow, nrow), fix(*mimp.shape)],
        out_specs=per_b(nrow, HD),
        scratch_shapes=[pltpu.VMEM((nrow, 1), F32), pltpu.VMEM((nrow, 1), F32), pltpu.VMEM((nrow, KV_DIM), F32),
                        pltpu.VMEM((nrow, mimp.shape[1]), F32), pltpu.VMEM((nrow, KV_DIM), F32),
                        pltpu.VMEM((nrow, KV_DIM), F32)])
    return pl.pallas_call(
        functools.partial(_nsa_sample_kernel, n_pages=n_pages, pps=pps, n_cmp=n_cmp, n_slc=n_slc,
                          n_top=min(SLC_TOP, n_slc), wbuf=wbuf),
        grid_spec=gs, out_shape=jax.ShapeDtypeStruct((nb, nrow, HD), F32),
        compiler_params=_cp("parallel", "arbitrary"),
    )(page_table, qbd, kc, vc, *([cache_k] * pps), *([cache_v] * pps), new_rows, win_k, win_v, gates16, slopes16,
      grp16, mimp)


def _outproj_kernel(h_ref, y_ref, o_ref, wp_ref, wn_ref, out_ref):
    acc = h_ref[...] + _dot(y_ref[...], wp_ref[...])
    for g in range(KVH):
        acc = acc + _dot(o_ref[g], wn_ref[g])
    out_ref[...] = acc


def _outproj(h, y_pool, o4, wp, wn, tm):
    m, d = h.shape
    return pl.pallas_call(
        _outproj_kernel, grid=(m // tm,),
        in_specs=[pl.BlockSpec((tm, d), lambda i: (i, 0)), pl.BlockSpec((tm, POOL_DIM), lambda i: (i, 0)),
                  pl.BlockSpec((KVH, tm, 256), lambda i: (0, i, 0)), pl.BlockSpec((POOL_DIM, d), lambda i: (0, 0)),
                  pl.BlockSpec((KVH, 256, d), lambda i: (0, 0, 0))],
        out_specs=pl.BlockSpec((tm, d), lambda i: (i, 0)),
        out_shape=jax.ShapeDtypeStruct((m, d), F32), compiler_params=_cp("parallel"))(h, y_pool, o4, wp, wn)


def _rwkv_proj_kernel(*refs, has_vres):
    if has_vres:
        (xn_ref, xp_ref, mu_ref, wr_ref, wk_ref, wv_ref, w1_ref, w2_ref, a1_ref, a2_ref, g1_ref, g2_ref,
         w0_ref, a0_ref, kk_ref, ka_ref, vf_ref, v0_ref, v1_ref, v2_ref,
         r_out, w_out, k_out, v_out, kk_out, b_out, g_out) = refs
    else:
        (xn_ref, xp_ref, mu_ref, wr_ref, wk_ref, wv_ref, w1_ref, w2_ref, a1_ref, a2_ref, g1_ref, g2_ref,
         w0_ref, a0_ref, kk_ref, ka_ref,
         r_out, w_out, k_out, v_out, kk_out, b_out, g_out) = refs
    xn = xn_ref[...]
    xx = xp_ref[...] - xn
    mix = lambda j: xn + xx * mu_ref[j:j + 1, :]
    xr, xw, xk, xv, xa, xg = [mix(j) for j in range(6)]
    r_out[...] = _dot(xr, wr_ref[...])
    k = _dot(xk, wk_ref[...])
    v = _dot(xv, wv_ref[...])
    z = w0_ref[...] + _dot(jnp.tanh(_dot(xw, w1_ref[...])), w2_ref[...])
    w_log = -(jnp.maximum(-z, 0.0) + jnp.log(1.0 + jnp.exp(-jnp.abs(z)))) - 0.5
    w_out[...] = -jnp.exp(w_log)
    if has_vres:
        v = v + (vf_ref[...] - v) * _sigmoid(v0_ref[...] + _dot(_dot(xv, v1_ref[...]), v2_ref[...]))
    v_out[...] = v
    a = _sigmoid(a0_ref[...] + _dot(_dot(xa, a1_ref[...]), a2_ref[...]))
    g_out[...] = _dot(_sigmoid(_dot(xg, g1_ref[...])), g2_ref[...])
    kk = k * kk_ref[...]
    for c in range(kk.shape[1] // 128):
        sl = slice(c * 128, (c + 1) * 128)
        kt = kk[:, sl]
        kn = kt / jnp.maximum(jnp.sqrt(_seg64_sum(kt * kt)), 1e-12)
        kk_out[:, sl] = kn
        b_out[:, sl] = kn * a[:, sl]
    k_out[...] = k * (1.0 + (a - 1.0) * ka_ref[...])


def _rwkv_proj(xn, xprev, p, vres, tm):
    m, d = xn.shape
    row = pl.BlockSpec((tm, d), lambda i: (i, 0))
    fix = lambda a: pl.BlockSpec(a.shape, lambda i: (0,) * a.ndim)
    args = [xn, xprev, p['mu'], p['wr'], p['wk'], p['wv'], p['w1'], p['w2'], p['a1'], p['a2'], p['g1'], p['g2'],
            p['w0'], p['a0'], p['k_k'], p['k_a']]
    specs = [row, row] + [fix(a) for a in args[2:]]
    if vres is not None:
        vf, v0, v1, v2 = vres
        args += [vf, v0, v1, v2]
        specs += [row, fix(v0), fix(v1), fix(v2)]
    return pl.pallas_call(
        functools.partial(_rwkv_proj_kernel, has_vres=vres is not None), grid=(m // tm,),
        in_specs=specs, out_specs=[row] * 7, out_shape=[jax.ShapeDtypeStruct((m, d), F32)] * 7,
        compiler_params=_cp("parallel"))(*args)


_NN = (((1,), (0,)), ((), ()))
_NT = (((1,), (1,)), ((), ()))
_TN = (((0,), (0,)), ((), ()))


def _split(a):
    hi = a.astype(_MX)
    return hi, (a - hi.astype(F32)).astype(_MX)


def _dot3(a, b, dims=_NN, passes=3):
    dg = lambda x, y: lax.dot_general(x, y, dims, preferred_element_type=F32)
    a_hi, a_lo = _split(a)
    if passes == 1:
        return dg(a_hi, b.astype(_MX))
    b_hi, b_lo = _split(b)
    out = dg(a_hi, b_hi) + dg(a_hi, b_lo)
    return out if passes == 2 else out + dg(a_lo, b_hi)


def _wkv_kernel(r_ref, lw_ref, k_ref, v_ref, kk_ref, b_ref, s0_ref, o_ref, s_ref, *, hb):
    @pl.when(pl.program_id(2) == 0)
    def _():
        s_ref[...] = s0_ref[...]

    c = r_ref.shape[0]
    hd = lambda ref, h: ref[:, h * HD:(h + 1) * HD]
    row = lax.broadcasted_iota(jnp.int32, (c, c), 0)
    col = lax.broadcasted_iota(jnp.int32, (c, c), 1)
    incl = col <= row
    strict = col < row
    ltri = jnp.where(incl, 1.0, 0.0)
    levels = int(math.log2(c))
    heads = range(hb)
    lw = [hd(lw_ref, h) for h in heads]
    v = [hd(v_ref, h) for h in heads]
    p_in, p_solve, p_out = WKV_PASSES
    cum = [_dot3(ltri, lw[h], passes=2) for h in heads]
    g_in = [jnp.exp(cum[h]) for h in heads]
    g_inv = [jnp.exp(-cum[h]) for h in heads]
    b_t = [hd(b_ref, h) * g_inv[h] for h in heads]
    k_t = [hd(k_ref, h) * g_inv[h] for h in heads]
    ar = [jnp.concatenate([-hd(kk_ref, h) * jnp.exp(cum[h] - lw[h]), hd(r_ref, h) * g_in[h]], axis=0) for h in heads]
    pb = [_dot3(ar[h], b_t[h], _NT, p_in) for h in heads]
    pk = [_dot3(ar[h], k_t[h], _NT, p_in) for h in heads]
    xs = [_dot3(ar[h], s_ref[h], _NT, p_in) for h in heads]
    n_mat = [jnp.where(strict, pb[h][:c], 0.0) for h in heads]
    x = [xs[h][:c] + _dot3(jnp.where(strict, pk[h][:c], 0.0), v[h], _NN, p_in) for h in heads]
    for lvl in range(levels):
        x = [x[h] + _dot3(n_mat[h], x[h], _NN, p_solve) for h in heads]
        if lvl + 1 < levels:
            n_mat = [_dot3(n_mat[h], n_mat[h], _NN, p_solve) for h in heads]
    for h in heads:
        o_ref[:, h * HD:(h + 1) * HD] = (xs[h][c:] + _dot3(jnp.where(incl, pb[h][c:], 0.0), x[h], _NN, p_out)
                                         + _dot3(jnp.where(incl, pk[h][c:], 0.0), v[h], _NN, p_out))
    for h in heads:
        s_ref[h] = ((s_ref[h] + _dot3(x[h], b_t[h], _TN, p_out) + _dot3(v[h], k_t[h], _TN, p_out))
                    * g_in[h][c - 1:c, :])


def _wkv_scan(r, lw, k, v, kk, b, s0, tc, hb):
    n, t, d = r.shape
    nh_ = d // HD
    tok = pl.BlockSpec((None, tc, hb * HD), lambda bb, hh, tt: (bb, tt, hh))
    st = pl.BlockSpec((None, hb, HD, HD), lambda bb, hh, tt: (bb, hh, 0, 0))
    return pl.pallas_call(
        functools.partial(_wkv_kernel, hb=hb), grid=(n, nh_ // hb, t // tc),
        in_specs=[tok] * 6 + [st], out_specs=[tok, st],
        out_shape=[jax.ShapeDtypeStruct((n, t, d), F32), jax.ShapeDtypeStruct((n, nh_, HD, HD), F32)],
        compiler_params=_cp("parallel", "parallel", "arbitrary"))(r, lw, k, v, kk, b, s0)


def _wkv_step_kernel(r_ref, lw_ref, k_ref, kk_ref, b_ref, vc_ref, s0_ref, o_ref, s_ref):
    s = s0_ref[...]
    sa = -jnp.sum(s * kk_ref[...], axis=-1, keepdims=True)
    s = s * jnp.exp(lw_ref[...]) + sa * b_ref[...] + vc_ref[...] * k_ref[...]
    s_ref[...] = s
    o_ref[...] = jnp.sum(s * r_ref[...], axis=-1, keepdims=True)


def _wkv_step(r, w, k, kk, b, v_col, s0):
    n, nh_ = r.shape[:2]
    rowv = pl.BlockSpec((None, nh_, 1, HD), lambda i: (i, 0, 0, 0))
    colv = pl.BlockSpec((None, nh_, HD, 1), lambda i: (i, 0, 0, 0))
    st = pl.BlockSpec((None, nh_, HD, HD), lambda i: (i, 0, 0, 0))
    return pl.pallas_call(
        _wkv_step_kernel, grid=(n,), in_specs=[rowv] * 5 + [colv, st], out_specs=[colv, st],
        out_shape=[jax.ShapeDtypeStruct((n, nh_, HD, 1), F32), jax.ShapeDtypeStruct((n, nh_, HD, HD), F32)],
        compiler_params=_cp("parallel"))(r, w, k, kk, b, v_col, s0)


def _rwkv_out_kernel(h_ref, o_ref, r_ref, k_ref, v_ref, g_ref, lnw_ref, lnb_ref, rk_ref, wo_ref, out_ref, y_ref):
    for c in range(h_ref.shape[1] // 128):
        sl = slice(c * 128, (c + 1) * 128)
        o = o_ref[:, sl]
        dlt = o - _seg64_sum(o) * (1.0 / HD)
        var = _seg64_sum(dlt * dlt) * (1.0 / HD)
        on = dlt * lax.rsqrt(var + LN_X_EPS) * lnw_ref[:, sl] + lnb_ref[:, sl]
        bonus = _seg64_sum(r_ref[:, sl] * k_ref[:, sl] * rk_ref[:, sl])
        y_ref[:, sl] = ((on + bonus * v_ref[:, sl]) * g_ref[:, sl]).astype(y_ref.dtype)
    out_ref[...] = h_ref[...] + jnp.dot(y_ref[...], wo_ref[...], preferred_element_type=F32)


def _rwkv_out(h, o, r, k, v, g, lnw, lnb, rk, wo, tm):
    m, d = h.shape
    row = pl.BlockSpec((tm, d), lambda i: (i, 0))
    vec = pl.BlockSpec((1, d), lambda i: (0, 0))
    return pl.pallas_call(
        _rwkv_out_kernel, grid=(m // tm,),
        in_specs=[row] * 6 + [vec] * 3 + [pl.BlockSpec((d, d), lambda i: (0, 0))],
        out_specs=row, out_shape=jax.ShapeDtypeStruct((m, d), F32),
        scratch_shapes=[pltpu.VMEM((tm, d), _MX)],
        compiler_params=_cp("parallel"))(h, o, r, k, v, g, lnw, lnb, rk, wo)


def _row_tile(m, pref):
    return pref if m % pref == 0 else m


def _block_diag(w):
    g, a, b = w.shape
    eye = jnp.eye(g, dtype=w.dtype)
    return (eye[:, None, :, None] * w[:, :, None, :]).reshape(g * a, g * b)


def _imp_matrix(nh, n_cmp, n_slc, width):
    per = SLC_LEN // CMP_STRIDE
    lead = CMP_LEN // CMP_STRIDE - 1
    c = np.arange(nh)[:, None]
    j = np.arange(width)[None, :]
    m = (c - per * j >= -lead) & (c - per * j < per) & (c < n_cmp) & (j < n_slc)
    return jnp.asarray(m.astype(np.float32), dtype=_MX)


def _pad_lanes(n):
    return -(-n // 128) * 128


def _heads(x, n, t):
    return x.reshape(n, t, -1, HD).transpose(0, 2, 1, 3).astype(_MX)


def kernel(x_prompt, x_sample, cache_k, cache_v, page_table, state_win_k, state_win_v, state_pool, state_shift, state_wkv, p_prompt, p_sample, norm_mix, norm_mlp, norm_ple, mlp_w1, mlp_w2, ple_proj, ple_gate, even_w_in, even_w_out, pool_w, pool_scale, q_gain, k_gain, cmp_pe, cmp_w, rwkv_mu, rwkv_wr, rwkv_wk, rwkv_wv, rwkv_wo, rwkv_w0, rwkv_w1, rwkv_w2, rwkv_a0, rwkv_a1, rwkv_a2, rwkv_v0, rwkv_v1, rwkv_v2, rwkv_g1, rwkv_g2, rwkv_kk, rwkv_ka, rwkv_rk, rwkv_lnw, rwkv_lnb):
    nb, t, d = x_prompt.shape
    ns = x_sample.shape[0]
    depth = norm_mix.shape[0]
    n_even = even_w_in.shape[0]
    n_pages = page_table.shape[1]
    past_len = n_pages * PAGE
    wbuf = state_win_k.shape[2]
    n_phys = cache_k.shape[1]
    mp = nb * t
    tm_p = _row_tile(mp, 512)
    tm_a = _row_tile(mp, 256)
    slopes = jnp.asarray(_alibi_slopes(KVH * REP))
    mx = lambda a: a.astype(_MX)
    row1 = lambda a: a.reshape(1, -1)
    two = lambda a: jnp.tile(a.reshape(1, HD), (1, 2))

    rid = np.arange(16)
    real = (rid % 4) < REP
    slopes16 = jnp.asarray(np.where(real, _alibi_slopes(KVH * REP)[np.minimum((rid // 4) * REP + rid % 4, 11)],
                                    0.0).astype(np.float32))[:, None] * jnp.ones((1, 128), F32)
    grp16 = jnp.asarray(((rid[:, None] // 4 == rid[None, :] // 4) & real[None, :]).astype(np.float32))

    cache_k5 = cache_k.reshape(n_even, n_phys, PAGE, 2 * KV_DIM)
    cache_v5 = cache_v.reshape(n_even, n_phys, PAGE, 2 * KV_DIM)
    win_k4 = state_win_k.reshape(n_even, ns, wbuf, KV_DIM)
    win_v4 = state_win_v.reshape(n_even, ns, wbuf, KV_DIM)

    h_p = x_prompt.reshape(mp, d)
    h_s = x_sample.reshape(ns, d)
    outs = {k_: [] for k_ in ('nk_p', 'nv_p', 'nk_s', 'nv_s', 'wk_p', 'wv_p', 'wk_s', 'wv_s', 'pl_p', 'pl_s',
                              'sh_p', 'sh_s', 'st_p', 'st_s')}
    vf_p = vf_s = None

    for i in range(depth):
        gn = row1(norm_mix[i])
        if i % 2 == 0:
            e = i // 2
            w_in = mx(jnp.pad(even_w_in[e], ((0, 0), (0, IN_PAD - even_w_in.shape[2]))))
            wbd = mx(_block_diag(pool_w[e]))
            psc = row1(pool_scale[e])
            qg, ksg, kwg, kcg = two(q_gain[e]), two(k_gain[e, 1]), two(k_gain[e, 2]), two(k_gain[e, 0])
            w_out = even_w_out[e]
            wo_pool = mx(w_out[:POOL_DIM])
            wo_nsa = mx(jnp.pad(w_out[POOL_DIM:].reshape(KVH, REP * HD, d), ((0, 0), (0, 256 - REP * HD), (0, 0))))
            pe_k = jnp.tile(cmp_pe[e, 0], (1, 2))
            pe_v = jnp.tile(cmp_pe[e, 1], (1, 2))
            eye = jnp.eye(2, dtype=F32)
            bd = lambda w: mx((eye[None, :, None, :, None] * w[:, None, :, None, :]).reshape(CMP_LEN, 128, 128))
            cw_k, cw_v = bd(cmp_w[e, 0]), bd(cmp_w[e, 1])

            u, q, nk, nv, kw, vw, gate = _inproj(h_p, gn, w_in, qg, ksg, kwg, tm_a, HD ** -0.5 * LOG2E)
            y_pool = _pool_prompt(u.reshape(nb, t, POOL_DIM), wbd, psc, _row_tile(t, 512))
            nh_p = t // CMP_STRIDE
            pt_p = jnp.arange(nb * (t // PAGE), dtype=jnp.int32).reshape(nb, t // PAGE)
            kc, vc = _compress(nk.reshape(1, mp // PAGE, PAGE, 2 * KV_DIM), nv.reshape(1, mp // PAGE, PAGE, 2 * KV_DIM),
                               pt_p, 0, pe_k, pe_v, cw_k, cw_v, kcg)
            n_cmp_p = nh_p - (CMP_LEN // CMP_STRIDE - 1)
            n_slc_p = t // SLC_LEN
            gates_t = gate[:, :KVH * REP * 3].reshape(nb, t, KVH, REP * 3).transpose(0, 2, 3, 1)
            chunks_t = lambda x: (x.reshape(nb, t // KEY_TILE, KEY_TILE, KVH, HD).transpose(0, 3, 1, 4, 2).astype(_MX))
            o4t = _nsa_prompt(slopes, _heads(q, nb, t), _heads(kc, nb, nh_p),
                              vc.reshape(nb, nh_p, KVH, HD).transpose(0, 2, 3, 1).astype(_MX),
                              _heads(nk[:, KV_DIM:], nb, t), chunks_t(nv[:, KV_DIM:]),
                              _heads(kw, nb, t), chunks_t(vw), gates_t,
                              _imp_matrix(nh_p, n_cmp_p, n_slc_p, -(-n_slc_p // 8) * 8).T, n_cmp_p)
            o4 = o4t.transpose(0, 1, 3, 2).reshape(KVH, mp, 256)
            h_p = _outproj(h_p, y_pool.reshape(mp, POOL_DIM), o4, wo_pool, wo_nsa, tm_p)
            outs['nk_p'].append(nk.reshape(nb, t, 2, KVH, HD))
            outs['nv_p'].append(nv.reshape(nb, t, 2, KVH, HD))
            kw3 = kw.reshape(nb, t, KVH, HD)
            vw3 = vw.reshape(nb, t, KVH, HD)
            if t < wbuf:
                zpad = jnp.zeros((nb, wbuf - t, KVH, HD), F32)
                kw3, vw3 = jnp.concatenate([zpad, kw3], 1), jnp.concatenate([zpad, vw3], 1)
            outs['wk_p'].append(kw3[:, -wbuf:])
            outs['wv_p'].append(vw3[:, -wbuf:])
            outs['pl_p'].append(u.reshape(nb, t, POOL_DIM)[:, -POOL_BUF:])

            u, q, nk, nv, kw, vw, gate = _inproj(h_s, gn, w_in, qg, ksg, kwg, ns, HD ** -0.5)
            y_pool = _pool_step(state_pool[e], u, wbd, psc, past_len)
            kc, vc = _compress(cache_k5, cache_v5, page_table, e, pe_k, pe_v, cw_k, cw_v, kcg)
            nh_s = kc.shape[1]
            n_cmp_s = nh_s - (CMP_LEN // CMP_STRIDE - 1)
            n_slc_s = -(-(past_len + 1) // SLC_LEN)
            q16 = jnp.pad(q.reshape(ns, KVH, REP, HD), ((0, 0), (0, 0), (0, 1), (0, 0))).reshape(ns, 16, 1, HD)
            gsel = jnp.asarray((np.arange(16)[:, None] // 4 == np.arange(KVH)[None, :]).astype(np.float32))
            qbd = (q16.astype(F32) * gsel[None, :, :, None]).reshape(ns, 16, KV_DIM).astype(_MX)
            new_rows = jnp.pad(jnp.stack([nk[:, KV_DIM:], nv[:, KV_DIM:], kw, vw], axis=1), ((0, 0), (0, 4), (0, 0)))
            g16 = jnp.pad(gate[:, :KVH * REP * 3].reshape(ns, KVH, REP, 3), ((0, 0), (0, 0), (0, 1), (0, 125)))
            o16 = _nsa_sample(page_table, e, qbd, mx(kc), mx(vc), cache_k5, cache_v5, new_rows, win_k4, win_v4,
                              g16.reshape(ns, 16, 128), slopes16, grp16,
                              _imp_matrix(nh_s, n_cmp_s, n_slc_s, _pad_lanes(n_slc_s)), n_cmp_s, n_slc_s)
            o4 = o16.reshape(ns, KVH, 4, HD)[:, :, :REP].reshape(ns, KVH, REP * HD).transpose(1, 0, 2)
            o4 = jnp.pad(o4, ((0, 0), (0, 0), (0, 256 - REP * HD)))
            h_s = _outproj(h_s, y_pool, o4, wo_pool, wo_nsa, ns)
            outs['nk_s'].append(nk.reshape(ns, 1, 2, KVH, HD))
            outs['nv_s'].append(nv.reshape(ns, 1, 2, KVH, HD))
            outs['wk_s'].append(jnp.concatenate([state_win_k[e], kw.reshape(ns, 1, KVH, HD)], axis=1)[:, -wbuf:])
            outs['wv_s'].append(jnp.concatenate([state_win_v[e], vw.reshape(ns, 1, KVH, HD)], axis=1)[:, -wbuf:])
            outs['pl_s'].append(jnp.concatenate([state_pool[e], u[:, None]], axis=1)[:, -POOL_BUF:])
        else:
            o = i // 2
            lora_in = lambda a: mx(jnp.pad(a, ((0, 0), (0, LORA_PAD - a.shape[1]))))
            lora_out = lambda a: mx(jnp.pad(a, ((0, LORA_PAD - a.shape[0]), (0, 0))))
            p = dict(mu=rwkv_mu[o], wr=mx(rwkv_wr[o]), wk=mx(rwkv_wk[o]), wv=mx(rwkv_wv[o]),
                     w1=lora_in(rwkv_w1[o]), w2=lora_out(rwkv_w2[o]), a1=lora_in(rwkv_a1[o]), a2=lora_out(rwkv_a2[o]),
                     g1=lora_in(rwkv_g1[o]), g2=lora_out(rwkv_g2[o]), w0=row1(rwkv_w0[o]), a0=row1(rwkv_a0[o]),
                     k_k=row1(rwkv_kk[o]), k_a=row1(rwkv_ka[o]))
            vparams = None if o == 0 else (row1(rwkv_v0[o - 1]), lora_in(rwkv_v1[o - 1]), lora_out(rwkv_v2[o - 1]))
            lnw, lnb, rk, wo = row1(rwkv_lnw[o]), row1(rwkv_lnb[o]), row1(rwkv_rk[o]), mx(rwkv_wo[o])
            nhd = d // HD

            xn = _norm(h_p, gn, tm_p)
            xn3 = xn.reshape(nb, t, d)
            xprev = jnp.concatenate([jnp.zeros((nb, 1, d), F32), xn3[:, :-1]], axis=1).reshape(mp, d)
            vres = None if o == 0 else (vf_p,) + vparams
            r, w, k, v, kk, b, g = _rwkv_proj(xn, xprev, p, vres, tm_a)
            if o == 0:
                vf_p = v
            seq = lambda a: a.reshape(nb, t, d)
            o_seq, s_fin = _wkv_scan(seq(r), seq(w), seq(k), seq(v), seq(kk), seq(b),
                                     jnp.zeros((nb, nhd, HD, HD), F32), WKV_CHUNK, WKV_HEADS)
            o_tok = o_seq.reshape(mp, d)
            h_p = _rwkv_out(h_p, o_tok, r, k, v, g, lnw, lnb, rk, wo, tm_a)
            outs['sh_p'].append(xn3[:, -1])
            outs['st_p'].append(s_fin)

            xn = _norm(h_s, gn, ns)
            vres = None if o == 0 else (vf_s,) + vparams
            r, w, k, v, kk, b, g = _rwkv_proj(xn, state_shift[o], p, vres, ns)
            if o == 0:
                vf_s = v
            rows = lambda a: a.reshape(ns, nhd, 1, HD)
            o_col, s_fin = _wkv_step(rows(r), rows(w), rows(k), rows(kk), rows(b), v.reshape(ns, nhd, HD, 1),
                                     state_wkv[o])
            h_s = _rwkv_out(h_s, o_col.reshape(ns, d), r, k, v, g, lnw, lnb, rk, wo, ns)
            outs['sh_s'].append(xn)
            outs['st_s'].append(s_fin)

        w1, w2 = mx(mlp_w1[i]), mx(mlp_w2[i])
        gm, gp, wg, wp = row1(norm_mlp[i]), row1(norm_ple[i]), mx(ple_gate[i]), mx(ple_proj[i])
        h_p = _mlp(h_p, gm, w1, w2, tm_p, 1024)
        h_p = _ple(h_p, p_prompt[i].reshape(mp, -1), gp, wg, wp, tm_p)
        h_s = _mlp(h_s, gm, w1, w2, ns, 1024)
        h_s = _ple(h_s, p_sample[i].reshape(ns, -1), gp, wg, wp, ns)

    st = lambda name: jnp.stack(outs[name])
    return (h_p.reshape(nb, t, d), h_s.reshape(ns, 1, d), st('nk_p'), st('nv_p'), st('nk_s'), st('nv_s'),
            st('wk_p'), st('wv_p'), st('wk_s'), st('wv_s'), st('pl_p'), st('pl_s'),
            st('sh_p'), st('sh_s'), st('st_p'), st('st_s'))
```

```python
import functools
import math

import numpy as np
import jax
import jax.numpy as jnp
from jax import lax
from jax.experimental import pallas as pl
from jax.experimental.pallas import tpu as pltpu

F32 = jnp.float32
_MX = jnp.bfloat16

HD = 64
POOL_DIM = 256
POOL_WINDOWS = (2, 4, 8, 16)
POOL_BUF = 15
KVH = 4
REP = 3
NSA_DIM = KVH * REP * HD
KV_DIM = KVH * HD
CMP_LEN, CMP_STRIDE = 32, 16
SLC_LEN, SLC_TOP = 64, 16
WINDOW = 512
QBLK = 256
KEY_TILE = 256
PAGE = 128
RMS_EPS = 1e-6
LN_X_EPS = 64e-5
NEG = -1e30
LOG2E = 1.4426950408889634
BIG = 1e9
WKV_CHUNK = 64
WKV_PASSES = (1, 2, 1)
WKV_HEADS = 16
LORA_PAD = 128
IN_PAD = 2688
VMEM_LIMIT = 56 * 1024 * 1024


def _cp(*sem):
    return pltpu.CompilerParams(dimension_semantics=sem, vmem_limit_bytes=VMEM_LIMIT)


def _alibi_slopes(n):
    p = 2 ** int(math.floor(math.log2(n)))
    s = [2.0 ** (-8.0 * (i + 1) / p) for i in range(p)]
    if p < n:
        s += [2.0 ** (-8.0 * (i + 1) / (2 * p)) for i in range(0, 2 * p, 2)][: n - p]
    return np.asarray(s, dtype=np.float32)


def _rms(x, g):
    return x * lax.rsqrt(jnp.mean(x * x, axis=-1, keepdims=True) + RMS_EPS) * g


def _sigmoid(x):
    return 1.0 / (1.0 + jnp.exp(-x))


def _dot(a, b):
    return jnp.dot(a.astype(_MX), b.astype(_MX), preferred_element_type=F32)


def _dot_nt(a, b):
    return lax.dot_general(a.astype(_MX), b.astype(_MX), (((1,), (1,)), ((), ())),
                           preferred_element_type=F32)


def _dot2(a, b):
    hi = a.astype(_MX)
    lo = (a - hi.astype(F32)).astype(_MX)
    b = b.astype(_MX)
    return (jnp.dot(hi, b, preferred_element_type=F32) + jnp.dot(lo, b, preferred_element_type=F32))


def _dot2r(a, b):
    hi = b.astype(_MX)
    lo = (b - hi.astype(F32)).astype(_MX)
    a = a.astype(_MX)
    return (jnp.dot(a, hi, preferred_element_type=F32) + jnp.dot(a, lo, preferred_element_type=F32))


def _seg64_sum(y):
    left = lax.broadcasted_iota(jnp.int32, y.shape, 1) < HD
    sa = jnp.sum(jnp.where(left, y, 0.0), axis=-1, keepdims=True)
    sb = jnp.sum(jnp.where(left, 0.0, y), axis=-1, keepdims=True)
    return jnp.where(left, sa, sb)


def _head_rms_tile(zt, gain2):
    ms = _seg64_sum(zt * zt) * (1.0 / HD)
    return zt * lax.rsqrt(ms + RMS_EPS) * gain2


def _masked_softmax(s, valid):
    s = jnp.where(valid, s, NEG)
    e = jnp.where(valid, jnp.exp(s - jnp.max(s, axis=-1, keepdims=True)), 0.0)
    return e / jnp.maximum(jnp.sum(e, axis=-1, keepdims=True), 1e-30)


def _online_update(s, valid, v, m_ref, l_ref, acc_ref):
    s = jnp.where(valid, s, NEG)
    m_old = m_ref[...]
    m_new = jnp.maximum(m_old, jnp.max(s, axis=-1, keepdims=True))
    p = jnp.where(valid, jnp.exp(s - m_new), 0.0)
    alpha = jnp.exp(m_old - m_new)
    l_ref[...] = alpha * l_ref[...] + jnp.sum(p, axis=-1, keepdims=True)
    acc_ref[...] = alpha * acc_ref[...] + _dot(p, v)
    m_ref[...] = m_new


def _topk_mask(score, n_cand, n_top):
    col_id = lax.broadcasted_iota(jnp.int32, score.shape, 1)
    rank = jnp.zeros(score.shape, F32)
    for j0 in range(n_cand):
        cj = score[:, j0:j0 + 1]
        beats = jnp.where(cj > score, 1.0, jnp.where((cj == score) & (col_id > j0), 1.0, 0.0))
        rank = rank + beats
    return jnp.where((rank < n_top) & (col_id < n_cand), 1.0, 0.0)


def _norm_kernel(x_ref, g_ref, o_ref):
    o_ref[...] = _rms(x_ref[...], g_ref[...])


def _norm(x, g, tm):
    m, d = x.shape
    return pl.pallas_call(
        _norm_kernel, grid=(m // tm,),
        in_specs=[pl.BlockSpec((tm, d), lambda i: (i, 0)), pl.BlockSpec((1, d), lambda i: (0, 0))],
        out_specs=pl.BlockSpec((tm, d), lambda i: (i, 0)),
        out_shape=jax.ShapeDtypeStruct((m, d), F32), compiler_params=_cp("parallel"))(x, g)


def _mlp_kernel(h_ref, g_ref, w1_ref, w2_ref, o_ref, xn_ref):
    @pl.when(pl.program_id(1) == 0)
    def _():
        x = h_ref[...]
        xn_ref[...] = _rms(x, g_ref[...]).astype(_MX)
        o_ref[...] = x

    a = jnp.dot(xn_ref[...], w1_ref[...], preferred_element_type=F32)
    a = jnp.square(jnp.maximum(a, 0.0))
    o_ref[...] += _dot(a, w2_ref[...])


def _mlp(h, g, w1, w2, tm, tf):
    m, d = h.shape
    dff = w1.shape[1]
    return pl.pallas_call(
        _mlp_kernel, grid=(m // tm, dff // tf),
        in_specs=[pl.BlockSpec((tm, d), lambda i, j: (i, 0)), pl.BlockSpec((1, d), lambda i, j: (0, 0)),
                  pl.BlockSpec((d, tf), lambda i, j: (0, j)), pl.BlockSpec((tf, d), lambda i, j: (j, 0))],
        out_specs=pl.BlockSpec((tm, d), lambda i, j: (i, 0)),
        out_shape=jax.ShapeDtypeStruct((m, d), F32),
        scratch_shapes=[pltpu.VMEM((tm, d), _MX)],
        compiler_params=_cp("parallel", "arbitrary"))(h, g, w1, w2)


def _ple_kernel(h_ref, p_ref, g_ref, wg_ref, wp_ref, o_ref):
    h = h_ref[...]
    gate = _sigmoid(_dot(_rms(h, g_ref[...]), wg_ref[...]))
    o_ref[...] = h + gate * _dot(p_ref[...], wp_ref[...])


def _ple(h, p, g, wg, wp, tm):
    m, d = h.shape
    pd = p.shape[1]
    return pl.pallas_call(
        _ple_kernel, grid=(m // tm,),
        in_specs=[pl.BlockSpec((tm, d), lambda i: (i, 0)), pl.BlockSpec((tm, pd), lambda i: (i, 0)),
                  pl.BlockSpec((1, d), lambda i: (0, 0)), pl.BlockSpec((d, d), lambda i: (0, 0)),
                  pl.BlockSpec((pd, d), lambda i: (0, 0))],
        out_specs=pl.BlockSpec((tm, d), lambda i: (i, 0)),
        out_shape=jax.ShapeDtypeStruct((m, d), F32), compiler_params=_cp("parallel"))(h, p, g, wg, wp)


_SEG_U = 0
_SEG_Q = POOL_DIM
_SEG_KV = POOL_DIM + NSA_DIM
_SEG_GL = _SEG_KV + 6 * KV_DIM


def _inproj_kernel(x_ref, gn_ref, w_ref, qg_ref, ksg_ref, kwg_ref,
                   u_ref, q_ref, nk_ref, nv_ref, kw_ref, vw_ref, gate_ref, z_ref, *, q_scale):
    xn = _rms(x_ref[...], gn_ref[...])
    z_ref[...] = _dot(xn, w_ref[...])
    u_ref[...] = z_ref[:, _SEG_U:_SEG_U + POOL_DIM]
    for c in range(NSA_DIM // 128):
        zt = z_ref[:, _SEG_Q + c * 128:_SEG_Q + (c + 1) * 128]
        q_ref[:, c * 128:(c + 1) * 128] = (_head_rms_tile(zt, qg_ref[...]) * q_scale).astype(q_ref.dtype)
    kv = _SEG_KV
    nk_ref[:, 0:KV_DIM] = z_ref[:, kv:kv + KV_DIM]
    nv_ref[:, 0:KV_DIM] = z_ref[:, kv + KV_DIM:kv + 2 * KV_DIM]
    nv_ref[:, KV_DIM:2 * KV_DIM] = z_ref[:, kv + 3 * KV_DIM:kv + 4 * KV_DIM]
    vw_ref[...] = z_ref[:, kv + 5 * KV_DIM:kv + 6 * KV_DIM]
    for c in range(KV_DIM // 128):
        zs = z_ref[:, kv + 2 * KV_DIM + c * 128:kv + 2 * KV_DIM + (c + 1) * 128]
        nk_ref[:, KV_DIM + c * 128:KV_DIM + (c + 1) * 128] = _head_rms_tile(zs, ksg_ref[...])
        zw = z_ref[:, kv + 4 * KV_DIM + c * 128:kv + 4 * KV_DIM + (c + 1) * 128]
        kw_ref[:, c * 128:(c + 1) * 128] = _head_rms_tile(zw, kwg_ref[...])
    gate_ref[...] = _sigmoid(z_ref[:, _SEG_GL:_SEG_GL + 128])


def _inproj(x, gn, w, qg, ksg, kwg, tm, q_scale):
    m, d = x.shape
    row = lambda i: (i, 0)
    fix = lambda i: (0, 0)
    outs = [(POOL_DIM, F32), (NSA_DIM, _MX), (2 * KV_DIM, F32), (2 * KV_DIM, F32), (KV_DIM, F32), (KV_DIM, F32),
            (128, F32)]
    return pl.pallas_call(
        functools.partial(_inproj_kernel, q_scale=q_scale), grid=(m // tm,),
        in_specs=[pl.BlockSpec((tm, d), row), pl.BlockSpec((1, d), fix), pl.BlockSpec((d, IN_PAD), fix),
                  pl.BlockSpec((1, 128), fix), pl.BlockSpec((1, 128), fix), pl.BlockSpec((1, 128), fix)],
        out_specs=[pl.BlockSpec((tm, c), row) for c, _ in outs],
        out_shape=[jax.ShapeDtypeStruct((m, c), dt) for c, dt in outs],
        scratch_shapes=[pltpu.VMEM((tm, IN_PAD), F32)],
        compiler_params=_cp("parallel"))(x, gn, w, qg, ksg, kwg)


def _pool_select(sums, u, cnt):
    grp = lax.broadcasted_iota(jnp.int32, u.shape, 1) >> 6
    ssel = jnp.where(grp == 0, sums[2], jnp.where(grp == 1, sums[4], jnp.where(grp == 2, sums[8], sums[16])))
    return ssel / cnt - u, grp


def _pool_kernel(u_ref, wbd_ref, sc_ref, y_ref, ext_ref, *, tm):
    i = pl.program_id(1)

    @pl.when(i == 0)
    def _():
        ext_ref[0:16, :] = jnp.zeros((16, POOL_DIM), F32)

    u = u_ref[...]
    ext_ref[16:16 + tm, :] = u
    acc = u
    sums = {}
    for s in range(1, 16):
        acc = acc + ext_ref[16 - s:16 - s + tm, :]
        if s + 1 in POOL_WINDOWS:
            sums[s + 1] = acc
    grp = lax.broadcasted_iota(jnp.int32, u.shape, 1) >> 6
    win = jnp.where(grp == 0, 2, jnp.where(grp == 1, 4, jnp.where(grp == 2, 8, 16)))
    pos = i * tm + lax.broadcasted_iota(jnp.int32, u.shape, 0)
    cnt = jnp.minimum(win, pos + 1).astype(F32)
    mixed, _ = _pool_select(sums, u, cnt)
    y_ref[...] = _dot(mixed, wbd_ref[...]) * sc_ref[...]
    ext_ref[0:16, :] = ext_ref[tm:tm + 16, :]


def _pool_prompt(u, wbd, scale, tm):
    n, t, c = u.shape
    return pl.pallas_call(
        functools.partial(_pool_kernel, tm=tm), grid=(n, t // tm),
        in_specs=[pl.BlockSpec((None, tm, c), lambda b, i: (b, i, 0)), pl.BlockSpec((c, c), lambda b, i: (0, 0)),
                  pl.BlockSpec((1, c), lambda b, i: (0, 0))],
        out_specs=pl.BlockSpec((None, tm, c), lambda b, i: (b, i, 0)),
        out_shape=jax.ShapeDtypeStruct((n, t, c), F32),
        scratch_shapes=[pltpu.VMEM((tm + 16, c), F32)],
        compiler_params=_cp("parallel", "arbitrary"))(u, wbd, scale)


def _pool_step_kernel(buf_ref, u_ref, wbd_ref, sc_ref, y_ref, *, pos0):
    u = u_ref[...]
    acc = u
    sums = {}
    for s in range(1, 16):
        acc = acc + buf_ref[:, POOL_BUF - s, :]
        if s + 1 in POOL_WINDOWS:
            sums[s + 1] = acc
    grp = lax.broadcasted_iota(jnp.int32, u.shape, 1) >> 6
    win = jnp.where(grp == 0, 2, jnp.where(grp == 1, 4, jnp.where(grp == 2, 8, 16)))
    cnt = jnp.minimum(win, pos0 + 1).astype(F32)
    mixed, _ = _pool_select(sums, u, cnt)
    y_ref[...] = _dot(mixed, wbd_ref[...]) * sc_ref[...]


def _pool_step(buf, u, wbd, scale, pos0):
    n, c = u.shape
    full = lambda *shape: pl.BlockSpec(shape, lambda i: (0,) * len(shape))
    return pl.pallas_call(
        functools.partial(_pool_step_kernel, pos0=pos0), grid=(1,),
        in_specs=[full(n, POOL_BUF, c), full(n, c), full(c, c), full(1, c)],
        out_specs=full(n, c), out_shape=jax.ShapeDtypeStruct((n, c), F32),
        compiler_params=_cp("arbitrary"))(buf, u, wbd, scale)


def _pages_per_step(n_pages, most=8):
    return next(p for p in (32, 16, 8, 4, 2, 1) if p <= most and n_pages % p == 0)


def _compress_kernel(pt_ref, *refs, n_pages, pps):
    xk_refs, xv_refs = refs[:pps], refs[pps:2 * pps]
    pek_ref, pev_ref, wk_ref, wv_ref, kcg_ref, kc_ref, vc_ref, xs_k, xs_v, bsh = refs[2 * pps:]
    j = pl.program_id(1)
    n_tiles = KV_DIM // 128
    for p in range(pps):
        row0 = pl.multiple_of((j * pps + p) * PAGE, PAGE)
        for c in range(n_tiles):
            xs_k[c, pl.ds(row0, PAGE), :] = xk_refs[p][:, c * 128:(c + 1) * 128]
            xs_v[c, pl.ds(row0, PAGE), :] = xv_refs[p][:, c * 128:(c + 1) * 128]

    @pl.when(j == n_pages // pps - 1)
    def _():
        nh = n_pages * (PAGE // CMP_STRIDE)

        def run(xs, c, pe_ref, w_ref):
            a = jnp.zeros((nh, 128), F32)
            b = jnp.zeros((nh, 128), F32)
            for l in range(CMP_STRIDE):
                xl = xs[c, pl.ds(l, nh, stride=CMP_STRIDE), :]
                a = a + _dot(xl + pe_ref[l:l + 1, :], w_ref[l])
                b = b + _dot(xl + pe_ref[CMP_STRIDE + l:CMP_STRIDE + l + 1, :], w_ref[CMP_STRIDE + l])
            bsh[0:nh, :] = b
            bsh[nh:nh + 8, :] = jnp.zeros((8, 128), F32)
            return a + bsh[1:nh + 1, :]

        for c in range(n_tiles):
            kc_ref[:, c * 128:(c + 1) * 128] = _head_rms_tile(run(xs_k, c, pek_ref, wk_ref), kcg_ref[...])
            vc_ref[:, c * 128:(c + 1) * 128] = run(xs_v, c, pev_ref, wv_ref)


def _compress(cache_k, cache_v, page_table, e, pek, pev, wk, wv, kcg):
    nb, n_pages = page_table.shape
    nh = n_pages * (PAGE // CMP_STRIDE)
    pps = _pages_per_step(n_pages, most=32)
    page = lambda p: pl.BlockSpec((None, None, PAGE, KV_DIM), lambda b, j, pt: (e, pt[b, j * pps + p], 0, 0))
    fix2 = lambda b, j, pt: (0, 0)
    fix3 = lambda b, j, pt: (0, 0, 0)
    out = pl.BlockSpec((None, nh, KV_DIM), lambda b, j, pt: (b, 0, 0))
    gs = pltpu.PrefetchScalarGridSpec(
        num_scalar_prefetch=1, grid=(nb, n_pages // pps),
        in_specs=[page(p) for p in range(pps)] * 2
                 + [pl.BlockSpec((CMP_LEN, 128), fix2), pl.BlockSpec((CMP_LEN, 128), fix2),
                    pl.BlockSpec((CMP_LEN, 128, 128), fix3), pl.BlockSpec((CMP_LEN, 128, 128), fix3),
                    pl.BlockSpec((1, 128), fix2)],
        out_specs=[out, out],
        scratch_shapes=[pltpu.VMEM((KV_DIM // 128, n_pages * PAGE, 128), F32),
                        pltpu.VMEM((KV_DIM // 128, n_pages * PAGE, 128), F32),
                        pltpu.VMEM((nh + 8, 128), F32)])
    return pl.pallas_call(
        functools.partial(_compress_kernel, n_pages=n_pages, pps=pps), grid_spec=gs,
        out_shape=[jax.ShapeDtypeStruct((nb, nh, KV_DIM), F32)] * 2,
        compiler_params=_cp("parallel", "arbitrary"),
    )(page_table, *([cache_k] * pps), *([cache_v] * pps), pek, pev, wk, wv, kcg)


def _topk_rows(score, n_top):
    rid = lax.broadcasted_iota(jnp.int32, score.shape, 0).astype(F32)
    sel = jnp.zeros(score.shape, F32)
    for _ in range(n_top):
        m = jnp.max(score, axis=0, keepdims=True)
        first = jnp.min(jnp.where(score == m, rid, 1e9), axis=0, keepdims=True)
        hit = rid == first
        sel = jnp.where(hit, 1.0, sel)
        score = jnp.where(hit, -jnp.inf, score)
    return sel


def _nsa_prompt_kernel(sl_ref, q_ref, kc_ref, vct_ref, ks_ref, vst_ref, kw_ref, vwt_ref, gt_ref, mimpt_ref,
                       o_ref, sel_ref, s_ref, acc_ref, p_ref, bias_ref, mask_ref, *, n_cmp, n_slc, n_top):
    g = pl.program_id(1)
    i = pl.program_id(2)
    nh = kc_ref.shape[0]
    q = q_ref[...].reshape(REP * QBLK, HD)
    qpos0 = i * QBLK
    pos = qpos0 + lax.broadcasted_iota(jnp.int32, (1, QBLK), 1)
    slope = [sl_ref[g * REP + r] * LOG2E for r in range(REP)]

    cid = lax.broadcasted_iota(jnp.int32, (nh, QBLK), 0)
    dist_c = pos - (cid * CMP_STRIDE + (CMP_LEN - 1))
    valid_c = (dist_c >= 0) & (cid < n_cmp)
    dist_cf = dist_c.astype(F32)
    s_c = _dot_nt(kc_ref[...], q)
    vct = vct_ref[...]
    imp = jnp.zeros((nh, QBLK), F32)
    o_cmp = []
    for r in range(REP):
        s = jnp.where(valid_c, s_c[:, r * QBLK:(r + 1) * QBLK] - slope[r] * dist_cf, NEG)
        e = jnp.where(valid_c, jnp.exp2(s - jnp.max(s, axis=0, keepdims=True)), 0.0)
        p = e / jnp.maximum(jnp.sum(e, axis=0, keepdims=True), 1e-30)
        imp = imp + p
        o_cmp.append(_dot(vct, p))
    imp_slc = _dot2r(mimpt_ref[...], imp)
    blk = lax.broadcasted_iota(jnp.int32, imp_slc.shape, 0)
    cur = pos >> 6
    forced = (blk == 0) | (blk == cur) | (blk == cur - 1)
    causal = blk * SLC_LEN <= pos
    score = jnp.where(forced, BIG, jnp.where(causal, imp_slc, -BIG))
    score = jnp.where(blk < n_slc, score, -jnp.inf)
    sel_ref[...] = _topk_rows(score, n_top)

    d0 = (lax.broadcasted_iota(jnp.int32, (KEY_TILE, QBLK), 1)
          - lax.broadcasted_iota(jnp.int32, (KEY_TILE, QBLK), 0))
    d0f = d0.astype(F32)
    for r in range(REP):
        bias_ref[r] = -slope[r] * d0f
    mask_ref[0] = jnp.where(d0 >= 0, 0.0, NEG)
    mask_ref[1] = jnp.where(d0 < 0, 0.0, NEG)
    hi = (qpos0 + QBLK - 1) // KEY_TILE

    def sweep(lo, k_ref, vt_ref, mask_fn):
        def scores(c):
            k0 = pl.multiple_of(c * KEY_TILE, KEY_TILE)
            return _dot_nt(k_ref[pl.ds(k0, KEY_TILE), :], q)

        def body(c, carry):
            m, l, p_prev = carry
            pv = _dot(vt_ref[jnp.maximum(c - 1, lo)], p_prev)
            s_next = scores(jnp.minimum(c + 1, hi))
            off = qpos0 - c * KEY_TILE
            madd = mask_fn(c, off)
            off_f = off.astype(F32)
            cur = c % 2
            s, shift = [], []
            for r in range(REP):
                s_r = s_ref[cur, :, r * QBLK:(r + 1) * QBLK] + bias_ref[r] + madd
                s.append(s_r)
                shift.append(jnp.max(s_r, axis=0, keepdims=True) - slope[r] * off_f)
            m_new = jnp.maximum(m, jnp.concatenate(shift, axis=1))
            p = jnp.concatenate([jnp.exp2(s[r] - (m_new[:, r * QBLK:(r + 1) * QBLK] + slope[r] * off_f))
                                 for r in range(REP)], axis=1)
            alpha = jnp.exp2(m - m_new)
            l = alpha * l + jnp.sum(p, axis=0, keepdims=True)
            acc_ref[...] = alpha * (acc_ref[...] + pv)
            s_ref[(c + 1) % 2] = s_next
            return m_new, l, p.astype(_MX)

        s_ref[lo % 2] = scores(lo)
        acc_ref[...] = jnp.zeros(acc_ref.shape, F32)
        p_ref[...] = jnp.zeros(p_ref.shape, p_ref.dtype)
        init = (jnp.full((1, REP * QBLK), NEG, F32), jnp.zeros((1, REP * QBLK), F32), p_ref[...])
        _, l, p_last = lax.fori_loop(lo, hi + 1, body, init)
        return (acc_ref[...] + _dot(vt_ref[hi], p_last)) / l

    def slc_mask(c, off):
        per_tile = KEY_TILE // SLC_LEN
        rows = [jnp.broadcast_to(sel_ref[pl.ds(per_tile * c + b, 1), :], (SLC_LEN, QBLK)) for b in range(per_tile)]
        picked = jnp.where(jnp.concatenate(rows, axis=0) > 0.5, 0.0, NEG)
        return picked + jnp.where(off == 0, mask_ref[0], 0.0)

    def win_mask(c, off):
        return jnp.where(off == 0, mask_ref[0], jnp.where(off == WINDOW, mask_ref[1], 0.0))

    o_slc = sweep(0, ks_ref, vst_ref, slc_mask)
    lo_win = jnp.maximum(qpos0 - WINDOW, 0) // KEY_TILE
    o_win = sweep(lo_win, kw_ref, vwt_ref, win_mask)

    gates = gt_ref[...]
    for r in range(REP):
        sl = slice(r * QBLK, (r + 1) * QBLK)
        o_ref[r * HD:(r + 1) * HD, :] = (gates[3 * r:3 * r + 1, :] * o_cmp[r] + gates[3 * r + 1:3 * r + 2, :] * o_slc[:, sl]
                                         + gates[3 * r + 2:3 * r + 3, :] * o_win[:, sl])
    o_ref[REP * HD:, :] = jnp.zeros((o_ref.shape[0] - REP * HD, QBLK), F32)


def _nsa_prompt(slopes, q_h, kc_h, vc_t, ks_h, vs_t, kw_h, vw_t, gates_t, mimp_t, n_cmp):
    n, _, t, _ = q_h.shape
    assert t % KEY_TILE == 0 and QBLK == KEY_TILE and WINDOW % KEY_TILE == 0
    nh = kc_h.shape[2]
    n_slc = t // SLC_LEN
    nq = t // QBLK
    nkt = t // KEY_TILE
    seq = lambda *shape: pl.BlockSpec((None, None) + shape, lambda b, g, i: (b, g) + (0,) * len(shape))
    return pl.pallas_call(
        functools.partial(_nsa_prompt_kernel, n_cmp=n_cmp, n_slc=n_slc, n_top=min(SLC_TOP, n_slc)),
        grid=(n, KVH, nq),
        in_specs=[pl.BlockSpec(memory_space=pltpu.SMEM),
                  pl.BlockSpec((None, REP, QBLK, HD), lambda b, g, i: (b, g, i, 0)),
                  seq(nh, HD), seq(HD, nh), seq(t, HD), seq(nkt, HD, KEY_TILE), seq(t, HD), seq(nkt, HD, KEY_TILE),
                  pl.BlockSpec((None, None, 9, QBLK), lambda b, g, i: (b, g, 0, i)),
                  pl.BlockSpec(mimp_t.shape, lambda b, g, i: (0, 0))],
        out_specs=pl.BlockSpec((None, None, 256, QBLK), lambda b, g, i: (g, b, 0, i)),
        out_shape=jax.ShapeDtypeStruct((KVH, n, 256, t), F32),
        scratch_shapes=[pltpu.VMEM((mimp_t.shape[0], QBLK), F32), pltpu.VMEM((2, KEY_TILE, REP * QBLK), F32),
                        pltpu.VMEM((HD, REP * QBLK), F32), pltpu.VMEM((KEY_TILE, REP * QBLK), _MX),
                        pltpu.VMEM((REP, KEY_TILE, QBLK), F32), pltpu.VMEM((2, KEY_TILE, QBLK), F32)],
        compiler_params=_cp("parallel", "parallel", "arbitrary"),
    )(slopes, q_h, kc_h, vc_t, ks_h, vs_t, kw_h, vw_t, gates_t, mimp_t)


def _nsa_sample_kernel(pt_ref, *refs, n_pages, pps, n_cmp, n_slc, n_top, wbuf):
    q_ref, kc_ref, vc_ref = refs[:3]
    ck_refs, cv_refs = refs[3:3 + pps], refs[3 + pps:3 + 2 * pps]
    (new_ref, wk_ref, wv_ref, g_ref, sl_ref, grp_ref, mimp_ref, o_ref, m_ref, l_ref, acc_ref, sel_ref, ocmp_ref,
     ofull_ref) = refs[3 + 2 * pps:]
    j = pl.program_id(1)
    span = pps * PAGE
    pos0 = n_pages * PAGE
    q = q_ref[...]
    slope = sl_ref[:, 0:1]
    nrow = q.shape[0]

    @pl.when(j == 0)
    def _():
        nh = kc_ref.shape[0]
        cid = lax.broadcasted_iota(jnp.int32, (nrow, nh), 1)
        dist = pos0 - (cid * CMP_STRIDE + (CMP_LEN - 1))
        valid = (dist >= 0) & (cid < n_cmp)
        p = _masked_softmax(_dot_nt(q, kc_ref[...]) - slope * dist.astype(F32), valid)
        ocmp_ref[...] = _dot(p, vc_ref[...])
        imp = _dot2(_dot2r(grp_ref[...], p), mimp_ref[...])
        blk = lax.broadcasted_iota(jnp.int32, imp.shape, 1)
        cur = pos0 // SLC_LEN
        forced = (blk == 0) | (blk == cur) | (blk == cur - 1)
        causal = blk * SLC_LEN <= pos0
        score = jnp.where(forced, BIG, jnp.where(causal, imp, -BIG))
        sel_ref[...] = _topk_mask(score, n_slc, n_top)
        m_ref[...] = jnp.full(m_ref.shape, NEG, F32)
        l_ref[...] = jnp.zeros(l_ref.shape, F32)
        acc_ref[...] = jnp.zeros(acc_ref.shape, F32)

    sel = sel_ref[...]
    e_row = lax.broadcasted_iota(jnp.int32, (sel.shape[1], span), 0)
    e_col = lax.broadcasted_iota(jnp.int32, (sel.shape[1], span), 1) >> 6
    picked = _dot(sel, jnp.where(e_row == (span // SLC_LEN) * j + e_col, 1.0, 0.0)) > 0.5
    kpos = j * span + lax.broadcasted_iota(jnp.int32, (nrow, span), 1)
    dist = pos0 - kpos
    keys = jnp.concatenate([r[...].astype(_MX) for r in ck_refs], axis=0)
    vals = jnp.concatenate([r[...].astype(_MX) for r in cv_refs], axis=0)
    s = _dot_nt(q, keys) - slope * dist.astype(F32)
    _online_update(s, picked & (dist >= 0), vals, m_ref, l_ref, acc_ref)

    @pl.when(j == n_pages // pps - 1)
    def _():
        qf = q.astype(F32)
        new = new_ref[...]
        s_new = jnp.sum(qf * new[0:1, :], axis=-1, keepdims=True)
        ok = sel[:, n_slc - 1:n_slc] > 0.5
        s_new = jnp.where(ok, s_new, NEG)
        m_old = m_ref[...]
        m_new = jnp.maximum(m_old, s_new)
        p_new = jnp.where(ok, jnp.exp(s_new - m_new), 0.0)
        alpha = jnp.exp(m_old - m_new)
        l_tot = alpha * l_ref[...] + p_new
        acc = alpha * acc_ref[...] + p_new * new[1:2, :]
        o_slc = acc / jnp.maximum(l_tot, 1e-30)

        idx = lax.broadcasted_iota(jnp.int32, (nrow, wbuf), 1)
        dw = wbuf - idx
        valid_w = (dw >= 0) & (dw < WINDOW)
        s_w = jnp.where(valid_w, _dot_nt(q, wk_ref[...]) - slope * dw.astype(F32), NEG)
        s_wn = jnp.sum(qf * new[2:3, :], axis=-1, keepdims=True)
        m_w = jnp.maximum(jnp.max(s_w, axis=-1, keepdims=True), s_wn)
        p_w = jnp.where(valid_w, jnp.exp(s_w - m_w), 0.0)
        p_wn = jnp.exp(s_wn - m_w)
        den = jnp.maximum(jnp.sum(p_w, axis=-1, keepdims=True) + p_wn, 1e-30)
        o_win = (_dot(p_w, wv_ref[...]) + p_wn * new[3:4, :]) / den

        gates = g_ref[...]
        ofull_ref[...] = gates[:, 0:1] * ocmp_ref[...] + gates[:, 1:2] * o_slc + gates[:, 2:3] * o_win
        row_g = lax.broadcasted_iota(jnp.int32, (nrow, HD), 0) >> 2
        o = jnp.zeros((nrow, HD), F32)
        for gg in range(KVH):
            o = o + jnp.where(row_g == gg, ofull_ref[:, gg * HD:(gg + 1) * HD], 0.0)
        o_ref[...] = o


def _nsa_sample(page_table, e, qbd, kc, vc, cache_k, cache_v, new_rows, win_k, win_v, gates16, slopes16, grp16,
                mimp, n_cmp, n_slc):
    nb, n_pages = page_table.shape
    nh = kc.shape[1]
    wbuf = win_k.shape[2]
    per_b = lambda *shape: pl.BlockSpec((None,) + shape, lambda b, j, pt: (b,) + (0,) * len(shape))
    fix = lambda *shape: pl.BlockSpec(shape, lambda b, j, pt: (0,) * len(shape))
    pps = _pages_per_step(n_pages)
    page = lambda p: pl.BlockSpec((None, None, PAGE, KV_DIM), lambda b, j, pt: (e, pt[b, j * pps + p], 0, 1))
    win = pl.BlockSpec((None, None, wbuf, KV_DIM), lambda b, j, pt: (e, b, 0, 0))
    nrow = qbd.shape[1]
    gs = pltpu.PrefetchScalarGridSpec(
        num_scalar_prefetch=1, grid=(nb, n_pages // pps),
        in_specs=[per_b(nrow, KV_DIM), per_b(nh, KV_DIM), per_b(nh, KV_DIM)] + [page(p) for p in range(pps)] * 2
                 + [per_b(8, KV_DIM), win, win, per_b(nrow, 128), fix(nrow, 128), fix(nrow, nrow), fix(*mimp.shape)],
        out_specs=per_b(nrow, HD),
        scratch_shapes=[pltpu.VMEM((nrow, 1), F32), pltpu.VMEM((nrow, 1), F32), pltpu.VMEM((nrow, KV_DIM), F32),
                        pltpu.VMEM((nrow, mimp.shape[1]), F32), pltpu.VMEM((nrow, KV_DIM), F32),
                        pltpu.VMEM((nrow, KV_DIM), F32)])
    return pl.pallas_call(
        functools.partial(_nsa_sample_kernel, n_pages=n_pages, pps=pps, n_cmp=n_cmp, n_slc=n_slc,
                          n_top=min(SLC_TOP, n_slc), wbuf=wbuf),
        grid_spec=gs, out_shape=jax.ShapeDtypeStruct((nb, nrow, HD), F32),
        compiler_params=_cp("parallel", "arbitrary"),
    )(page_table, qbd, kc, vc, *([cache_k] * pps), *([cache_v] * pps), new_rows, win_k, win_v, gates16, slopes16,
      grp16, mimp)


def _outproj_kernel(h_ref, y_ref, o_ref, wp_ref, wn_ref, out_ref):
    acc = h_ref[...] + _dot(y_ref[...], wp_ref[...])
    for g in range(KVH):
        acc = acc + _dot(o_ref[g], wn_ref[g])
    out_ref[...] = acc


def _outproj(h, y_pool, o4, wp, wn, tm):
    m, d = h.shape
    return pl.pallas_call(
        _outproj_kernel, grid=(m // tm,),
        in_specs=[pl.BlockSpec((tm, d), lambda i: (i, 0)), pl.BlockSpec((tm, POOL_DIM), lambda i: (i, 0)),
                  pl.BlockSpec((KVH, tm, 256), lambda i: (0, i, 0)), pl.BlockSpec((POOL_DIM, d), lambda i: (0, 0)),
                  pl.BlockSpec((KVH, 256, d), lambda i: (0, 0, 0))],
        out_specs=pl.BlockSpec((tm, d), lambda i: (i, 0)),
        out_shape=jax.ShapeDtypeStruct((m, d), F32), compiler_params=_cp("parallel"))(h, y_pool, o4, wp, wn)


def _rwkv_proj_kernel(*refs, has_vres):
    if has_vres:
        (xn_ref, xp_ref, mu_ref, wr_ref, wk_ref, wv_ref, w1_ref, w2_ref, a1_ref, a2_ref, g1_ref, g2_ref,
         w0_ref, a0_ref, kk_ref, ka_ref, vf_ref, v0_ref, v1_ref, v2_ref,
         r_out, w_out, k_out, v_out, kk_out, b_out, g_out) = refs
    else:
        (xn_ref, xp_ref, mu_ref, wr_ref, wk_ref, wv_ref, w1_ref, w2_ref, a1_ref, a2_ref, g1_ref, g2_ref,
         w0_ref, a0_ref, kk_ref, ka_ref,
         r_out, w_out, k_out, v_out, kk_out, b_out, g_out) = refs
    xn = xn_ref[...]
    xx = xp_ref[...] - xn
    mix = lambda j: xn + xx * mu_ref[j:j + 1, :]
    xr, xw, xk, xv, xa, xg = [mix(j) for j in range(6)]
    r_out[...] = _dot(xr, wr_ref[...])
    k = _dot(xk, wk_ref[...])
    v = _dot(xv, wv_ref[...])
    z = w0_ref[...] + _dot(jnp.tanh(_dot(xw, w1_ref[...])), w2_ref[...])
    w_log = -(jnp.maximum(-z, 0.0) + jnp.log(1.0 + jnp.exp(-jnp.abs(z)))) - 0.5
    w_out[...] = -jnp.exp(w_log)
    if has_vres:
        v = v + (vf_ref[...] - v) * _sigmoid(v0_ref[...] + _dot(_dot(xv, v1_ref[...]), v2_ref[...]))
    v_out[...] = v
    a = _sigmoid(a0_ref[...] + _dot(_dot(xa, a1_ref[...]), a2_ref[...]))
    g_out[...] = _dot(_sigmoid(_dot(xg, g1_ref[...])), g2_ref[...])
    kk = k * kk_ref[...]
    for c in range(kk.shape[1] // 128):
        sl = slice(c * 128, (c + 1) * 128)
        kt = kk[:, sl]
        kn = kt / jnp.maximum(jnp.sqrt(_seg64_sum(kt * kt)), 1e-12)
        kk_out[:, sl] = kn
        b_out[:, sl] = kn * a[:, sl]
    k_out[...] = k * (1.0 + (a - 1.0) * ka_ref[...])


def _rwkv_proj(xn, xprev, p, vres, tm):
    m, d = xn.shape
    row = pl.BlockSpec((tm, d), lambda i: (i, 0))
    fix = lambda a: pl.BlockSpec(a.shape, lambda i: (0,) * a.ndim)
    args = [xn, xprev, p['mu'], p['wr'], p['wk'], p['wv'], p['w1'], p['w2'], p['a1'], p['a2'], p['g1'], p['g2'],
            p['w0'], p['a0'], p['k_k'], p['k_a']]
    specs = [row, row] + [fix(a) for a in args[2:]]
    if vres is not None:
        vf, v0, v1, v2 = vres
        args += [vf, v0, v1, v2]
        specs += [row, fix(v0), fix(v1), fix(v2)]
    return pl.pallas_call(
        functools.partial(_rwkv_proj_kernel, has_vres=vres is not None), grid=(m // tm,),
        in_specs=specs, out_specs=[row] * 7, out_shape=[jax.ShapeDtypeStruct((m, d), F32)] * 7,
        compiler_params=_cp("parallel"))(*args)


_NN = (((1,), (0,)), ((), ()))
_NT = (((1,), (1,)), ((), ()))
_TN = (((0,), (0,)), ((), ()))


def _split(a):
    hi = a.astype(_MX)
    return hi, (a - hi.astype(F32)).astype(_MX)


def _dot3(a, b, dims=_NN, passes=3):
    dg = lambda x, y: lax.dot_general(x, y, dims, preferred_element_type=F32)
    a_hi, a_lo = _split(a)
    if passes == 1:
        return dg(a_hi, b.astype(_MX))
    b_hi, b_lo = _split(b)
    out = dg(a_hi, b_hi) + dg(a_hi, b_lo)
    return out if passes == 2 else out + dg(a_lo, b_hi)


def _wkv_kernel(r_ref, lw_ref, k_ref, v_ref, kk_ref, b_ref, s0_ref, o_ref, s_ref, *, hb):
    @pl.when(pl.program_id(2) == 0)
    def _():
        s_ref[...] = s0_ref[...]

    c = r_ref.shape[0]
    hd = lambda ref, h: ref[:, h * HD:(h + 1) * HD]
    row = lax.broadcasted_iota(jnp.int32, (c, c), 0)
    col = lax.broadcasted_iota(jnp.int32, (c, c), 1)
    incl = col <= row
    strict = col < row
    ltri = jnp.where(incl, 1.0, 0.0)
    levels = int(math.log2(c))
    heads = range(hb)
    lw = [hd(lw_ref, h) for h in heads]
    v = [hd(v_ref, h) for h in heads]
    p_in, p_solve, p_out = WKV_PASSES
    cum = [_dot3(ltri, lw[h], passes=2) for h in heads]
    g_in = [jnp.exp(cum[h]) for h in heads]
    g_inv = [jnp.exp(-cum[h]) for h in heads]
    b_t = [hd(b_ref, h) * g_inv[h] for h in heads]
    k_t = [hd(k_ref, h) * g_inv[h] for h in heads]
    ar = [jnp.concatenate([-hd(kk_ref, h) * jnp.exp(cum[h] - lw[h]), hd(r_ref, h) * g_in[h]], axis=0) for h in heads]
    pb = [_dot3(ar[h], b_t[h], _NT, p_in) for h in heads]
    pk = [_dot3(ar[h], k_t[h], _NT, p_in) for h in heads]
    xs = [_dot3(ar[h], s_ref[h], _NT, p_in) for h in heads]
    n_mat = [jnp.where(strict, pb[h][:c], 0.0) for h in heads]
    x = [xs[h][:c] + _dot3(jnp.where(strict, pk[h][:c], 0.0), v[h], _NN, p_in) for h in heads]
    for lvl in range(levels):
        x = [x[h] + _dot3(n_mat[h], x[h], _NN, p_solve) for h in heads]
        if lvl + 1 < levels:
            n_mat = [_dot3(n_mat[h], n_mat[h], _NN, p_solve) for h in heads]
    for h in heads:
        o_ref[:, h * HD:(h + 1) * HD] = (xs[h][c:] + _dot3(jnp.where(incl, pb[h][c:], 0.0), x[h], _NN, p_out)
                                         + _dot3(jnp.where(incl, pk[h][c:], 0.0), v[h], _NN, p_out))
    for h in heads:
        s_ref[h] = ((s_ref[h] + _dot3(x[h], b_t[h], _TN, p_out) + _dot3(v[h], k_t[h], _TN, p_out))
                    * g_in[h][c - 1:c, :])


def _wkv_scan(r, lw, k, v, kk, b, s0, tc, hb):
    n, t, d = r.shape
    nh_ = d // HD
    tok = pl.BlockSpec((None, tc, hb * HD), lambda bb, hh, tt: (bb, tt, hh))
    st = pl.BlockSpec((None, hb, HD, HD), lambda bb, hh, tt: (bb, hh, 0, 0))
    return pl.pallas_call(
        functools.partial(_wkv_kernel, hb=hb), grid=(n, nh_ // hb, t // tc),
        in_specs=[tok] * 6 + [st], out_specs=[tok, st],
        out_shape=[jax.ShapeDtypeStruct((n, t, d), F32), jax.ShapeDtypeStruct((n, nh_, HD, HD), F32)],
        compiler_params=_cp("parallel", "parallel", "arbitrary"))(r, lw, k, v, kk, b, s0)


def _wkv_step_kernel(r_ref, lw_ref, k_ref, kk_ref, b_ref, vc_ref, s0_ref, o_ref, s_ref):
    s = s0_ref[...]
    sa = -jnp.sum(s * kk_ref[...], axis=-1, keepdims=True)
    s = s * jnp.exp(lw_ref[...]) + sa * b_ref[...] + vc_ref[...] * k_ref[...]
    s_ref[...] = s
    o_ref[...] = jnp.sum(s * r_ref[...], axis=-1, keepdims=True)


def _wkv_step(r, w, k, kk, b, v_col, s0):
    n, nh_ = r.shape[:2]
    rowv = pl.BlockSpec((None, nh_, 1, HD), lambda i: (i, 0, 0, 0))
    colv = pl.BlockSpec((None, nh_, HD, 1), lambda i: (i, 0, 0, 0))
    st = pl.BlockSpec((None, nh_, HD, HD), lambda i: (i, 0, 0, 0))
    return pl.pallas_call(
        _wkv_step_kernel, grid=(n,), in_specs=[rowv] * 5 + [colv, st], out_specs=[colv, st],
        out_shape=[jax.ShapeDtypeStruct((n, nh_, HD, 1), F32), jax.ShapeDtypeStruct((n, nh_, HD, HD), F32)],
        compiler_params=_cp("parallel"))(r, w, k, kk, b, v_col, s0)


def _rwkv_out_kernel(h_ref, o_ref, r_ref, k_ref, v_ref, g_ref, lnw_ref, lnb_ref, rk_ref, wo_ref, out_ref, y_ref):
    for c in range(h_ref.shape[1] // 128):
        sl = slice(c * 128, (c + 1) * 128)
        o = o_ref[:, sl]
        dlt = o - _seg64_sum(o) * (1.0 / HD)
        var = _seg64_sum(dlt * dlt) * (1.0 / HD)
        on = dlt * lax.rsqrt(var + LN_X_EPS) * lnw_ref[:, sl] + lnb_ref[:, sl]
        bonus = _seg64_sum(r_ref[:, sl] * k_ref[:, sl] * rk_ref[:, sl])
        y_ref[:, sl] = ((on + bonus * v_ref[:, sl]) * g_ref[:, sl]).astype(y_ref.dtype)
    out_ref[...] = h_ref[...] + jnp.dot(y_ref[...], wo_ref[...], preferred_element_type=F32)


def _rwkv_out(h, o, r, k, v, g, lnw, lnb, rk, wo, tm):
    m, d = h.shape
    row = pl.BlockSpec((tm, d), lambda i: (i, 0))
    vec = pl.BlockSpec((1, d), lambda i: (0, 0))
    return pl.pallas_call(
        _rwkv_out_kernel, grid=(m // tm,),
        in_specs=[row] * 6 + [vec] * 3 + [pl.BlockSpec((d, d), lambda i: (0, 0))],
        out_specs=row, out_shape=jax.ShapeDtypeStruct((m, d), F32),
        scratch_shapes=[pltpu.VMEM((tm, d), _MX)],
        compiler_params=_cp("parallel"))(h, o, r, k, v, g, lnw, lnb, rk, wo)


def _row_tile(m, pref):
    return pref if m % pref == 0 else m


def _block_diag(w):
    g, a, b = w.shape
    eye = jnp.eye(g, dtype=w.dtype)
    return (eye[:, None, :, None] * w[:, :, None, :]).reshape(g * a, g * b)


def _imp_matrix(nh, n_cmp, n_slc, width):
    per = SLC_LEN // CMP_STRIDE
    lead = CMP_LEN // CMP_STRIDE - 1
    c = np.arange(nh)[:, None]
    j = np.arange(width)[None, :]
    m = (c - per * j >= -lead) & (c - per * j < per) & (c < n_cmp) & (j < n_slc)
    return jnp.asarray(m.astype(np.float32), dtype=_MX)


def _pad_lanes(n):
    return -(-n // 128) * 128


def _heads(x, n, t):
    return x.reshape(n, t, -1, HD).transpose(0, 2, 1, 3).astype(_MX)


def kernel(x_prompt, x_sample, cache_k, cache_v, page_table, state_win_k, state_win_v, state_pool, state_shift, state_wkv, p_prompt, p_sample, norm_mix, norm_mlp, norm_ple, mlp_w1, mlp_w2, ple_proj, ple_gate, even_w_in, even_w_out, pool_w, pool_scale, q_gain, k_gain, cmp_pe, cmp_w, rwkv_mu, rwkv_wr, rwkv_wk, rwkv_wv, rwkv_wo, rwkv_w0, rwkv_w1, rwkv_w2, rwkv_a0, rwkv_a1, rwkv_a2, rwkv_v0, rwkv_v1, rwkv_v2, rwkv_g1, rwkv_g2, rwkv_kk, rwkv_ka, rwkv_rk, rwkv_lnw, rwkv_lnb):
    nb, t, d = x_prompt.shape
    ns = x_sample.shape[0]
    depth = norm_mix.shape[0]
    n_even = even_w_in.shape[0]
    n_pages = page_table.shape[1]
    past_len = n_pages * PAGE
    wbuf = state_win_k.shape[2]
    n_phys = cache_k.shape[1]
    mp = nb * t
    tm_p = _row_tile(mp, 512)
    tm_a = _row_tile(mp, 256)
    slopes = jnp.asarray(_alibi_slopes(KVH * REP))
    mx = lambda a: a.astype(_MX)
    row1 = lambda a: a.reshape(1, -1)
    two = lambda a: jnp.tile(a.reshape(1, HD), (1, 2))

    rid = np.arange(16)
    real = (rid % 4) < REP
    slopes16 = jnp.asarray(np.where(real, _alibi_slopes(KVH * REP)[np.minimum((rid // 4) * REP + rid % 4, 11)],
                                    0.0).astype(np.float32))[:, None] * jnp.ones((1, 128), F32)
    grp16 = jnp.asarray(((rid[:, None] // 4 == rid[None, :] // 4) & real[None, :]).astype(np.float32))

    cache_k5 = cache_k.reshape(n_even, n_phys, PAGE, 2 * KV_DIM)
    cache_v5 = cache_v.reshape(n_even, n_phys, PAGE, 2 * KV_DIM)
    win_k4 = state_win_k.reshape(n_even, ns, wbuf, KV_DIM)
    win_v4 = state_win_v.reshape(n_even, ns, wbuf, KV_DIM)

    h_p = x_prompt.reshape(mp, d)
    h_s = x_sample.reshape(ns, d)
    outs = {k_: [] for k_ in ('nk_p', 'nv_p', 'nk_s', 'nv_s', 'wk_p', 'wv_p', 'wk_s', 'wv_s', 'pl_p', 'pl_s',
                              'sh_p', 'sh_s', 'st_p', 'st_s')}
    vf_p = vf_s = None

    for i in range(depth):
        gn = row1(norm_mix[i])
        if i % 2 == 0:
            e = i // 2
            w_in = mx(jnp.pad(even_w_in[e], ((0, 0), (0, IN_PAD - even_w_in.shape[2]))))
            wbd = mx(_block_diag(pool_w[e]))
            psc = row1(pool_scale[e])
            qg, ksg, kwg, kcg = two(q_gain[e]), two(k_gain[e, 1]), two(k_gain[e, 2]), two(k_gain[e, 0])
            w_out = even_w_out[e]
            wo_pool = mx(w_out[:POOL_DIM])
            wo_nsa = mx(jnp.pad(w_out[POOL_DIM:].reshape(KVH, REP * HD, d), ((0, 0), (0, 256 - REP * HD), (0, 0))))
            pe_k = jnp.tile(cmp_pe[e, 0], (1, 2))
            pe_v = jnp.tile(cmp_pe[e, 1], (1, 2))
            eye = jnp.eye(2, dtype=F32)
            bd = lambda w: mx((eye[None, :, None, :, None] * w[:, None, :, None, :]).reshape(CMP_LEN, 128, 128))
            cw_k, cw_v = bd(cmp_w[e, 0]), bd(cmp_w[e, 1])

            u, q, nk, nv, kw, vw, gate = _inproj(h_p, gn, w_in, qg, ksg, kwg, tm_a, HD ** -0.5 * LOG2E)
            y_pool = _pool_prompt(u.reshape(nb, t, POOL_DIM), wbd, psc, _row_tile(t, 512))
            nh_p = t // CMP_STRIDE
            pt_p = jnp.arange(nb * (t // PAGE), dtype=jnp.int32).reshape(nb, t // PAGE)
            kc, vc = _compress(nk.reshape(1, mp // PAGE, PAGE, 2 * KV_DIM), nv.reshape(1, mp // PAGE, PAGE, 2 * KV_DIM),
                               pt_p, 0, pe_k, pe_v, cw_k, cw_v, kcg)
            n_cmp_p = nh_p - (CMP_LEN // CMP_STRIDE - 1)
            n_slc_p = t // SLC_LEN
            gates_t = gate[:, :KVH * REP * 3].reshape(nb, t, KVH, REP * 3).transpose(0, 2, 3, 1)
            chunks_t = lambda x: (x.reshape(nb, t // KEY_TILE, KEY_TILE, KVH, HD).transpose(0, 3, 1, 4, 2).astype(_MX))
            o4t = _nsa_prompt(slopes, _heads(q, nb, t), _heads(kc, nb, nh_p),
                              vc.reshape(nb, nh_p, KVH, HD).transpose(0, 2, 3, 1).astype(_MX),
                              _heads(nk[:, KV_DIM:], nb, t), chunks_t(nv[:, KV_DIM:]),
                              _heads(kw, nb, t), chunks_t(vw), gates_t,
                              _imp_matrix(nh_p, n_cmp_p, n_slc_p, -(-n_slc_p // 8) * 8).T, n_cmp_p)
            o4 = o4t.transpose(0, 1, 3, 2).reshape(KVH, mp, 256)
            h_p = _outproj(h_p, y_pool.reshape(mp, POOL_DIM), o4, wo_pool, wo_nsa, tm_p)
            outs['nk_p'].append(nk.reshape(nb, t, 2, KVH, HD))
            outs['nv_p'].append(nv.reshape(nb, t, 2, KVH, HD))
            kw3 = kw.reshape(nb, t, KVH, HD)
            vw3 = vw.reshape(nb, t, KVH, HD)
            if t < wbuf:
                zpad = jnp.zeros((nb, wbuf - t, KVH, HD), F32)
                kw3, vw3 = jnp.concatenate([zpad, kw3], 1), jnp.concatenate([zpad, vw3], 1)
            outs['wk_p'].append(kw3[:, -wbuf:])
            outs['wv_p'].append(vw3[:, -wbuf:])
            outs['pl_p'].append(u.reshape(nb, t, POOL_DIM)[:, -POOL_BUF:])

            u, q, nk, nv, kw, vw, gate = _inproj(h_s, gn, w_in, qg, ksg, kwg, ns, HD ** -0.5)
            y_pool = _pool_step(state_pool[e], u, wbd, psc, past_len)
            kc, vc = _compress(cache_k5, cache_v5, page_table, e, pe_k, pe_v, cw_k, cw_v, kcg)
            nh_s = kc.shape[1]
            n_cmp_s = nh_s - (CMP_LEN // CMP_STRIDE - 1)
            n_slc_s = -(-(past_len + 1) // SLC_LEN)
            q16 = jnp.pad(q.reshape(ns, KVH, REP, HD), ((0, 0), (0, 0), (0, 1), (0, 0))).reshape(ns, 16, 1, HD)
            gsel = jnp.asarray((np.arange(16)[:, None] // 4 == np.arange(KVH)[None, :]).astype(np.float32))
            qbd = (q16.astype(F32) * gsel[None, :, :, None]).reshape(ns, 16, KV_DIM).astype(_MX)
            new_rows = jnp.pad(jnp.stack([nk[:, KV_DIM:], nv[:, KV_DIM:], kw, vw], axis=1), ((0, 0), (0, 4), (0, 0)))
            g16 = jnp.pad(gate[:, :KVH * REP * 3].reshape(ns, KVH, REP, 3), ((0, 0), (0, 0), (0, 1), (0, 125)))
            o16 = _nsa_sample(page_table, e, qbd, mx(kc), mx(vc), cache_k5, cache_v5, new_rows, win_k4, win_v4,
                              g16.reshape(ns, 16, 128), slopes16, grp16,
                              _imp_matrix(nh_s, n_cmp_s, n_slc_s, _pad_lanes(n_slc_s)), n_cmp_s, n_slc_s)
            o4 = o16.reshape(ns, KVH, 4, HD)[:, :, :REP].reshape(ns, KVH, REP * HD).transpose(1, 0, 2)
            o4 = jnp.pad(o4, ((0, 0), (0, 0), (0, 256 - REP * HD)))
            h_s = _outproj(h_s, y_pool, o4, wo_pool, wo_nsa, ns)
            outs['nk_s'].append(nk.reshape(ns, 1, 2, KVH, HD))
            outs['nv_s'].append(nv.reshape(ns, 1, 2, KVH, HD))
            outs['wk_s'].append(jnp.concatenate([state_win_k[e], kw.reshape(ns, 1, KVH, HD)], axis=1)[:, -wbuf:])
            outs['wv_s'].append(jnp.concatenate([state_win_v[e], vw.reshape(ns, 1, KVH, HD)], axis=1)[:, -wbuf:])
            outs['pl_s'].append(jnp.concatenate([state_pool[e], u[:, None]], axis=1)[:, -POOL_BUF:])
        else:
            o = i // 2
            lora_in = lambda a: mx(jnp.pad(a, ((0, 0), (0, LORA_PAD - a.shape[1]))))
            lora_out = lambda a: mx(jnp.pad(a, ((0, LORA_PAD - a.shape[0]), (0, 0))))
            p = dict(mu=rwkv_mu[o], wr=mx(rwkv_wr[o]), wk=mx(rwkv_wk[o]), wv=mx(rwkv_wv[o]),
                     w1=lora_in(rwkv_w1[o]), w2=lora_out(rwkv_w2[o]), a1=lora_in(rwkv_a1[o]), a2=lora_out(rwkv_a2[o]),
                     g1=lora_in(rwkv_g1[o]), g2=lora_out(rwkv_g2[o]), w0=row1(rwkv_w0[o]), a0=row1(rwkv_a0[o]),
                     k_k=row1(rwkv_kk[o]), k_a=row1(rwkv_ka[o]))
            vparams = None if o == 0 else (row1(rwkv_v0[o - 1]), lora_in(rwkv_v1[o - 1]), lora_out(rwkv_v2[o - 1]))
            lnw, lnb, rk, wo = row1(rwkv_lnw[o]), row1(rwkv_lnb[o]), row1(rwkv_rk[o]), mx(rwkv_wo[o])
            nhd = d // HD

            xn = _norm(h_p, gn, tm_p)
            xn3 = xn.reshape(nb, t, d)
            xprev = jnp.concatenate([jnp.zeros((nb, 1, d), F32), xn3[:, :-1]], axis=1).reshape(mp, d)
            vres = None if o == 0 else (vf_p,) + vparams
            r, w, k, v, kk, b, g = _rwkv_proj(xn, xprev, p, vres, tm_a)
            if o == 0:
                vf_p = v
            seq = lambda a: a.reshape(nb, t, d)
            o_seq, s_fin = _wkv_scan(seq(r), seq(w), seq(k), seq(v), seq(kk), seq(b),
                                     jnp.zeros((nb, nhd, HD, HD), F32), WKV_CHUNK, WKV_HEADS)
            o_tok = o_seq.reshape(mp, d)
            h_p = _rwkv_out(h_p, o_tok, r, k, v, g, lnw, lnb, rk, wo, tm_a)
            outs['sh_p'].append(xn3[:, -1])
            outs['st_p'].append(s_fin)

            xn = _norm(h_s, gn, ns)
            vres = None if o == 0 else (vf_s,) + vparams
            r, w, k, v, kk, b, g = _rwkv_proj(xn, state_shift[o], p, vres, ns)
            if o == 0:
                vf_s = v
            rows = lambda a: a.reshape(ns, nhd, 1, HD)
            o_col, s_fin = _wkv_step(rows(r), rows(w), rows(k), rows(kk), rows(b), v.reshape(ns, nhd, HD, 1),
                                     state_wkv[o])
            h_s = _rwkv_out(h_s, o_col.reshape(ns, d), r, k, v, g, lnw, lnb, rk, wo, ns)
            outs['sh_s'].append(xn)
            outs['st_s'].append(s_fin)

        w1, w2 = mx(mlp_w1[i]), mx(mlp_w2[i])
        gm, gp, wg, wp = row1(norm_mlp[i]), row1(norm_ple[i]), mx(ple_gate[i]), mx(ple_proj[i])
        h_p = _mlp(h_p, gm, w1, w2, tm_p, 1024)
        h_p = _ple(h_p, p_prompt[i].reshape(mp, -1), gp, wg, wp, tm_p)
        h_s = _mlp(h_s, gm, w1, w2, ns, 1024)
        h_s = _ple(h_s, p_sample[i].reshape(ns, -1), gp, wg, wp, ns)

    st = lambda name: jnp.stack(outs[name])
    return (h_p.reshape(nb, t, d), h_s.reshape(ns, 1, d), st('nk_p'), st('nv_p'), st('nk_s'), st('nv_s'),
            st('wk_p'), st('wv_p'), st('wk_s'), st('wv_s'), st('pl_p'), st('pl_s'),
            st('sh_p'), st('sh_s'), st('st_p'), st('st_s'))
```

```python
import functools
import math

import numpy as np
import jax
import jax.numpy as jnp
from jax import lax
from jax.experimental import pallas as pl
from jax.experimental.pallas import tpu as pltpu

F32 = jnp.float32
_MX = jnp.bfloat16

HD = 64
POOL_DIM = 256
POOL_WINDOWS = (2, 4, 8, 16)
POOL_BUF = 15
KVH = 4
REP = 3
NSA_DIM = KVH * REP * HD
KV_DIM = KVH * HD
CMP_LEN, CMP_STRIDE = 32, 16
SLC_LEN, SLC_TOP = 64, 16
WINDOW = 512
QBLK = 256
KEY_TILE = 256
PAGE = 128
RMS_EPS = 1e-6
LN_X_EPS = 64e-5
NEG = -1e30
LOG2E = 1.4426950408889634
BIG = 1e9
WKV_CHUNK = 64
WKV_PASSES = (1, 2, 1)
WKV_HEADS = 16
LORA_PAD = 128
IN_PAD = 2688
VMEM_LIMIT = 56 * 1024 * 1024


def _cp(*sem):
    return pltpu.CompilerParams(dimension_semantics=sem, vmem_limit_bytes=VMEM_LIMIT)


def _alibi_slopes(n):
    p = 2 ** int(math.floor(math.log2(n)))
    s = [2.0 ** (-8.0 * (i + 1) / p) for i in range(p)]
    if p < n:
        s += [2.0 ** (-8.0 * (i + 1) / (2 * p)) for i in range(0, 2 * p, 2)][: n - p]
    return np.asarray(s, dtype=np.float32)


def _rms(x, g):
    return x * lax.rsqrt(jnp.mean(x * x, axis=-1, keepdims=True) + RMS_EPS) * g


def _sigmoid(x):
    return 1.0 / (1.0 + jnp.exp(-x))


def _dot(a, b):
    return jnp.dot(a.astype(_MX), b.astype(_MX), preferred_element_type=F32)


def _dot_nt(a, b):
    return lax.dot_general(a.astype(_MX), b.astype(_MX), (((1,), (1,)), ((), ())),
                           preferred_element_type=F32)


def _dot2(a, b):
    hi = a.astype(_MX)
    lo = (a - hi.astype(F32)).astype(_MX)
    b = b.astype(_MX)
    return (jnp.dot(hi, b, preferred_element_type=F32) + jnp.dot(lo, b, preferred_element_type=F32))


def _dot2r(a, b):
    hi = b.astype(_MX)
    lo = (b - hi.astype(F32)).astype(_MX)
    a = a.astype(_MX)
    return (jnp.dot(a, hi, preferred_element_type=F32) + jnp.dot(a, lo, preferred_element_type=F32))


def _seg64_sum(y):
    left = lax.broadcasted_iota(jnp.int32, y.shape, 1) < HD
    sa = jnp.sum(jnp.where(left, y, 0.0), axis=-1, keepdims=True)
    sb = jnp.sum(jnp.where(left, 0.0, y), axis=-1, keepdims=True)
    return jnp.where(left, sa, sb)


def _head_rms_tile(zt, gain2):
    ms = _seg64_sum(zt * zt) * (1.0 / HD)
    return zt * lax.rsqrt(ms + RMS_EPS) * gain2


def _masked_softmax(s, valid):
    s = jnp.where(valid, s, NEG)
    e = jnp.where(valid, jnp.exp(s - jnp.max(s, axis=-1, keepdims=True)), 0.0)
    return e / jnp.maximum(jnp.sum(e, axis=-1, keepdims=True), 1e-30)


def _online_update(s, valid, v, m_ref, l_ref, acc_ref):
    s = jnp.where(valid, s, NEG)
    m_old = m_ref[...]
    m_new = jnp.maximum(m_old, jnp.max(s, axis=-1, keepdims=True))
    p = jnp.where(valid, jnp.exp(s - m_new), 0.0)
    alpha = jnp.exp(m_old - m_new)
    l_ref[...] = alpha * l_ref[...] + jnp.sum(p, axis=-1, keepdims=True)
    acc_ref[...] = alpha * acc_ref[...] + _dot(p, v)
    m_ref[...] = m_new


def _topk_mask(score, n_cand, n_top):
    col_id = lax.broadcasted_iota(jnp.int32, score.shape, 1)
    rank = jnp.zeros(score.shape, F32)
    for j0 in range(n_cand):
        cj = score[:, j0:j0 + 1]
        beats = jnp.where(cj > score, 1.0, jnp.where((cj == score) & (col_id > j0), 1.0, 0.0))
        rank = rank + beats
    return jnp.where((rank < n_top) & (col_id < n_cand), 1.0, 0.0)


def _norm_kernel(x_ref, g_ref, o_ref):
    o_ref[...] = _rms(x_ref[...], g_ref[...])


def _norm(x, g, tm):
    m, d = x.shape
    return pl.pallas_call(
        _norm_kernel, grid=(m // tm,),
        in_specs=[pl.BlockSpec((tm, d), lambda i: (i, 0)), pl.BlockSpec((1, d), lambda i: (0, 0))],
        out_specs=pl.BlockSpec((tm, d), lambda i: (i, 0)),
        out_shape=jax.ShapeDtypeStruct((m, d), F32), compiler_params=_cp("parallel"))(x, g)


def _mlp_kernel(h_ref, g_ref, w1_ref, w2_ref, o_ref, xn_ref):
    @pl.when(pl.program_id(1) == 0)
    def _():
        x = h_ref[...]
        xn_ref[...] = _rms(x, g_ref[...]).astype(_MX)
        o_ref[...] = x

    a = jnp.dot(xn_ref[...], w1_ref[...], preferred_element_type=F32)
    a = jnp.square(jnp.maximum(a, 0.0))
    o_ref[...] += _dot(a, w2_ref[...])


def _mlp(h, g, w1, w2, tm, tf):
    m, d = h.shape
    dff = w1.shape[1]
    return pl.pallas_call(
        _mlp_kernel, grid=(m // tm, dff // tf),
        in_specs=[pl.BlockSpec((tm, d), lambda i, j: (i, 0)), pl.BlockSpec((1, d), lambda i, j: (0, 0)),
                  pl.BlockSpec((d, tf), lambda i, j: (0, j)), pl.BlockSpec((tf, d), lambda i, j: (j, 0))],
        out_specs=pl.BlockSpec((tm, d), lambda i, j: (i, 0)),
        out_shape=jax.ShapeDtypeStruct((m, d), F32),
        scratch_shapes=[pltpu.VMEM((tm, d), _MX)],
        compiler_params=_cp("parallel", "arbitrary"))(h, g, w1, w2)


def _ple_kernel(h_ref, p_ref, g_ref, wg_ref, wp_ref, o_ref):
    h = h_ref[...]
    gate = _sigmoid(_dot(_rms(h, g_ref[...]), wg_ref[...]))
    o_ref[...] = h + gate * _dot(p_ref[...], wp_ref[...])


def _ple(h, p, g, wg, wp, tm):
    m, d = h.shape
    pd = p.shape[1]
    return pl.pallas_call(
        _ple_kernel, grid=(m // tm,),
        in_specs=[pl.BlockSpec((tm, d), lambda i: (i, 0)), pl.BlockSpec((tm, pd), lambda i: (i, 0)),
                  pl.BlockSpec((1, d), lambda i: (0, 0)), pl.BlockSpec((d, d), lambda i: (0, 0)),
                  pl.BlockSpec((pd, d), lambda i: (0, 0))],
        out_specs=pl.BlockSpec((tm, d), lambda i: (i, 0)),
        out_shape=jax.ShapeDtypeStruct((m, d), F32), compiler_params=_cp("parallel"))(h, p, g, wg, wp)


_SEG_U = 0
_SEG_Q = POOL_DIM
_SEG_KV = POOL_DIM + NSA_DIM
_SEG_GL = _SEG_KV + 6 * KV_DIM


def _inproj_kernel(x_ref, gn_ref, w_ref, qg_ref, ksg_ref, kwg_ref,
                   u_ref, q_ref, nk_ref, nv_ref, kw_ref, vw_ref, gate_ref, z_ref, *, q_scale):
    xn = _rms(x_ref[...], gn_ref[...])
    z_ref[...] = _dot(xn, w_ref[...])
    u_ref[...] = z_ref[:, _SEG_U:_SEG_U + POOL_DIM]
    for c in range(NSA_DIM // 128):
        zt = z_ref[:, _SEG_Q + c * 128:_SEG_Q + (c + 1) * 128]
        q_ref[:, c * 128:(c + 1) * 128] = (_head_rms_tile(zt, qg_ref[...]) * q_scale).astype(q_ref.dtype)
    kv = _SEG_KV
    nk_ref[:, 0:KV_DIM] = z_ref[:, kv:kv + KV_DIM]
    nv_ref[:, 0:KV_DIM] = z_ref[:, kv + KV_DIM:kv + 2 * KV_DIM]
    nv_ref[:, KV_DIM:2 * KV_DIM] = z_ref[:, kv + 3 * KV_DIM:kv + 4 * KV_DIM]
    vw_ref[...] = z_ref[:, kv + 5 * KV_DIM:kv + 6 * KV_DIM]
    for c in range(KV_DIM // 128):
        zs = z_ref[:, kv + 2 * KV_DIM + c * 128:kv + 2 * KV_DIM + (c + 1) * 128]
        nk_ref[:, KV_DIM + c * 128:KV_DIM + (c + 1) * 128] = _head_rms_tile(zs, ksg_ref[...])
        zw = z_ref[:, kv + 4 * KV_DIM + c * 128:kv + 4 * KV_DIM + (c + 1) * 128]
        kw_ref[:, c * 128:(c + 1) * 128] = _head_rms_tile(zw, kwg_ref[...])
    gate_ref[...] = _sigmoid(z_ref[:, _SEG_GL:_SEG_GL + 128])


def _inproj(x, gn, w, qg, ksg, kwg, tm, q_scale):
    m, d = x.shape
    row = lambda i: (i, 0)
    fix = lambda i: (0, 0)
    outs = [(POOL_DIM, F32), (NSA_DIM, _MX), (2 * KV_DIM, F32), (2 * KV_DIM, F32), (KV_DIM, F32), (KV_DIM, F32),
            (128, F32)]
    return pl.pallas_call(
        functools.partial(_inproj_kernel, q_scale=q_scale), grid=(m // tm,),
        in_specs=[pl.BlockSpec((tm, d), row), pl.BlockSpec((1, d), fix), pl.BlockSpec((d, IN_PAD), fix),
                  pl.BlockSpec((1, 128), fix), pl.BlockSpec((1, 128), fix), pl.BlockSpec((1, 128), fix)],
        out_specs=[pl.BlockSpec((tm, c), row) for c, _ in outs],
        out_shape=[jax.ShapeDtypeStruct((m, c), dt) for c, dt in outs],
        scratch_shapes=[pltpu.VMEM((tm, IN_PAD), F32)],
        compiler_params=_cp("parallel"))(x, gn, w, qg, ksg, kwg)


def _pool_select(sums, u, cnt):
    grp = lax.broadcasted_iota(jnp.int32, u.shape, 1) >> 6
    ssel = jnp.where(grp == 0, sums[2], jnp.where(grp == 1, sums[4], jnp.where(grp == 2, sums[8], sums[16])))
    return ssel / cnt - u, grp


def _pool_kernel(u_ref, wbd_ref, sc_ref, y_ref, ext_ref, *, tm):
    i = pl.program_id(1)

    @pl.when(i == 0)
    def _():
        ext_ref[0:16, :] = jnp.zeros((16, POOL_DIM), F32)

    u = u_ref[...]
    ext_ref[16:16 + tm, :] = u
    acc = u
    sums = {}
    for s in range(1, 16):
        acc = acc + ext_ref[16 - s:16 - s + tm, :]
        if s + 1 in POOL_WINDOWS:
            sums[s + 1] = acc
    grp = lax.broadcasted_iota(jnp.int32, u.shape, 1) >> 6
    win = jnp.where(grp == 0, 2, jnp.where(grp == 1, 4, jnp.where(grp == 2, 8, 16)))
    pos = i * tm + lax.broadcasted_iota(jnp.int32, u.shape, 0)
    cnt = jnp.minimum(win, pos + 1).astype(F32)
    mixed, _ = _pool_select(sums, u, cnt)
    y_ref[...] = _dot(mixed, wbd_ref[...]) * sc_ref[...]
    ext_ref[0:16, :] = ext_ref[tm:tm + 16, :]


def _pool_prompt(u, wbd, scale, tm):
    n, t, c = u.shape
    return pl.pallas_call(
        functools.partial(_pool_kernel, tm=tm), grid=(n, t // tm),
        in_specs=[pl.BlockSpec((None, tm, c), lambda b, i: (b, i, 0)), pl.BlockSpec((c, c), lambda b, i: (0, 0)),
                  pl.BlockSpec((1, c), lambda b, i: (0, 0))],
        out_specs=pl.BlockSpec((None, tm, c), lambda b, i: (b, i, 0)),
        out_shape=jax.ShapeDtypeStruct((n, t, c), F32),
        scratch_shapes=[pltpu.VMEM((tm + 16, c), F32)],
        compiler_params=_cp("parallel", "arbitrary"))(u, wbd, scale)


def _pool_step_kernel(buf_ref, u_ref, wbd_ref, sc_ref, y_ref, *, pos0):
    u = u_ref[...]
    acc = u
    sums = {}
    for s in range(1, 16):
        acc = acc + buf_ref[:, POOL_BUF - s, :]
        if s + 1 in POOL_WINDOWS:
            sums[s + 1] = acc
    grp = lax.broadcasted_iota(jnp.int32, u.shape, 1) >> 6
    win = jnp.where(grp == 0, 2, jnp.where(grp == 1, 4, jnp.where(grp == 2, 8, 16)))
    cnt = jnp.minimum(win, pos0 + 1).astype(F32)
    mixed, _ = _pool_select(sums, u, cnt)
    y_ref[...] = _dot(mixed, wbd_ref[...]) * sc_ref[...]


def _pool_step(buf, u, wbd, scale, pos0):
    n, c = u.shape
    full = lambda *shape: pl.BlockSpec(shape, lambda i: (0,) * len(shape))
    return pl.pallas_call(
        functools.partial(_pool_step_kernel, pos0=pos0), grid=(1,),
        in_specs=[full(n, POOL_BUF, c), full(n, c), full(c, c), full(1, c)],
        out_specs=full(n, c), out_shape=jax.ShapeDtypeStruct((n, c), F32),
        compiler_params=_cp("arbitrary"))(buf, u, wbd, scale)


def _pages_per_step(n_pages, most=8):
    return next(p for p in (32, 16, 8, 4, 2, 1) if p <= most and n_pages % p == 0)


def _compress_kernel(pt_ref, *refs, n_pages, pps):
    xk_refs, xv_refs = refs[:pps], refs[pps:2 * pps]
    pek_ref, pev_ref, wk_ref, wv_ref, kcg_ref, kc_ref, vc_ref, xs_k, xs_v, bsh = refs[2 * pps:]
    j = pl.program_id(1)
    n_tiles = KV_DIM // 128
    for p in range(pps):
        row0 = pl.multiple_of((j * pps + p) * PAGE, PAGE)
        for c in range(n_tiles):
            xs_k[c, pl.ds(row0, PAGE), :] = xk_refs[p][:, c * 128:(c + 1) * 128]
            xs_v[c, pl.ds(row0, PAGE), :] = xv_refs[p][:, c * 128:(c + 1) * 128]

    @pl.when(j == n_pages // pps - 1)
    def _():
        nh = n_pages * (PAGE // CMP_STRIDE)

        def run(xs, c, pe_ref, w_ref):
            a = jnp.zeros((nh, 128), F32)
            b = jnp.zeros((nh, 128), F32)
            for l in range(CMP_STRIDE):
                xl = xs[c, pl.ds(l, nh, stride=CMP_STRIDE), :]
                a = a + _dot(xl + pe_ref[l:l + 1, :], w_ref[l])
                b = b + _dot(xl + pe_ref[CMP_STRIDE + l:CMP_STRIDE + l + 1, :], w_ref[CMP_STRIDE + l])
            bsh[0:nh, :] = b
            bsh[nh:nh + 8, :] = jnp.zeros((8, 128), F32)
            return a + bsh[1:nh + 1, :]

        for c in range(n_tiles):
            kc_ref[:, c * 128:(c + 1) * 128] = _head_rms_tile(run(xs_k, c, pek_ref, wk_ref), kcg_ref[...])
            vc_ref[:, c * 128:(c + 1) * 128] = run(xs_v, c, pev_ref, wv_ref)


def _compress(cache_k, cache_v, page_table, e, pek, pev, wk, wv, kcg):
    nb, n_pages = page_table.shape
    nh = n_pages * (PAGE // CMP_STRIDE)
    pps = _pages_per_step(n_pages, most=32)
    page = lambda p: pl.BlockSpec((None, None, PAGE, KV_DIM), lambda b, j, pt: (e, pt[b, j * pps + p], 0, 0))
    fix2 = lambda b, j, pt: (0, 0)
    fix3 = lambda b, j, pt: (0, 0, 0)
    out = pl.BlockSpec((None, nh, KV_DIM), lambda b, j, pt: (b, 0, 0))
    gs = pltpu.PrefetchScalarGridSpec(
        num_scalar_prefetch=1, grid=(nb, n_pages // pps),
        in_specs=[page(p) for p in range(pps)] * 2
                 + [pl.BlockSpec((CMP_LEN, 128), fix2), pl.BlockSpec((CMP_LEN, 128), fix2),
                    pl.BlockSpec((CMP_LEN, 128, 128), fix3), pl.BlockSpec((CMP_LEN, 128, 128), fix3),
                    pl.BlockSpec((1, 128), fix2)],
        out_specs=[out, out],
        scratch_shapes=[pltpu.VMEM((KV_DIM // 128, n_pages * PAGE, 128), F32),
                        pltpu.VMEM((KV_DIM // 128, n_pages * PAGE, 128), F32),
                        pltpu.VMEM((nh + 8, 128), F32)])
    return pl.pallas_call(
        functools.partial(_compress_kernel, n_pages=n_pages, pps=pps), grid_spec=gs,
        out_shape=[jax.ShapeDtypeStruct((nb, nh, KV_DIM), F32)] * 2,
        compiler_params=_cp("parallel", "arbitrary"),
    )(page_table, *([cache_k] * pps), *([cache_v] * pps), pek, pev, wk, wv, kcg)


def _topk_rows(score, n_top):
    rid = lax.broadcasted_iota(jnp.int32, score.shape, 0).astype(F32)
    sel = jnp.zeros(score.shape, F32)
    for _ in range(n_top):
        m = jnp.max(score, axis=0, keepdims=True)
        first = jnp.min(jnp.where(score == m, rid, 1e9), axis=0, keepdims=True)
        hit = rid == first
        sel = jnp.where(hit, 1.0, sel)
        score = jnp.where(hit, -jnp.inf, score)
    return sel


def _nsa_prompt_kernel(sl_ref, q_ref, kc_ref, vct_ref, ks_ref, vst_ref, kw_ref, vwt_ref, gt_ref, mimpt_ref,
                       o_ref, sel_ref, s_ref, acc_ref, p_ref, bias_ref, mask_ref, *, n_cmp, n_slc, n_top):
    g = pl.program_id(1)
    i = pl.program_id(2)
    nh = kc_ref.shape[0]
    q = q_ref[...].reshape(REP * QBLK, HD)
    qpos0 = i * QBLK
    pos = qpos0 + lax.broadcasted_iota(jnp.int32, (1, QBLK), 1)
    slope = [sl_ref[g * REP + r] * LOG2E for r in range(REP)]

    cid = lax.broadcasted_iota(jnp.int32, (nh, QBLK), 0)
    dist_c = pos - (cid * CMP_STRIDE + (CMP_LEN - 1))
    valid_c = (dist_c >= 0) & (cid < n_cmp)
    dist_cf = dist_c.astype(F32)
    s_c = _dot_nt(kc_ref[...], q)
    vct = vct_ref[...]
    imp = jnp.zeros((nh, QBLK), F32)
    o_cmp = []
    for r in range(REP):
        s = jnp.where(valid_c, s_c[:, r * QBLK:(r + 1) * QBLK] - slope[r] * dist_cf, NEG)
        e = jnp.where(valid_c, jnp.exp2(s - jnp.max(s, axis=0, keepdims=True)), 0.0)
        p = e / jnp.maximum(jnp.sum(e, axis=0, keepdims=True), 1e-30)
        imp = imp + p
        o_cmp.append(_dot(vct, p))
    imp_slc = _dot2r(mimpt_ref[...], imp)
    blk = lax.broadcasted_iota(jnp.int32, imp_slc.shape, 0)
    cur = pos >> 6
    forced = (blk == 0) | (blk == cur) | (blk == cur - 1)
    causal = blk * SLC_LEN <= pos
    score = jnp.where(forced, BIG, jnp.where(causal, imp_slc, -BIG))
    score = jnp.where(blk < n_slc, score, -jnp.inf)
    sel_ref[...] = _topk_rows(score, n_top)

    d0 = (lax.broadcasted_iota(jnp.int32, (KEY_TILE, QBLK), 1)
          - lax.broadcasted_iota(jnp.int32, (KEY_TILE, QBLK), 0))
    d0f = d0.astype(F32)
    for r in range(REP):
        bias_ref[r] = -slope[r] * d0f
    mask_ref[0] = jnp.where(d0 >= 0, 0.0, NEG)
    mask_ref[1] = jnp.where(d0 < 0, 0.0, NEG)
    hi = (qpos0 + QBLK - 1) // KEY_TILE

    def sweep(lo, k_ref, vt_ref, mask_fn):
        def scores(c):
            k0 = pl.multiple_of(c * KEY_TILE, KEY_TILE)
            return _dot_nt(k_ref[pl.ds(k0, KEY_TILE), :], q)

        def body(c, carry):
            m, l, p_prev = carry
            pv = _dot(vt_ref[jnp.maximum(c - 1, lo)], p_prev)
            s_next = scores(jnp.minimum(c + 1, hi))
            off = qpos0 - c * KEY_TILE
            madd = mask_fn(c, off)
            off_f = off.astype(F32)
            cur = c % 2
            m_new, p, p_sum = [], [], []
            for r in range(REP):
                for half in range(QBLK // 128):
                    qc = slice(half * 128, (half + 1) * 128)
                    cols = slice(r * QBLK + half * 128, r * QBLK + (half + 1) * 128)
                    s_c = s_ref[cur, :, cols] + bias_ref[r, :, qc] + madd[:, qc]
                    m_c = jnp.maximum(m[:, cols], jnp.max(s_c, axis=0, keepdims=True) - slope[r] * off_f)
                    p_c = jnp.exp2(s_c - (m_c + slope[r] * off_f))
                    m_new.append(m_c)
                    p_sum.append(jnp.sum(p_c, axis=0, keepdims=True))
                    p.append(p_c.astype(_MX))
            m_new = jnp.concatenate(m_new, axis=1)
            alpha = jnp.exp2(m - m_new)
            l = alpha * l + jnp.concatenate(p_sum, axis=1)
            acc_ref[...] = alpha * (acc_ref[...] + pv)
            s_ref[(c + 1) % 2] = s_next
            return m_new, l, jnp.concatenate(p, axis=1)

        s_ref[lo % 2] = scores(lo)
        acc_ref[...] = jnp.zeros(acc_ref.shape, F32)
        p_ref[...] = jnp.zeros(p_ref.shape, p_ref.dtype)
        init = (jnp.full((1, REP * QBLK), NEG, F32), jnp.zeros((1, REP * QBLK), F32), p_ref[...])
        _, l, p_last = lax.fori_loop(lo, hi + 1, body, init)
        return (acc_ref[...] + _dot(vt_ref[hi], p_last)) / l

    def slc_mask(c, off):
        per_tile = KEY_TILE // SLC_LEN
        rows = [jnp.broadcast_to(sel_ref[pl.ds(per_tile * c + b, 1), :], (SLC_LEN, QBLK)) for b in range(per_tile)]
        picked = jnp.where(jnp.concatenate(rows, axis=0) > 0.5, 0.0, NEG)
        return picked + jnp.where(off == 0, mask_ref[0], 0.0)

    def win_mask(c, off):
        return jnp.where(off == 0, mask_ref[0], jnp.where(off == WINDOW, mask_ref[1], 0.0))

    o_slc = sweep(0, ks_ref, vst_ref, slc_mask)
    lo_win = jnp.maximum(qpos0 - WINDOW, 0) // KEY_TILE
    o_win = sweep(lo_win, kw_ref, vwt_ref, win_mask)

    gates = gt_ref[...]
    for r in range(REP):
        sl = slice(r * QBLK, (r + 1) * QBLK)
        o_ref[r * HD:(r + 1) * HD, :] = (gates[3 * r:3 * r + 1, :] * o_cmp[r] + gates[3 * r + 1:3 * r + 2, :] * o_slc[:, sl]
                                         + gates[3 * r + 2:3 * r + 3, :] * o_win[:, sl])
    o_ref[REP * HD:, :] = jnp.zeros((o_ref.shape[0] - REP * HD, QBLK), F32)


def _nsa_prompt(slopes, q_h, kc_h, vc_t, ks_h, vs_t, kw_h, vw_t, gates_t, mimp_t, n_cmp):
    n, _, t, _ = q_h.shape
    assert t % KEY_TILE == 0 and QBLK == KEY_TILE and WINDOW % KEY_TILE == 0
    nh = kc_h.shape[2]
    n_slc = t // SLC_LEN
    nq = t // QBLK
    nkt = t // KEY_TILE
    seq = lambda *shape: pl.BlockSpec((None, None) + shape, lambda b, g, i: (b, g) + (0,) * len(shape))
    return pl.pallas_call(
        functools.partial(_nsa_prompt_kernel, n_cmp=n_cmp, n_slc=n_slc, n_top=min(SLC_TOP, n_slc)),
        grid=(n, KVH, nq),
        in_specs=[pl.BlockSpec(memory_space=pltpu.SMEM),
                  pl.BlockSpec((None, REP, QBLK, HD), lambda b, g, i: (b, g, i, 0)),
                  seq(nh, HD), seq(HD, nh), seq(t, HD), seq(nkt, HD, KEY_TILE), seq(t, HD), seq(nkt, HD, KEY_TILE),
                  pl.BlockSpec((None, None, 9, QBLK), lambda b, g, i: (b, g, 0, i)),
                  pl.BlockSpec(mimp_t.shape, lambda b, g, i: (0, 0))],
        out_specs=pl.BlockSpec((None, None, 256, QBLK), lambda b, g, i: (g, b, 0, i)),
        out_shape=jax.ShapeDtypeStruct((KVH, n, 256, t), F32),
        scratch_shapes=[pltpu.VMEM((mimp_t.shape[0], QBLK), F32), pltpu.VMEM((2, KEY_TILE, REP * QBLK), F32),
                        pltpu.VMEM((HD, REP * QBLK), F32), pltpu.VMEM((KEY_TILE, REP * QBLK), _MX),
                        pltpu.VMEM((REP, KEY_TILE, QBLK), F32), pltpu.VMEM((2, KEY_TILE, QBLK), F32)],
        compiler_params=_cp("parallel", "parallel", "arbitrary"),
    )(slopes, q_h, kc_h, vc_t, ks_h, vs_t, kw_h, vw_t, gates_t, mimp_t)


def _nsa_sample_kernel(pt_ref, *refs, n_pages, pps, n_cmp, n_slc, n_top, wbuf):
    q_ref, kc_ref, vc_ref = refs[:3]
    ck_refs, cv_refs = refs[3:3 + pps], refs[3 + pps:3 + 2 * pps]
    (new_ref, wk_ref, wv_ref, g_ref, sl_ref, grp_ref, mimp_ref, o_ref, m_ref, l_ref, acc_ref, sel_ref, ocmp_ref,
     ofull_ref) = refs[3 + 2 * pps:]
    j = pl.program_id(1)
    span = pps * PAGE
    pos0 = n_pages * PAGE
    q = q_ref[...]
    slope = sl_ref[:, 0:1]
    nrow = q.shape[0]

    @pl.when(j == 0)
    def _():
        nh = kc_ref.shape[0]
        cid = lax.broadcasted_iota(jnp.int32, (nrow, nh), 1)
        dist = pos0 - (cid * CMP_STRIDE + (CMP_LEN - 1))
        valid = (dist >= 0) & (cid < n_cmp)
        p = _masked_softmax(_dot_nt(q, kc_ref[...]) - slope * dist.astype(F32), valid)
        ocmp_ref[...] = _dot(p, vc_ref[...])
        imp = _dot2(_dot2r(grp_ref[...], p), mimp_ref[...])
        blk = lax.broadcasted_iota(jnp.int32, imp.shape, 1)
        cur = pos0 // SLC_LEN
        forced = (blk == 0) | (blk == cur) | (blk == cur - 1)
        causal = blk * SLC_LEN <= pos0
        score = jnp.where(forced, BIG, jnp.where(causal, imp, -BIG))
        sel_ref[...] = _topk_mask(score, n_slc, n_top)
        m_ref[...] = jnp.full(m_ref.shape, NEG, F32)
        l_ref[...] = jnp.zeros(l_ref.shape, F32)
        acc_ref[...] = jnp.zeros(acc_ref.shape, F32)

    sel = sel_ref[...]
    e_row = lax.broadcasted_iota(jnp.int32, (sel.shape[1], span), 0)
    e_col = lax.broadcasted_iota(jnp.int32, (sel.shape[1], span), 1) >> 6
    picked = _dot(sel, jnp.where(e_row == (span // SLC_LEN) * j + e_col, 1.0, 0.0)) > 0.5
    kpos = j * span + lax.broadcasted_iota(jnp.int32, (nrow, span), 1)
    dist = pos0 - kpos
    keys = jnp.concatenate([r[...].astype(_MX) for r in ck_refs], axis=0)
    vals = jnp.concatenate([r[...].astype(_MX) for r in cv_refs], axis=0)
    s = _dot_nt(q, keys) - slope * dist.astype(F32)
    _online_update(s, picked & (dist >= 0), vals, m_ref, l_ref, acc_ref)

    @pl.when(j == n_pages // pps - 1)
    def _():
        qf = q.astype(F32)
        new = new_ref[...]
        s_new = jnp.sum(qf * new[0:1, :], axis=-1, keepdims=True)
        ok = sel[:, n_slc - 1:n_slc] > 0.5
        s_new = jnp.where(ok, s_new, NEG)
        m_old = m_ref[...]
        m_new = jnp.maximum(m_old, s_new)
        p_new = jnp.where(ok, jnp.exp(s_new - m_new), 0.0)
        alpha = jnp.exp(m_old - m_new)
        l_tot = alpha * l_ref[...] + p_new
        acc = alpha * acc_ref[...] + p_new * new[1:2, :]
        o_slc = acc / jnp.maximum(l_tot, 1e-30)

        idx = lax.broadcasted_iota(jnp.int32, (nrow, wbuf), 1)
        dw = wbuf - idx
        valid_w = (dw >= 0) & (dw < WINDOW)
        s_w = jnp.where(valid_w, _dot_nt(q, wk_ref[...]) - slope * dw.astype(F32), NEG)
        s_wn = jnp.sum(qf * new[2:3, :], axis=-1, keepdims=True)
        m_w = jnp.maximum(jnp.max(s_w, axis=-1, keepdims=True), s_wn)
        p_w = jnp.where(valid_w, jnp.exp(s_w - m_w), 0.0)
        p_wn = jnp.exp(s_wn - m_w)
        den = jnp.maximum(jnp.sum(p_w, axis=-1, keepdims=True) + p_wn, 1e-30)
        o_win = (_dot(p_w, wv_ref[...]) + p_wn * new[3:4, :]) / den

        gates = g_ref[...]
        ofull_ref[...] = gates[:, 0:1] * ocmp_ref[...] + gates[:, 1:2] * o_slc + gates[:, 2:3] * o_win
        row_g = lax.broadcasted_iota(jnp.int32, (nrow, HD), 0) >> 2
        o = jnp.zeros((nrow, HD), F32)
        for gg in range(KVH):
            o = o + jnp.where(row_g == gg, ofull_ref[:, gg * HD:(gg + 1) * HD], 0.0)
        o_ref[...] = o


def _nsa_sample(page_table, e, qbd, kc, vc, cache_k, cache_v, new_rows, win_k, win_v, gates16, slopes16, grp16,
                mimp, n_cmp, n_slc):
    nb, n_pages = page_table.shape
    nh = kc.shape[1]
    wbuf = win_k.shape[2]
    per_b = lambda *shape: pl.BlockSpec((None,) + shape, lambda b, j, pt: (b,) + (0,) * len(shape))
    fix = lambda *shape: pl.BlockSpec(shape, lambda b, j, pt: (0,) * len(shape))
    pps = _pages_per_step(n_pages)
    page = lambda p: pl.BlockSpec((None, None, PAGE, KV_DIM), lambda b, j, pt: (e, pt[b, j * pps + p], 0, 1))
    win = pl.BlockSpec((None, None, wbuf, KV_DIM), lambda b, j, pt: (e, b, 0, 0))
    nrow = qbd.shape[1]
    gs = pltpu.PrefetchScalarGridSpec(
        num_scalar_prefetch=1, grid=(nb, n_pages // pps),
        in_specs=[per_b(nrow, KV_DIM), per_b(nh, KV_DIM), per_b(nh, KV_DIM)] + [page(p) for p in range(pps)] * 2
                 + [per_b(8, KV_DIM), win, win, per_b(nrow, 128), fix(nrow, 128), fix(nrow, nrow), fix(*mimp.shape)],
        out_specs=per_b(nrow, HD),
        scratch_shapes=[pltpu.VMEM((nrow, 1), F32), pltpu.VMEM((nrow, 1), F32), pltpu.VMEM((nrow, KV_DIM), F32),
                        pltpu.VMEM((nrow, mimp.shape[1]), F32), pltpu.VMEM((nrow, KV_DIM), F32),
                        pltpu.VMEM((nrow, KV_DIM), F32)])
    return pl.pallas_call(
        functools.partial(_nsa_sample_kernel, n_pages=n_pages, pps=pps, n_cmp=n_cmp, n_slc=n_slc,
                          n_top=min(SLC_TOP, n_slc), wbuf=wbuf),
        grid_spec=gs, out_shape=jax.ShapeDtypeStruct((nb, nrow, HD), F32),
        compiler_params=_cp("parallel", "arbitrary"),
    )(page_table, qbd, kc, vc, *([cache_k] * pps), *([cache_v] * pps), new_rows, win_k, win_v, gates16, slopes16,
      grp16, mimp)


def _outproj_kernel(h_ref, y_ref, o_ref, wp_ref, wn_ref, out_ref):
    acc = h_ref[...] + _dot(y_ref[...], wp_ref[...])
    for g in range(KVH):
        acc = acc + _dot(o_ref[g], wn_ref[g])
    out_ref[...] = acc


def _outproj(h, y_pool, o4, wp, wn, tm):
    m, d = h.shape
    return pl.pallas_call(
        _outproj_kernel, grid=(m // tm,),
        in_specs=[pl.BlockSpec((tm, d), lambda i: (i, 0)), pl.BlockSpec((tm, POOL_DIM), lambda i: (i, 0)),
                  pl.BlockSpec((KVH, tm, 256), lambda i: (0, i, 0)), pl.BlockSpec((POOL_DIM, d), lambda i: (0, 0)),
                  pl.BlockSpec((KVH, 256, d), lambda i: (0, 0, 0))],
        out_specs=pl.BlockSpec((tm, d), lambda i: (i, 0)),
        out_shape=jax.ShapeDtypeStruct((m, d), F32), compiler_params=_cp("parallel"))(h, y_pool, o4, wp, wn)


def _rwkv_proj_kernel(*refs, has_vres):
    if has_vres:
        (xn_ref, xp_ref, mu_ref, wr_ref, wk_ref, wv_ref, w1_ref, w2_ref, a1_ref, a2_ref, g1_ref, g2_ref,
         w0_ref, a0_ref, kk_ref, ka_ref, vf_ref, v0_ref, v1_ref, v2_ref,
         r_out, w_out, k_out, v_out, kk_out, b_out, g_out) = refs
    else:
        (xn_ref, xp_ref, mu_ref, wr_ref, wk_ref, wv_ref, w1_ref, w2_ref, a1_ref, a2_ref, g1_ref, g2_ref,
         w0_ref, a0_ref, kk_ref, ka_ref,
         r_out, w_out, k_out, v_out, kk_out, b_out, g_out) = refs
    xn = xn_ref[...]
    xx = xp_ref[...] - xn
    mix = lambda j: xn + xx * mu_ref[j:j + 1, :]
    xr, xw, xk, xv, xa, xg = [mix(j) for j in range(6)]
    r_out[...] = _dot(xr, wr_ref[...])
    k = _dot(xk, wk_ref[...])
    v = _dot(xv, wv_ref[...])
    z = w0_ref[...] + _dot(jnp.tanh(_dot(xw, w1_ref[...])), w2_ref[...])
    w_log = -(jnp.maximum(-z, 0.0) + jnp.log(1.0 + jnp.exp(-jnp.abs(z)))) - 0.5
    w_out[...] = -jnp.exp(w_log)
    if has_vres:
        v = v + (vf_ref[...] - v) * _sigmoid(v0_ref[...] + _dot(_dot(xv, v1_ref[...]), v2_ref[...]))
    v_out[...] = v
    a = _sigmoid(a0_ref[...] + _dot(_dot(xa, a1_ref[...]), a2_ref[...]))
    g_out[...] = _dot(_sigmoid(_dot(xg, g1_ref[...])), g2_ref[...])
    kk = k * kk_ref[...]
    for c in range(kk.shape[1] // 128):
        sl = slice(c * 128, (c + 1) * 128)
        kt = kk[:, sl]
        kn = kt / jnp.maximum(jnp.sqrt(_seg64_sum(kt * kt)), 1e-12)
        kk_out[:, sl] = kn
        b_out[:, sl] = kn * a[:, sl]
    k_out[...] = k * (1.0 + (a - 1.0) * ka_ref[...])


def _rwkv_proj(xn, xprev, p, vres, tm):
    m, d = xn.shape
    row = pl.BlockSpec((tm, d), lambda i: (i, 0))
    fix = lambda a: pl.BlockSpec(a.shape, lambda i: (0,) * a.ndim)
    args = [xn, xprev, p['mu'], p['wr'], p['wk'], p['wv'], p['w1'], p['w2'], p['a1'], p['a2'], p['g1'], p['g2'],
            p['w0'], p['a0'], p['k_k'], p['k_a']]
    specs = [row, row] + [fix(a) for a in args[2:]]
    if vres is not None:
        vf, v0, v1, v2 = vres
        args += [vf, v0, v1, v2]
        specs += [row, fix(v0), fix(v1), fix(v2)]
    return pl.pallas_call(
        functools.partial(_rwkv_proj_kernel, has_vres=vres is not None), grid=(m // tm,),
        in_specs=specs, out_specs=[row] * 7, out_shape=[jax.ShapeDtypeStruct((m, d), F32)] * 7,
        compiler_params=_cp("parallel"))(*args)


_NN = (((1,), (0,)), ((), ()))
_NT = (((1,), (1,)), ((), ()))
_TN = (((0,), (0,)), ((), ()))


def _split(a):
    hi = a.astype(_MX)
    return hi, (a - hi.astype(F32)).astype(_MX)


def _dot3(a, b, dims=_NN, passes=3):
    dg = lambda x, y: lax.dot_general(x, y, dims, preferred_element_type=F32)
    a_hi, a_lo = _split(a)
    if passes == 1:
        return dg(a_hi, b.astype(_MX))
    b_hi, b_lo = _split(b)
    out = dg(a_hi, b_hi) + dg(a_hi, b_lo)
    return out if passes == 2 else out + dg(a_lo, b_hi)


def _wkv_kernel(r_ref, lw_ref, k_ref, v_ref, kk_ref, b_ref, s0_ref, o_ref, s_ref, *, hb):
    @pl.when(pl.program_id(2) == 0)
    def _():
        s_ref[...] = s0_ref[...]

    c = r_ref.shape[0]
    hd = lambda ref, h: ref[:, h * HD:(h + 1) * HD]
    row = lax.broadcasted_iota(jnp.int32, (c, c), 0)
    col = lax.broadcasted_iota(jnp.int32, (c, c), 1)
    incl = col <= row
    strict = col < row
    ltri = jnp.where(incl, 1.0, 0.0)
    levels = int(math.log2(c))
    heads = range(hb)
    lw = [hd(lw_ref, h) for h in heads]
    v = [hd(v_ref, h) for h in heads]
    p_in, p_solve, p_out = WKV_PASSES
    cum = [_dot3(ltri, lw[h], passes=2) for h in heads]
    g_in = [jnp.exp(cum[h]) for h in heads]
    g_inv = [jnp.exp(-cum[h]) for h in heads]
    b_t = [hd(b_ref, h) * g_inv[h] for h in heads]
    k_t = [hd(k_ref, h) * g_inv[h] for h in heads]
    ar = [jnp.concatenate([-hd(kk_ref, h) * jnp.exp(cum[h] - lw[h]), hd(r_ref, h) * g_in[h]], axis=0) for h in heads]
    pb = [_dot3(ar[h], b_t[h], _NT, p_in) for h in heads]
    pk = [_dot3(ar[h], k_t[h], _NT, p_in) for h in heads]
    xs = [_dot3(ar[h], s_ref[h], _NT, p_in) for h in heads]
    n_mat = [jnp.where(strict, pb[h][:c], 0.0) for h in heads]
    x = [xs[h][:c] + _dot3(jnp.where(strict, pk[h][:c], 0.0), v[h], _NN, p_in) for h in heads]
    for lvl in range(levels):
        x = [x[h] + _dot3(n_mat[h], x[h], _NN, p_solve) for h in heads]
        if lvl + 1 < levels:
            n_mat = [_dot3(n_mat[h], n_mat[h], _NN, p_solve) for h in heads]
    for h in heads:
        o_ref[:, h * HD:(h + 1) * HD] = (xs[h][c:] + _dot3(jnp.where(incl, pb[h][c:], 0.0), x[h], _NN, p_out)
                                         + _dot3(jnp.where(incl, pk[h][c:], 0.0), v[h], _NN, p_out))
    for h in heads:
        s_ref[h] = ((s_ref[h] + _dot3(x[h], b_t[h], _TN, p_out) + _dot3(v[h], k_t[h], _TN, p_out))
                    * g_in[h][c - 1:c, :])


def _wkv_scan(r, lw, k, v, kk, b, s0, tc, hb):
    n, t, d = r.shape
    nh_ = d // HD
    tok = pl.BlockSpec((None, tc, hb * HD), lambda bb, hh, tt: (bb, tt, hh))
    st = pl.BlockSpec((None, hb, HD, HD), lambda bb, hh, tt: (bb, hh, 0, 0))
    return pl.pallas_call(
        functools.partial(_wkv_kernel, hb=hb), grid=(n, nh_ // hb, t // tc),
        in_specs=[tok] * 6 + [st], out_specs=[tok, st],
        out_shape=[jax.ShapeDtypeStruct((n, t, d), F32), jax.ShapeDtypeStruct((n, nh_, HD, HD), F32)],
        compiler_params=_cp("parallel", "parallel", "arbitrary"))(r, lw, k, v, kk, b, s0)


def _wkv_step_kernel(r_ref, lw_ref, k_ref, kk_ref, b_ref, vc_ref, s0_ref, o_ref, s_ref):
    s = s0_ref[...]
    sa = -jnp.sum(s * kk_ref[...], axis=-1, keepdims=True)
    s = s * jnp.exp(lw_ref[...]) + sa * b_ref[...] + vc_ref[...] * k_ref[...]
    s_ref[...] = s
    o_ref[...] = jnp.sum(s * r_ref[...], axis=-1, keepdims=True)


def _wkv_step(r, w, k, kk, b, v_col, s0):
    n, nh_ = r.shape[:2]
    rowv = pl.BlockSpec((None, nh_, 1, HD), lambda i: (i, 0, 0, 0))
    colv = pl.BlockSpec((None, nh_, HD, 1), lambda i: (i, 0, 0, 0))
    st = pl.BlockSpec((None, nh_, HD, HD), lambda i: (i, 0, 0, 0))
    return pl.pallas_call(
        _wkv_step_kernel, grid=(n,), in_specs=[rowv] * 5 + [colv, st], out_specs=[colv, st],
        out_shape=[jax.ShapeDtypeStruct((n, nh_, HD, 1), F32), jax.ShapeDtypeStruct((n, nh_, HD, HD), F32)],
        compiler_params=_cp("parallel"))(r, w, k, kk, b, v_col, s0)


def _rwkv_out_kernel(h_ref, o_ref, r_ref, k_ref, v_ref, g_ref, lnw_ref, lnb_ref, rk_ref, wo_ref, out_ref, y_ref):
    for c in range(h_ref.shape[1] // 128):
        sl = slice(c * 128, (c + 1) * 128)
        o = o_ref[:, sl]
        dlt = o - _seg64_sum(o) * (1.0 / HD)
        var = _seg64_sum(dlt * dlt) * (1.0 / HD)
        on = dlt * lax.rsqrt(var + LN_X_EPS) * lnw_ref[:, sl] + lnb_ref[:, sl]
        bonus = _seg64_sum(r_ref[:, sl] * k_ref[:, sl] * rk_ref[:, sl])
        y_ref[:, sl] = ((on + bonus * v_ref[:, sl]) * g_ref[:, sl]).astype(y_ref.dtype)
    out_ref[...] = h_ref[...] + jnp.dot(y_ref[...], wo_ref[...], preferred_element_type=F32)


def _rwkv_out(h, o, r, k, v, g, lnw, lnb, rk, wo, tm):
    m, d = h.shape
    row = pl.BlockSpec((tm, d), lambda i: (i, 0))
    vec = pl.BlockSpec((1, d), lambda i: (0, 0))
    return pl.pallas_call(
        _rwkv_out_kernel, grid=(m // tm,),
        in_specs=[row] * 6 + [vec] * 3 + [pl.BlockSpec((d, d), lambda i: (0, 0))],
        out_specs=row, out_shape=jax.ShapeDtypeStruct((m, d), F32),
        scratch_shapes=[pltpu.VMEM((tm, d), _MX)],
        compiler_params=_cp("parallel"))(h, o, r, k, v, g, lnw, lnb, rk, wo)


def _row_tile(m, pref):
    return pref if m % pref == 0 else m


def _block_diag(w):
    g, a, b = w.shape
    eye = jnp.eye(g, dtype=w.dtype)
    return (eye[:, None, :, None] * w[:, :, None, :]).reshape(g * a, g * b)


def _imp_matrix(nh, n_cmp, n_slc, width):
    per = SLC_LEN // CMP_STRIDE
    lead = CMP_LEN // CMP_STRIDE - 1
    c = np.arange(nh)[:, None]
    j = np.arange(width)[None, :]
    m = (c - per * j >= -lead) & (c - per * j < per) & (c < n_cmp) & (j < n_slc)
    return jnp.asarray(m.astype(np.float32), dtype=_MX)


def _pad_lanes(n):
    return -(-n // 128) * 128


def _heads(x, n, t):
    return x.reshape(n, t, -1, HD).transpose(0, 2, 1, 3).astype(_MX)


def kernel(x_prompt, x_sample, cache_k, cache_v, page_table, state_win_k, state_win_v, state_pool, state_shift, state_wkv, p_prompt, p_sample, norm_mix, norm_mlp, norm_ple, mlp_w1, mlp_w2, ple_proj, ple_gate, even_w_in, even_w_out, pool_w, pool_scale, q_gain, k_gain, cmp_pe, cmp_w, rwkv_mu, rwkv_wr, rwkv_wk, rwkv_wv, rwkv_wo, rwkv_w0, rwkv_w1, rwkv_w2, rwkv_a0, rwkv_a1, rwkv_a2, rwkv_v0, rwkv_v1, rwkv_v2, rwkv_g1, rwkv_g2, rwkv_kk, rwkv_ka, rwkv_rk, rwkv_lnw, rwkv_lnb):
    nb, t, d = x_prompt.shape
    ns = x_sample.shape[0]
    depth = norm_mix.shape[0]
    n_even = even_w_in.shape[0]
    n_pages = page_table.shape[1]
    past_len = n_pages * PAGE
    wbuf = state_win_k.shape[2]
    n_phys = cache_k.shape[1]
    mp = nb * t
    tm_p = _row_tile(mp, 512)
    tm_a = _row_tile(mp, 256)
    slopes = jnp.asarray(_alibi_slopes(KVH * REP))
    mx = lambda a: a.astype(_MX)
    row1 = lambda a: a.reshape(1, -1)
    two = lambda a: jnp.tile(a.reshape(1, HD), (1, 2))

    rid = np.arange(16)
    real = (rid % 4) < REP
    slopes16 = jnp.asarray(np.where(real, _alibi_slopes(KVH * REP)[np.minimum((rid // 4) * REP + rid % 4, 11)],
                                    0.0).astype(np.float32))[:, None] * jnp.ones((1, 128), F32)
    grp16 = jnp.asarray(((rid[:, None] // 4 == rid[None, :] // 4) & real[None, :]).astype(np.float32))

    cache_k5 = cache_k.reshape(n_even, n_phys, PAGE, 2 * KV_DIM)
    cache_v5 = cache_v.reshape(n_even, n_phys, PAGE, 2 * KV_DIM)
    win_k4 = state_win_k.reshape(n_even, ns, wbuf, KV_DIM)
    win_v4 = state_win_v.reshape(n_even, ns, wbuf, KV_DIM)

    h_p = x_prompt.reshape(mp, d)
    h_s = x_sample.reshape(ns, d)
    outs = {k_: [] for k_ in ('nk_p', 'nv_p', 'nk_s', 'nv_s', 'wk_p', 'wv_p', 'wk_s', 'wv_s', 'pl_p', 'pl_s',
                              'sh_p', 'sh_s', 'st_p', 'st_s')}
    vf_p = vf_s = None

    for i in range(depth):
        gn = row1(norm_mix[i])
        if i % 2 == 0:
            e = i // 2
            w_in = mx(jnp.pad(even_w_in[e], ((0, 0), (0, IN_PAD - even_w_in.shape[2]))))
            wbd = mx(_block_diag(pool_w[e]))
            psc = row1(pool_scale[e])
            qg, ksg, kwg, kcg = two(q_gain[e]), two(k_gain[e, 1]), two(k_gain[e, 2]), two(k_gain[e, 0])
            w_out = even_w_out[e]
            wo_pool = mx(w_out[:POOL_DIM])
            wo_nsa = mx(jnp.pad(w_out[POOL_DIM:].reshape(KVH, REP * HD, d), ((0, 0), (0, 256 - REP * HD), (0, 0))))
            pe_k = jnp.tile(cmp_pe[e, 0], (1, 2))
            pe_v = jnp.tile(cmp_pe[e, 1], (1, 2))
            eye = jnp.eye(2, dtype=F32)
            bd = lambda w: mx((eye[None, :, None, :, None] * w[:, None, :, None, :]).reshape(CMP_LEN, 128, 128))
            cw_k, cw_v = bd(cmp_w[e, 0]), bd(cmp_w[e, 1])

            u, q, nk, nv, kw, vw, gate = _inproj(h_p, gn, w_in, qg, ksg, kwg, tm_a, HD ** -0.5 * LOG2E)
            y_pool = _pool_prompt(u.reshape(nb, t, POOL_DIM), wbd, psc, _row_tile(t, 512))
            nh_p = t // CMP_STRIDE
            pt_p = jnp.arange(nb * (t // PAGE), dtype=jnp.int32).reshape(nb, t // PAGE)
            kc, vc = _compress(nk.reshape(1, mp // PAGE, PAGE, 2 * KV_DIM), nv.reshape(1, mp // PAGE, PAGE, 2 * KV_DIM),
                               pt_p, 0, pe_k, pe_v, cw_k, cw_v, kcg)
            n_cmp_p = nh_p - (CMP_LEN // CMP_STRIDE - 1)
            n_slc_p = t // SLC_LEN
            gates_t = gate[:, :KVH * REP * 3].reshape(nb, t, KVH, REP * 3).transpose(0, 2, 3, 1)
            chunks_t = lambda x: (x.reshape(nb, t // KEY_TILE, KEY_TILE, KVH, HD).transpose(0, 3, 1, 4, 2).astype(_MX))
            o4t = _nsa_prompt(slopes, _heads(q, nb, t), _heads(kc, nb, nh_p),
                              vc.reshape(nb, nh_p, KVH, HD).transpose(0, 2, 3, 1).astype(_MX),
                              _heads(nk[:, KV_DIM:], nb, t), chunks_t(nv[:, KV_DIM:]),
                              _heads(kw, nb, t), chunks_t(vw), gates_t,
                              _imp_matrix(nh_p, n_cmp_p, n_slc_p, -(-n_slc_p // 8) * 8).T, n_cmp_p)
            o4 = o4t.transpose(0, 1, 3, 2).reshape(KVH, mp, 256)
            h_p = _outproj(h_p, y_pool.reshape(mp, POOL_DIM), o4, wo_pool, wo_nsa, tm_p)
            outs['nk_p'].append(nk.reshape(nb, t, 2, KVH, HD))
            outs['nv_p'].append(nv.reshape(nb, t, 2, KVH, HD))
            kw3 = kw.reshape(nb, t, KVH, HD)
            vw3 = vw.reshape(nb, t, KVH, HD)
            if t < wbuf:
                zpad = jnp.zeros((nb, wbuf - t, KVH, HD), F32)
                kw3, vw3 = jnp.concatenate([zpad, kw3], 1), jnp.concatenate([zpad, vw3], 1)
            outs['wk_p'].append(kw3[:, -wbuf:])
            outs['wv_p'].append(vw3[:, -wbuf:])
            outs['pl_p'].append(u.reshape(nb, t, POOL_DIM)[:, -POOL_BUF:])

            u, q, nk, nv, kw, vw, gate = _inproj(h_s, gn, w_in, qg, ksg, kwg, ns, HD ** -0.5)
            y_pool = _pool_step(state_pool[e], u, wbd, psc, past_len)
            kc, vc = _compress(cache_k5, cache_v5, page_table, e, pe_k, pe_v, cw_k, cw_v, kcg)
            nh_s = kc.shape[1]
            n_cmp_s = nh_s - (CMP_LEN // CMP_STRIDE - 1)
            n_slc_s = -(-(past_len + 1) // SLC_LEN)
            q16 = jnp.pad(q.reshape(ns, KVH, REP, HD), ((0, 0), (0, 0), (0, 1), (0, 0))).reshape(ns, 16, 1, HD)
            gsel = jnp.asarray((np.arange(16)[:, None] // 4 == np.arange(KVH)[None, :]).astype(np.float32))
            qbd = (q16.astype(F32) * gsel[None, :, :, None]).reshape(ns, 16, KV_DIM).astype(_MX)
            new_rows = jnp.pad(jnp.stack([nk[:, KV_DIM:], nv[:, KV_DIM:], kw, vw], axis=1), ((0, 0), (0, 4), (0, 0)))
            g16 = jnp.pad(gate[:, :KVH * REP * 3].reshape(ns, KVH, REP, 3), ((0, 0), (0, 0), (0, 1), (0, 125)))
            o16 = _nsa_sample(page_table, e, qbd, mx(kc), mx(vc), cache_k5, cache_v5, new_rows, win_k4, win_v4,
                              g16.reshape(ns, 16, 128), slopes16, grp16,
                              _imp_matrix(nh_s, n_cmp_s, n_slc_s, _pad_lanes(n_slc_s)), n_cmp_s, n_slc_s)
            o4 = o16.reshape(ns, KVH, 4, HD)[:, :, :REP].reshape(ns, KVH, REP * HD).transpose(1, 0, 2)
            o4 = jnp.pad(o4, ((0, 0), (0, 0), (0, 256 - REP * HD)))
            h_s = _outproj(h_s, y_pool, o4, wo_pool, wo_nsa, ns)
            outs['nk_s'].append(nk.reshape(ns, 1, 2, KVH, HD))
            outs['nv_s'].append(nv.reshape(ns, 1, 2, KVH, HD))
            outs['wk_s'].append(jnp.concatenate([state_win_k[e], kw.reshape(ns, 1, KVH, HD)], axis=1)[:, -wbuf:])
            outs['wv_s'].append(jnp.concatenate([state_win_v[e], vw.reshape(ns, 1, KVH, HD)], axis=1)[:, -wbuf:])
            outs['pl_s'].append(jnp.concatenate([state_pool[e], u[:, None]], axis=1)[:, -POOL_BUF:])
        else:
            o = i // 2
            lora_in = lambda a: mx(jnp.pad(a, ((0, 0), (0, LORA_PAD - a.shape[1]))))
            lora_out = lambda a: mx(jnp.pad(a, ((0, LORA_PAD - a.shape[0]), (0, 0))))
            p = dict(mu=rwkv_mu[o], wr=mx(rwkv_wr[o]), wk=mx(rwkv_wk[o]), wv=mx(rwkv_wv[o]),
                     w1=lora_in(rwkv_w1[o]), w2=lora_out(rwkv_w2[o]), a1=lora_in(rwkv_a1[o]), a2=lora_out(rwkv_a2[o]),
                     g1=lora_in(rwkv_g1[o]), g2=lora_out(rwkv_g2[o]), w0=row1(rwkv_w0[o]), a0=row1(rwkv_a0[o]),
                     k_k=row1(rwkv_kk[o]), k_a=row1(rwkv_ka[o]))
            vparams = None if o == 0 else (row1(rwkv_v0[o - 1]), lora_in(rwkv_v1[o - 1]), lora_out(rwkv_v2[o - 1]))
            lnw, lnb, rk, wo = row1(rwkv_lnw[o]), row1(rwkv_lnb[o]), row1(rwkv_rk[o]), mx(rwkv_wo[o])
            nhd = d // HD

            xn = _norm(h_p, gn, tm_p)
            xn3 = xn.reshape(nb, t, d)
            xprev = jnp.concatenate([jnp.zeros((nb, 1, d), F32), xn3[:, :-1]], axis=1).reshape(mp, d)
            vres = None if o == 0 else (vf_p,) + vparams
            r, w, k, v, kk, b, g = _rwkv_proj(xn, xprev, p, vres, tm_a)
            if o == 0:
                vf_p = v
            seq = lambda a: a.reshape(nb, t, d)
            o_seq, s_fin = _wkv_scan(seq(r), seq(w), seq(k), seq(v), seq(kk), seq(b),
                                     jnp.zeros((nb, nhd, HD, HD), F32), WKV_CHUNK, WKV_HEADS)
            o_tok = o_seq.reshape(mp, d)
            h_p = _rwkv_out(h_p, o_tok, r, k, v, g, lnw, lnb, rk, wo, tm_a)
            outs['sh_p'].append(xn3[:, -1])
            outs['st_p'].append(s_fin)

            xn = _norm(h_s, gn, ns)
            vres = None if o == 0 else (vf_s,) + vparams
            r, w, k, v, kk, b, g = _rwkv_proj(xn, state_shift[o], p, vres, ns)
            if o == 0:
                vf_s = v
            rows = lambda a: a.reshape(ns, nhd, 1, HD)
            o_col, s_fin = _wkv_step(rows(r), rows(w), rows(k), rows(kk), rows(b), v.reshape(ns, nhd, HD, 1),
                                     state_wkv[o])
            h_s = _rwkv_out(h_s, o_col.reshape(ns, d), r, k, v, g, lnw, lnb, rk, wo, ns)
            outs['sh_s'].append(xn)
            outs['st_s'].append(s_fin)

        w1, w2 = mx(mlp_w1[i]), mx(mlp_w2[i])
        gm, gp, wg, wp = row1(norm_mlp[i]), row1(norm_ple[i]), mx(ple_gate[i]), mx(ple_proj[i])
        h_p = _mlp(h_p, gm, w1, w2, tm_p, 1024)
        h_p = _ple(h_p, p_prompt[i].reshape(mp, -1), gp, wg, wp, tm_p)
        h_s = _mlp(h_s, gm, w1, w2, ns, 1024)
        h_s = _ple(h_s, p_sample[i].reshape(ns, -1), gp, wg, wp, ns)

    st = lambda name: jnp.stack(outs[name])
    return (h_p.reshape(nb, t, d), h_s.reshape(ns, 1, d), st('nk_p'), st('nv_p'), st('nk_s'), st('nv_s'),
            st('wk_p'), st('wv_p'), st('wk_s'), st('wv_s'), st('pl_p'), st('pl_s'),
            st('sh_p'), st('sh_s'), st('st_p'), st('st_s'))
```
